```python
import jax
import jax.numpy as jnp
from jax import lax
import numpy as np

D_MODEL = 1024
BATCH = 32
SEQ = 256
DEPTH = 2
DEC_BATCH = 8
DEC_SEQ = 1024
PAST_LEN = 512

GRID_W = 64
HEAD_DIM = 64
ROPE_BASE = 10000.0
QBLOCK = 128
A_HEADS = 8
A_KV_HEADS = 2
WINDOW = 128
B_HEADS = 8
B_DK = 64
B_DV = 64
RET_CHUNK = 128
C_HEADS = 16
NB_ROWS = 8
NB_COLS = 16
D_FF = 2816
N_EXPERTS = 8
TOP_K = 2
D_FF_EXPERT = 2816
N_EVEN = (DEPTH + 1) // 2
N_ODD = DEPTH // 2
EPS = 1e-6
NEG = -1e30
A_Q = A_HEADS * HEAD_DIM
A_KV = A_KV_HEADS * HEAD_DIM
B_QK = B_HEADS * B_DK
B_V = B_HEADS * B_DV
IN_EVEN = A_Q + 2 * A_KV + 2 * B_QK + 2 * B_V
MIX_EVEN = A_Q + B_V
C_W = C_HEADS * HEAD_DIM

kernel_name = 'hybrid_diffusion_prefix_step'


def _rmsnorm(x, g):
    xf = x.astype(jnp.float32)
    y = xf * lax.rsqrt(jnp.mean(xf * xf, axis=-1, keepdims=True) + EPS)
    return (y * g.astype(jnp.float32)).astype(x.dtype)


def _adaln(cvec, w, b):
    m = (jax.nn.silu(cvec) @ w + b).reshape(-1, 1, 6 * D_MODEL)
    return jnp.split(m, 6, axis=-1)


def _modulate(h, shift, scale):
    return h * (1 + scale) + shift


def _rotate_half(x, ang):
    f = ang.shape[-1]
    xf = x.astype(jnp.float32)
    x1, x2 = xf[..., :f], xf[..., f:]
    cos = jnp.cos(ang)[:, None, :]
    sin = jnp.sin(ang)[:, None, :]
    return jnp.concatenate([x1 * cos - x2 * sin, x1 * sin + x2 * cos], axis=-1).astype(x.dtype)


def _rope2d(x):
    T = x.shape[1]
    half = HEAD_DIM // 2
    nf = half // 2
    t = jnp.arange(T)
    row = (t // GRID_W).astype(jnp.float32)
    col = (t % GRID_W).astype(jnp.float32)
    inv = ROPE_BASE ** (-jnp.arange(nf, dtype=jnp.float32) / nf)
    return jnp.concatenate([_rotate_half(x[..., :half], row[:, None] * inv),
                            _rotate_half(x[..., half:], col[:, None] * inv)], axis=-1)


def _softmax_sink(s, sink):
    if sink is None:
        return jax.nn.softmax(s, axis=-1)
    col = jnp.broadcast_to(sink, s.shape[:-1] + (1,))
    return jax.nn.softmax(jnp.concatenate([col, s], axis=-1), axis=-1)[..., 1:]


def _context_attention(q, k, v, sink):
    B, L, H, d = q.shape
    G = k.shape[2]
    rep = H // G
    nb = L // QBLOCK
    qb = q.reshape(B, nb, QBLOCK, G, rep, d).transpose(1, 0, 2, 3, 4, 5)
    sink_b = None if sink is None else sink.astype(jnp.float32).reshape(1, G, rep, 1, 1)

    def block(qblk):
        s = jnp.einsum('bqgrd,bkgd->bgrqk', qblk, k, preferred_element_type=jnp.float32) * d ** -0.5
        p = _softmax_sink(s, sink_b)
        return jnp.einsum('bgrqk,bkgd->bqgrd', p.astype(v.dtype), v)

    o = lax.map(block, qb)
    return o.transpose(1, 0, 2, 3, 4, 5).reshape(B, L, H * d)


def _window_attention(q, k, v, ck, cv, sink):
    B, T, H, d = q.shape
    G = k.shape[2]
    rep = H // G
    nb = T // QBLOCK
    Lc = ck.shape[1]

    def band(x):
        xp = jnp.pad(x, ((0, 0), (QBLOCK, QBLOCK), (0, 0), (0, 0))).reshape(B, nb + 2, QBLOCK, G, d)
        return jnp.concatenate([xp[:, :-2], xp[:, 1:-1], xp[:, 2:]], axis=2)

    kb, vb = band(k), band(v)
    qb = q.reshape(B, nb, QBLOCK, G, rep, d)
    scale = d ** -0.5
    s_loc = jnp.einsum('bnqgrd,bnkgd->bngrqk', qb, kb, preferred_element_type=jnp.float32) * scale
    s_ctx = jnp.einsum('bnqgrd,bkgd->bngrqk', qb, ck, preferred_element_type=jnp.float32) * scale
    n = jnp.arange(nb)[:, None, None]
    qpos = n * QBLOCK + jnp.arange(QBLOCK)[None, :, None]
    kpos = n * QBLOCK - QBLOCK + jnp.arange(3 * QBLOCK)[None, None, :]
    mask = (jnp.abs(kpos - qpos) <= WINDOW) & (kpos >= 0) & (kpos < T)
    s_loc = jnp.where(mask[None, :, None, None], s_loc, NEG)
    sink_b = sink.astype(jnp.float32).reshape(1, 1, G, rep, 1, 1)
    p = _softmax_sink(jnp.concatenate([s_ctx, s_loc], axis=-1), sink_b).astype(v.dtype)
    o = (jnp.einsum('bngrqk,bkgd->bnqgrd', p[..., :Lc], cv)
         + jnp.einsum('bngrqk,bnkgd->bnqgrd', p[..., Lc:], vb))
    return o.reshape(B, T, H * d)


def _retention_scan(q, k, v, log_g, s0):
    B, H, T, dk = q.shape
    dv = v.shape[-1]
    C = RET_CHUNK
    n = T // C
    idx = jnp.arange(C, dtype=jnp.float32)
    lg = log_g[:, None]
    q_dec = jnp.exp(lg * (idx + 1.0))[:, :, None]
    k_dec = jnp.exp(lg * (C - 1.0 - idx))[:, :, None]
    diff = idx[:, None] - idx[None, :]
    dmask = jnp.where(diff >= 0, jnp.exp(log_g[:, None, None] * jnp.maximum(diff, 0.0)), 0.0)
    chunk_dec = jnp.exp(log_g * C)[:, None, None]

    def chunks(x):
        return x.reshape(B, H, n, C, x.shape[-1]).transpose(2, 0, 1, 3, 4)

    def step(S, inp):
        qc, kc, vc = inp
        att = jnp.einsum('bhqd,bhkd->bhqk', qc, kc) * dmask
        o = jnp.einsum('bhqk,bhkv->bhqv', att, vc) + jnp.einsum('bhqd,bhdv->bhqv', qc * q_dec, S)
        S = S * chunk_dec + jnp.einsum('bhkd,bhkv->bhdv', kc * k_dec, vc)
        return S, o

    S, o = lax.scan(step, s0, (chunks(q), chunks(k), chunks(v)))
    return o.transpose(1, 2, 0, 3, 4).reshape(B, H, T, dv), S


def _retention_branch(qr, kr, vr, gr, lg_f, lg_b, s0f, s0b):
    B, L, _ = qr.shape

    def heads(x, dh):
        return x.reshape(B, L, B_HEADS, dh).transpose(0, 2, 1, 3).astype(jnp.float32)

    q = heads(qr, B_DK)
    k = heads(kr, B_DK) * B_DK ** -0.5
    v = heads(vr, B_DV)
    lf = jax.nn.log_sigmoid(lg_f.astype(jnp.float32))
    lb = jax.nn.log_sigmoid(lg_b.astype(jnp.float32))
    of, sf = _retention_scan(q, k, v, lf, s0f.astype(jnp.float32))
    ob, sb = _retention_scan(q[:, :, ::-1], k[:, :, ::-1], v[:, :, ::-1], lb, s0b.astype(jnp.float32))
    o = of + ob[:, :, ::-1]
    mu = jnp.mean(o, axis=-1, keepdims=True)
    var = jnp.mean(jnp.square(o - mu), axis=-1, keepdims=True)
    o = (o - mu) * lax.rsqrt(var + EPS)
    o = o.transpose(0, 2, 1, 3).reshape(B, L, B_V).astype(gr.dtype)
    return jax.nn.silu(gr) * o, sf.astype(qr.dtype), sb.astype(qr.dtype)


def _even_project(h, w_in):
    B, L, _ = h.shape
    sizes = [A_Q, A_KV, A_KV, B_QK, B_QK, B_V, B_V]
    offs = [int(o) for o in np.cumsum(sizes)[:-1]]
    qa, ka, va, qr, kr, vr, gr = jnp.split(h @ w_in, offs, axis=-1)
    qa = qa.reshape(B, L, A_HEADS, HEAD_DIM)
    ka = ka.reshape(B, L, A_KV_HEADS, HEAD_DIM)
    va = va.reshape(B, L, A_KV_HEADS, HEAD_DIM)
    return qa, ka, va, qr, kr, vr, gr


def _even_mixer_context(h, w_in, w_out, sink, lg_f, lg_b):
    B = h.shape[0]
    qa, ka, va, qr, kr, vr, gr = _even_project(h, w_in)
    oa = _context_attention(qa, ka, va, sink)
    zero = jnp.zeros((B, B_HEADS, B_DK, B_DV), jnp.float32)
    orr, sf, sb = _retention_branch(qr, kr, vr, gr, lg_f, lg_b, zero, zero)
    out = jnp.concatenate([oa, orr], axis=-1) @ w_out
    return out, ka, va, sf, sb


def _even_mixer_latent(h, ck, cv, s0f, s0b, w_in, w_out, sink, lg_f, lg_b):
    qa, ka, va, qr, kr, vr, gr = _even_project(h, w_in)
    oa = _window_attention(_rope2d(qa), _rope2d(ka), va, ck, cv, sink)
    orr, _, _ = _retention_branch(qr, kr, vr, gr, lg_f, lg_b, s0f, s0b)
    return jnp.concatenate([oa, orr], axis=-1) @ w_out


def _odd_project(h, w_qkv):
    B, L, _ = h.shape
    q, k, v = jnp.split(h @ w_qkv, 3, axis=-1)
    shp = (B, L, C_HEADS, HEAD_DIM)
    return q.reshape(shp), k.reshape(shp), v.reshape(shp)


def _odd_mixer_context(h, w_qkv, w_out):
    q, k, v = _odd_project(h, w_qkv)
    return _context_attention(q, k, v, None) @ w_out, k, v


def _neighborhood_attention(q, k, v, ck, cv, rpb):
    B, T, H, d = q.shape
    R = T // GRID_W
    WR = min(NB_ROWS, R)
    Lc = ck.shape[1]
    r = jnp.arange(R)
    krow = jnp.clip(r - WR // 2, 0, R - WR)[:, None] + jnp.arange(WR)[None, :]
    cq = jnp.arange(GRID_W)
    cstart = jnp.clip(cq - NB_COLS // 2, 0, GRID_W - NB_COLS)
    col_ok = (cq[None, :] >= cstart[:, None]) & (cq[None, :] < cstart[:, None] + NB_COLS)
    qg = q.reshape(B, R, GRID_W, H, d)
    kg = k.reshape(B, R, GRID_W, H, d)[:, krow]
    vg = v.reshape(B, R, GRID_W, H, d)[:, krow]
    scale = d ** -0.5
    s_nb = jnp.einsum('brqhd,brjkhd->brhqjk', qg, kg, preferred_element_type=jnp.float32) * scale
    dr = krow - r[:, None] + NB_ROWS - 1
    dc = jnp.clip(cq[None, :] - cq[:, None], 1 - NB_COLS, NB_COLS - 1) + NB_COLS - 1
    bias = rpb.astype(jnp.float32)[:, dr[:, None, :, None], dc[None, :, None, :]]
    s_nb = s_nb + bias.transpose(1, 0, 2, 3, 4)[None]
    s_nb = jnp.where(col_ok[:, None, :], s_nb, NEG).reshape(B, R, H, GRID_W, WR * GRID_W)
    s_ctx = jnp.einsum('brqhd,bkhd->brhqk', qg, ck, preferred_element_type=jnp.float32) * scale
    p = jax.nn.softmax(jnp.concatenate([s_ctx, s_nb], axis=-1), axis=-1).astype(v.dtype)
    p_nb = p[..., Lc:].reshape(B, R, H, GRID_W, WR, GRID_W)
    o = (jnp.einsum('brhqk,bkhd->brqhd', p[..., :Lc], cv)
         + jnp.einsum('brhqjk,brjkhd->brqhd', p_nb, vg))
    return o.reshape(B, T, H * d)


def _odd_mixer_latent(h, ck, cv, w_qkv, w_out, rpb):
    q, k, v = _odd_project(h, w_qkv)
    return _neighborhood_attention(q, k, v, ck, cv, rpb) @ w_out


def _swiglu(h, w_gu, w_down):
    g, u = jnp.split(h @ w_gu, 2, axis=-1)
    return (jax.nn.silu(g) * u) @ w_down


def _moe(h, w_router, w_gu, w_down):
    logits = jnp.einsum('bld,de->ble', h, w_router, preferred_element_type=jnp.float32)
    top_v, top_i = lax.top_k(logits, TOP_K)
    gates = jax.nn.softmax(top_v, axis=-1)
    comb = jnp.sum(jax.nn.one_hot(top_i, N_EXPERTS, dtype=jnp.float32) * gates[..., None], axis=-2)
    comb = comb.astype(h.dtype)
    out = jnp.zeros_like(h)
    for e in range(N_EXPERTS):
        out = out + comb[..., e:e + 1] * _swiglu(h, w_gu[e], w_down[e])
    return out


def setup_inputs(seed: int = 0) -> dict:
    key = jax.random.key(seed)
    ks = jax.random.split(key, 32)

    def nrm(i, shape, scale):
        return jax.random.normal(ks[i], shape, jnp.float32) * scale

    D = D_MODEL
    base = jnp.asarray(np.log(2.0 ** (5 + np.arange(B_HEADS)) - 1.0), jnp.float32)
    return {
        'x_prompt': nrm(0, (BATCH, SEQ, D), 1.0),
        'x_sample': nrm(1, (DEC_BATCH, DEC_SEQ, D), 1.0),
        'cache_a_k': nrm(2, (DEC_BATCH, N_EVEN, PAST_LEN, A_KV_HEADS, HEAD_DIM), 1.0),
        'cache_a_v': nrm(3, (DEC_BATCH, N_EVEN, PAST_LEN, A_KV_HEADS, HEAD_DIM), 1.0),
        'state_ret_fwd': nrm(4, (DEC_BATCH, N_EVEN, B_HEADS, B_DK, B_DV), 1.0),
        'state_ret_bwd': nrm(5, (DEC_BATCH, N_EVEN, B_HEADS, B_DK, B_DV), 1.0),
        'cache_c_k': nrm(6, (DEC_BATCH, N_ODD, PAST_LEN, C_HEADS, HEAD_DIM), 1.0),
        'cache_c_v': nrm(7, (DEC_BATCH, N_ODD, PAST_LEN, C_HEADS, HEAD_DIM), 1.0),
        'c': nrm(8, (DEC_BATCH, D), 1.0),
        'c_ctx': nrm(9, (D,), 1.0),
        'mod_w': nrm(10, (DEPTH, D, 6 * D), D ** -0.5),
        'mod_b': nrm(11, (DEPTH, 6 * D), 0.02),
        'norm1_g': 1.0 + nrm(12, (DEPTH, D), 0.02),
        'norm2_g': 1.0 + nrm(13, (DEPTH, D), 0.02),
        'final_g': 1.0 + nrm(14, (D,), 0.02),
        'ev_w_in': nrm(15, (N_EVEN, D, IN_EVEN), D ** -0.5),
        'ev_w_out': nrm(16, (N_EVEN, MIX_EVEN, D), MIX_EVEN ** -0.5),
        'ev_sink': nrm(17, (N_EVEN, A_HEADS), 0.5),
        'ev_ret_logit_fwd': base + nrm(18, (N_EVEN, B_HEADS), 0.1),
        'ev_ret_logit_bwd': base + nrm(19, (N_EVEN, B_HEADS), 0.1),
        'ev_ffn_w_gu': nrm(20, (N_EVEN, D, 2 * D_FF), D ** -0.5),
        'ev_ffn_w_down': nrm(21, (N_EVEN, D_FF, D), D_FF ** -0.5),
        'od_w_qkv': nrm(22, (N_ODD, D, 3 * C_W), D ** -0.5),
        'od_w_out': nrm(23, (N_ODD, C_W, D), C_W ** -0.5),
        'od_rpb': nrm(24, (N_ODD, C_HEADS, 2 * NB_ROWS - 1, 2 * NB_COLS - 1), 0.1),
        'od_w_router': nrm(25, (N_ODD, D, N_EXPERTS), D ** -0.5),
        'od_moe_w_gu': nrm(26, (N_ODD, N_EXPERTS, D, 2 * D_FF_EXPERT), D ** -0.5),
        'od_moe_w_down': nrm(27, (N_ODD, N_EXPERTS, D_FF_EXPERT, D), D_FF_EXPERT ** -0.5),
    }


def reference(x_prompt, x_sample, cache_a_k, cache_a_v, state_ret_fwd, state_ret_bwd, cache_c_k, cache_c_v,
              c, c_ctx, mod_w, mod_b, norm1_g, norm2_g, final_g, ev_w_in, ev_w_out, ev_sink,
              ev_ret_logit_fwd, ev_ret_logit_bwd, ev_ffn_w_gu, ev_ffn_w_down, od_w_qkv, od_w_out, od_rpb,
              od_w_router, od_moe_w_gu, od_moe_w_down):
    xp, xs = x_prompt, x_sample
    a_k, a_v, r_f, r_b, c_k, c_v = [], [], [], [], [], []
    for i in range(DEPTH):
        j = i // 2
        mp = _adaln(c_ctx, mod_w[i], mod_b[i])
        ms = _adaln(c, mod_w[i], mod_b[i])
        hp = _modulate(_rmsnorm(xp, norm1_g[i]), mp[0], mp[1])
        hs = _modulate(_rmsnorm(xs, norm1_g[i]), ms[0], ms[1])
        if i % 2 == 0:
            op, ka, va, sf, sb = _even_mixer_context(hp, ev_w_in[j], ev_w_out[j], ev_sink[j],
                                                     ev_ret_logit_fwd[j], ev_ret_logit_bwd[j])
            os_ = _even_mixer_latent(hs, cache_a_k[:, j], cache_a_v[:, j], state_ret_fwd[:, j],
                                     state_ret_bwd[:, j], ev_w_in[j], ev_w_out[j], ev_sink[j],
                                     ev_ret_logit_fwd[j], ev_ret_logit_bwd[j])
            a_k.append(ka)
            a_v.append(va)
            r_f.append(sf)
            r_b.append(sb)
        else:
            op, kc, vc = _odd_mixer_context(hp, od_w_qkv[j], od_w_out[j])
            os_ = _odd_mixer_latent(hs, cache_c_k[:, j], cache_c_v[:, j], od_w_qkv[j], od_w_out[j], od_rpb[j])
            c_k.append(kc)
            c_v.append(vc)
        xp = xp + mp[2] * op
        xs = xs + ms[2] * os_
        hp = _modulate(_rmsnorm(xp, norm2_g[i]), mp[3], mp[4])
        hs = _modulate(_rmsnorm(xs, norm2_g[i]), ms[3], ms[4])
        if i % 2 == 0:
            fp = _swiglu(hp, ev_ffn_w_gu[j], ev_ffn_w_down[j])
            fs = _swiglu(hs, ev_ffn_w_gu[j], ev_ffn_w_down[j])
        else:
            fp = _moe(hp, od_w_router[j], od_moe_w_gu[j], od_moe_w_down[j])
            fs = _moe(hs, od_w_router[j], od_moe_w_gu[j], od_moe_w_down[j])
        xp = xp + mp[5] * fp
        xs = xs + ms[5] * fs
    y_prompt = _rmsnorm(xp, final_g)
    y_sample = _rmsnorm(xs, final_g)
    new_a_k = jnp.stack(a_k, axis=1)
    new_a_v = jnp.stack(a_v, axis=1)
    new_ret_fwd = jnp.stack(r_f, axis=1)
    new_ret_bwd = jnp.stack(r_b, axis=1)
    new_c_k = jnp.stack(c_k, axis=1)
    new_c_v = jnp.stack(c_v, axis=1)
    return (y_prompt, y_sample, new_a_k, new_a_v, new_ret_fwd, new_ret_bwd, new_c_k, new_c_v)
```

```python
import functools

import numpy as np
import jax
import jax.numpy as jnp
from jax import lax
from jax.experimental import pallas as pl
from jax.experimental.pallas import tpu as pltpu

f32 = jnp.float32
bf16 = jnp.bfloat16

D = 1024
HEAD_DIM = 64
GRID_W = 64
ROPE_BASE = 10000.0
A_HEADS = 8
A_KV_HEADS = 2
WINDOW = 128
B_HEADS = 8
RET_CHUNK = 128
C_HEADS = 16
NB_ROWS = 8
NB_COLS = 16
N_EXPERTS = 8
EPS = 1e-6
NEG = -1e30

LANES = 128
MOD_ROWS = 16
VMEM_LIMIT_MB = 56


def _cparams(sem, vmem_mb=VMEM_LIMIT_MB):
    return pltpu.CompilerParams(dimension_semantics=sem, vmem_limit_bytes=vmem_mb * 1024 * 1024)


def _dot(a, b):
    return jnp.dot(a, b, preferred_element_type=f32)


def _dot_nt(a, b):
    return lax.dot_general(a, b, (((1,), (1,)), ((), ())), preferred_element_type=f32)


def _dot_tn(a, b):
    return lax.dot_general(a, b, (((0,), (0,)), ((), ())), preferred_element_type=f32)


def _lane_lo():
    return lax.broadcasted_iota(jnp.int32, (1, LANES), 1) < HEAD_DIM


def _swap_halves(x):
    return pltpu.roll(x, HEAD_DIM, 1)


def _mod_spec(layer, which, tm, rows_per_batch, is_sample):
    def imap(i, *_):
        r = (1 + (i * tm) // rows_per_batch) if is_sample else 0
        return ((layer * MOD_ROWS + r) * 6 + which, 0, 0)

    return pl.BlockSpec((1, 1, D), imap)


def _adaln_kernel(cv_ref, w_ref, b_ref, o_ref):
    cv = cv_ref[...]
    s = (cv * jax.nn.sigmoid(cv)).astype(bf16)
    o_ref[0] = _dot(s, w_ref[0].astype(bf16)) + b_ref[0]


def _adaln(cvecs, mod_w, mod_b):
    depth = mod_w.shape[0]
    tn = 1024
    out = pl.pallas_call(
        _adaln_kernel,
        grid=(depth, 6 * D // tn),
        in_specs=[pl.BlockSpec((MOD_ROWS, D), lambda l, j: (0, 0)),
                  pl.BlockSpec((1, D, tn), lambda l, j: (l, 0, j)),
                  pl.BlockSpec((1, 1, tn), lambda l, j: (l, 0, j))],
        out_specs=pl.BlockSpec((1, MOD_ROWS, tn), lambda l, j: (l, 0, j)),
        out_shape=jax.ShapeDtypeStruct((depth, MOD_ROWS, 6 * D), f32),
        compiler_params=_cparams(("arbitrary", "arbitrary")),
        name="adaln",
    )(cvecs, mod_w, mod_b.reshape(depth, 1, 6 * D))
    return out.reshape(depth * MOD_ROWS * 6, 1, D)


def _norm_mod(x, g, shift, scale):
    y = x * lax.rsqrt(jnp.mean(x * x, axis=-1, keepdims=True) + EPS) * g
    return y * (1.0 + scale) + shift


def _proj_kernel(x_ref, g_ref, sh_ref, sc_ref, w_ref, *out_refs, splits, chunk):
    h = _norm_mod(x_ref[...], g_ref[...], sh_ref[0], sc_ref[0]).astype(bf16)
    off = 0
    for o_ref, width in zip(out_refs, splits):
        for c0 in range(0, width, chunk):
            cw = min(chunk, width - c0)
            o_ref[:, c0:c0 + cw] = _dot(h, w_ref[:, off + c0:off + c0 + cw]).astype(o_ref.dtype)
        off += width


def _proj(x, g, mod3, layer, w, splits, dtypes, rows_per_batch, is_sample, tm=512):
    m = x.shape[0]
    tm = min(tm, m)
    n = w.shape[1]
    kern = functools.partial(_proj_kernel, splits=tuple(splits), chunk=512)
    return pl.pallas_call(
        kern,
        grid=(m // tm,),
        in_specs=[pl.BlockSpec((tm, D), lambda i: (i, 0)),
                  pl.BlockSpec((1, D), lambda i: (0, 0)),
                  _mod_spec(layer, 0, tm, rows_per_batch, is_sample),
                  _mod_spec(layer, 1, tm, rows_per_batch, is_sample),
                  pl.BlockSpec((D, n), lambda i: (0, 0))],
        out_specs=[pl.BlockSpec((tm, s), lambda i: (i, 0)) for s in splits],
        out_shape=[jax.ShapeDtypeStruct((m, s), dt) for s, dt in zip(splits, dtypes)],
        compiler_params=_cparams(("arbitrary",)),
        name="norm_proj",
    )(x, g, mod3, mod3, w)


def _softmax_parts(parts, sink):
    m = parts[0].max(axis=-1, keepdims=True)
    for s in parts[1:]:
        m = jnp.maximum(m, s.max(axis=-1, keepdims=True))
    if sink is not None:
        m = jnp.maximum(m, sink)
    es = [jnp.exp(s - m) for s in parts]
    den = es[0].sum(axis=-1, keepdims=True)
    for e in es[1:]:
        den = den + e.sum(axis=-1, keepdims=True)
    if sink is not None:
        den = den + jnp.exp(sink - m)
    inv = 1.0 / den
    return [(e * inv).astype(bf16) for e in es]


def _ctx_attn_kernel(sink_ref, q_ref, k_ref, v_ref, o_ref, *, heads, kv_heads, has_sink):
    rep = heads // kv_heads
    lo = _lane_lo()
    scale = HEAD_DIM ** -0.5
    for pi in range(heads // 2):
        qp = q_ref[:, pi * LANES:(pi + 1) * LANES] * scale
        outs = []
        for hh in range(2):
            h = 2 * pi + hh
            g = h // rep
            kp, kh = g // 2, g % 2
            qv = (qp if kh == hh else _swap_halves(qp)).astype(bf16)
            kv = k_ref[:, kp * LANES:(kp + 1) * LANES]
            km = (jnp.where(lo, kv, 0.0) if kh == 0 else jnp.where(lo, 0.0, kv)).astype(bf16)
            s = _dot_nt(qv, km)
            (p,) = _softmax_parts([s], sink_ref[h] if has_sink else None)
            o = _dot(p, v_ref[:, kp * LANES:(kp + 1) * LANES].astype(bf16))
            outs.append(o if kh == hh else _swap_halves(o))
        o_ref[:, pi * LANES:(pi + 1) * LANES] = jnp.where(lo, outs[0], outs[1]).astype(o_ref.dtype)


def _ctx_attn(q, k, v, sink, batch, heads, kv_heads):
    m = q.shape[0]
    t = m // batch
    has_sink = sink is not None
    if sink is None:
        sink = jnp.zeros((heads,), f32)
    kern = functools.partial(_ctx_attn_kernel, heads=heads, kv_heads=kv_heads, has_sink=has_sink)
    return pl.pallas_call(
        kern,
        grid=(batch,),
        in_specs=[pl.BlockSpec(memory_space=pltpu.SMEM),
                  pl.BlockSpec((t, heads * HEAD_DIM), lambda b: (b, 0)),
                  pl.BlockSpec((t, kv_heads * HEAD_DIM), lambda b: (b, 0)),
                  pl.BlockSpec((t, kv_heads * HEAD_DIM), lambda b: (b, 0))],
        out_specs=pl.BlockSpec((t, heads * HEAD_DIM), lambda b: (b, 0)),
        out_shape=jax.ShapeDtypeStruct((m, heads * HEAD_DIM), bf16),
        compiler_params=_cparams(("arbitrary",)),
        name="ctx_attn",
    )(sink.astype(f32), q, k, v)


def _rope(x, cos, sin_signed):
    lane = lax.broadcasted_iota(jnp.int32, (1, LANES), 1)
    first = (lane % 32) < 16
    rot = jnp.where(first, pltpu.roll(x, LANES - 16, 1), pltpu.roll(x, 16, 1))
    return x * cos + rot * sin_signed


def _win_attn_kernel(sink_ref, q_ref, k_ref, v_ref, ck_ref, cv_ref, cosq_ref, sinq_ref, cosk_ref, sink_k_ref,
                     o_ref, klm, vl, ckm, cvb, *, qblk, t_lat):
    n = pl.program_id(1)
    lo = _lane_lo()
    span = 3 * qblk

    @pl.when(n == 0)
    def _():
        kr = _rope(k_ref[...], cosk_ref[...], sink_k_ref[...])
        klm[0] = jnp.where(lo, kr, 0.0).astype(bf16)
        klm[1] = jnp.where(lo, 0.0, kr).astype(bf16)
        vl[...] = v_ref[...].astype(bf16)
        c = ck_ref[0]
        ckm[0] = jnp.where(lo, c, 0.0).astype(bf16)
        ckm[1] = jnp.where(lo, 0.0, c).astype(bf16)
        cvb[...] = cv_ref[0].astype(bf16)

    start = pl.multiple_of(jnp.clip(n * qblk - qblk, 0, t_lat - span), qblk)
    qpos = n * qblk + lax.broadcasted_iota(jnp.int32, (qblk, 1), 0)
    kpos = start + lax.broadcasted_iota(jnp.int32, (1, span), 1)
    valid = jnp.abs(kpos - qpos) <= WINDOW
    rep = A_HEADS // A_KV_HEADS
    scale = HEAD_DIM ** -0.5
    cq = cosq_ref[...]
    sq = sinq_ref[...]
    for pi in range(A_HEADS // 2):
        qp = _rope(q_ref[:, pi * LANES:(pi + 1) * LANES], cq, sq) * scale
        outs = []
        for hh in range(2):
            h = 2 * pi + hh
            g = h // rep
            qv = (qp if g == hh else _swap_halves(qp)).astype(bf16)
            s_c = _dot_nt(qv, ckm[g])
            s_l = _dot_nt(qv, klm[g, pl.ds(start, span), :])
            s_l = jnp.where(valid, s_l, NEG)
            p_c, p_l = _softmax_parts([s_c, s_l], sink_ref[h])
            o = _dot(p_c, cvb[...]) + _dot(p_l, vl[pl.ds(start, span), :])
            outs.append(o if g == hh else _swap_halves(o))
        o_ref[:, pi * LANES:(pi + 1) * LANES] = jnp.where(lo, outs[0], outs[1]).astype(o_ref.dtype)


def _rope_tables(t):
    half = HEAD_DIM // 2
    nf = half // 2
    pos = jnp.arange(t)
    row = (pos // GRID_W).astype(f32)
    col = (pos % GRID_W).astype(f32)
    inv = ROPE_BASE ** (-jnp.arange(nf, dtype=f32) / nf)
    ang_r = row[:, None] * inv
    ang_c = col[:, None] * inv
    ang = jnp.concatenate([ang_r, ang_r, ang_c, ang_c], axis=-1)
    sign = jnp.concatenate([-jnp.ones((nf,), f32), jnp.ones((nf,), f32)] * 2)
    cos = jnp.cos(ang)
    sin_signed = jnp.sin(ang) * sign
    return jnp.tile(cos, (1, 2)), jnp.tile(sin_signed, (1, 2))


def _win_attn(q, k, v, ck, cv, sink, batch):
    m = q.shape[0]
    t = m // batch
    qblk = 128
    nb = t // qblk
    lc = ck.shape[1]
    cos, sin_s = _rope_tables(t)
    kern = functools.partial(_win_attn_kernel, qblk=qblk, t_lat=t)
    return pl.pallas_call(
        kern,
        grid=(batch, nb),
        in_specs=[pl.BlockSpec(memory_space=pltpu.SMEM),
                  pl.BlockSpec((qblk, A_HEADS * HEAD_DIM), lambda b, n: (b * nb + n, 0)),
                  pl.BlockSpec((t, LANES), lambda b, n: (b, 0)),
                  pl.BlockSpec((t, LANES), lambda b, n: (b, 0)),
                  pl.BlockSpec((1, lc, LANES), lambda b, n: (b, 0, 0)),
                  pl.BlockSpec((1, lc, LANES), lambda b, n: (b, 0, 0)),
                  pl.BlockSpec((qblk, LANES), lambda b, n: (n, 0)),
                  pl.BlockSpec((qblk, LANES), lambda b, n: (n, 0)),
                  pl.BlockSpec((t, LANES), lambda b, n: (0, 0)),
                  pl.BlockSpec((t, LANES), lambda b, n: (0, 0))],
        out_specs=pl.BlockSpec((qblk, A_HEADS * HEAD_DIM), lambda b, n: (b * nb + n, 0)),
        out_shape=jax.ShapeDtypeStruct((m, A_HEADS * HEAD_DIM), bf16),
        scratch_shapes=[pltpu.VMEM((2, t, LANES), bf16), pltpu.VMEM((t, LANES), bf16),
                        pltpu.VMEM((2, lc, LANES), bf16), pltpu.VMEM((lc, LANES), bf16)],
        compiler_params=_cparams(("arbitrary", "arbitrary")),
        name="win_attn",
    )(sink.astype(f32), q, k, v, ck, cv, cos, sin_s, cos, sin_s)


def _nb_attn_kernel(q_ref, k_ref, v_ref, ck_ref, cv_ref, tl_ref, tr_ref, o_ref, bias, *, t_lat, qblk):
    b = pl.program_id(1)
    lo = _lane_lo()
    rows = t_lat // GRID_W
    wr = min(NB_ROWS, rows)
    neg_slot = 2 * NB_ROWS - 1

    @pl.when(b == 0)
    def _():
        for hh in range(2):
            for rq in range(rows):
                k0 = min(max(rq - wr // 2, 0), rows - wr)
                for p in range(rows // 2):
                    idx = []
                    for rk in (2 * p, 2 * p + 1):
                        idx.append(rk - rq + NB_ROWS - 1 if k0 <= rk < k0 + wr else neg_slot)
                    bias[hh, rq * GRID_W:(rq + 1) * GRID_W, p * LANES:(p + 1) * LANES] = (
                        tl_ref[hh, idx[0]] + tr_ref[hh, idx[1]])

    kb = k_ref[...]
    km = [jnp.where(lo, kb, 0.0).astype(bf16), jnp.where(lo, 0.0, kb).astype(bf16)]
    vb = v_ref[...].astype(bf16)
    cb = ck_ref[0]
    ckm = [jnp.where(lo, cb, 0.0).astype(bf16), jnp.where(lo, 0.0, cb).astype(bf16)]
    cvb = cv_ref[0].astype(bf16)
    scale = HEAD_DIM ** -0.5
    for qb in range(t_lat // qblk):
        rs = slice(qb * qblk, (qb + 1) * qblk)
        qv = (q_ref[rs, :] * scale).astype(bf16)
        outs = []
        for hh in range(2):
            s_c = _dot_nt(qv, ckm[hh])
            s_n = _dot_nt(qv, km[hh]) + bias[hh, rs, :]
            p_c, p_n = _softmax_parts([s_c, s_n], None)
            outs.append(_dot(p_c, cvb) + _dot(p_n, vb))
        o_ref[rs, :] = jnp.where(lo, outs[0], outs[1]).astype(o_ref.dtype)


def _nb_bias_tiles(rpb):
    heads = rpb.shape[0]
    cq = jnp.arange(GRID_W)
    cstart = jnp.clip(cq - NB_COLS // 2, 0, GRID_W - NB_COLS)
    col_ok = (cq[None, :] >= cstart[:, None]) & (cq[None, :] < cstart[:, None] + NB_COLS)
    dc = jnp.clip(cq[None, :] - cq[:, None], 1 - NB_COLS, NB_COLS - 1) + NB_COLS - 1
    onehot = jax.nn.one_hot(dc.reshape(-1), 2 * NB_COLS - 1, dtype=f32)
    tm = jnp.einsum('hab,kb->hak', rpb.astype(f32), onehot, precision=lax.Precision.HIGHEST)
    tm = tm.reshape(heads, 2 * NB_ROWS - 1, GRID_W, GRID_W)
    tm = jnp.where(col_ok[None, None], tm, NEG)
    tm = jnp.concatenate([tm, jnp.full((heads, 1, GRID_W, GRID_W), NEG, f32)], axis=1)
    z = jnp.zeros_like(tm)
    return jnp.concatenate([tm, z], axis=-1), jnp.concatenate([z, tm], axis=-1)


def _nb_attn(q, k, v, ck, cv, rpb, batch):
    m = q.shape[0]
    t = m // batch
    lc = ck.shape[1]
    npairs = C_HEADS // 2
    tl, tr = _nb_bias_tiles(rpb)
    kern = functools.partial(_nb_attn_kernel, t_lat=t, qblk=256)
    return pl.pallas_call(
        kern,
        grid=(npairs, batch),
        in_specs=[pl.BlockSpec((t, LANES), lambda p, b: (b, p)),
                  pl.BlockSpec((t, LANES), lambda p, b: (b, p)),
                  pl.BlockSpec((t, LANES), lambda p, b: (b, p)),
                  pl.BlockSpec((1, lc, LANES), lambda p, b: (b, 0, p)),
                  pl.BlockSpec((1, lc, LANES), lambda p, b: (b, 0, p)),
                  pl.BlockSpec((2, 2 * NB_ROWS, GRID_W, LANES), lambda p, b: (p, 0, 0, 0)),
                  pl.BlockSpec((2, 2 * NB_ROWS, GRID_W, LANES), lambda p, b: (p, 0, 0, 0))],
        out_specs=pl.BlockSpec((t, LANES), lambda p, b: (b, p)),
        out_shape=jax.ShapeDtypeStruct((m, C_HEADS * HEAD_DIM), bf16),
        scratch_shapes=[pltpu.VMEM((2, t, t), f32)],
        compiler_params=_cparams(("arbitrary", "arbitrary")),
        name="nb_attn",
    )(q, k, v, ck, cv, tl, tr)


def _ret_kernel(q_ref, k_ref, v_ref, g_ref, s0f_ref, s0b_ref, qdf_ref, qdb_ref, kdf_ref, kdb_ref, dm_ref,
                cdf_ref, cdb_ref, bd_ref, o_ref, sf_ref, sb_ref, kvf, kvb, *, t):
    c = RET_CHUNK
    n = t // c
    lo = _lane_lo()
    bd = bd_ref[...]
    for pi in range(B_HEADS // 2):
        cols = slice(pi * LANES, (pi + 1) * LANES)
        qdf, qdb, kdf, kdb = qdf_ref[pi], qdb_ref[pi], kdf_ref[pi], kdb_ref[pi]
        cdf, cdb = cdf_ref[pi], cdb_ref[pi]
        for ci in range(n):
            rs = slice(ci * c, (ci + 1) * c)
            kc = k_ref[rs, cols] * (HEAD_DIM ** -0.5)
            vc = v_ref[rs, cols].astype(bf16)
            kvf[ci] = _dot_tn((kc * kdf).astype(bf16), vc) * bd
            kvb[ci] = _dot_tn((kc * kdb).astype(bf16), vc) * bd
        s = s0f_ref[0, pi]
        for ci in range(n):
            upd = kvf[ci]
            kvf[ci] = s
            s = s * cdf + upd
        sf_ref[0, pi] = s
        s = s0b_ref[0, pi]
        for ci in range(n - 1, -1, -1):
            upd = kvb[ci]
            kvb[ci] = s
            s = s * cdb + upd
        sb_ref[0, pi] = s
        for ci in range(n):
            rs = slice(ci * c, (ci + 1) * c)
            qc = q_ref[rs, cols]
            kc = k_ref[rs, cols] * (HEAD_DIM ** -0.5)
            vc = v_ref[rs, cols].astype(bf16)
            qb = qc.astype(bf16)
            a0 = _dot_nt(qb, jnp.where(lo, kc, 0.0).astype(bf16)) * dm_ref[2 * pi]
            a1 = _dot_nt(qb, jnp.where(lo, 0.0, kc).astype(bf16)) * dm_ref[2 * pi + 1]
            o = jnp.where(lo, _dot(a0.astype(bf16), vc), _dot(a1.astype(bf16), vc))
            o = o + _dot((qc * qdf).astype(bf16), kvf[ci].astype(bf16))
            o = o + _dot((qc * qdb).astype(bf16), kvb[ci].astype(bf16))
            inv_n = 1.0 / HEAD_DIM
            m0 = jnp.where(lo, o, 0.0).sum(axis=-1, keepdims=True) * inv_n
            m1 = jnp.where(lo, 0.0, o).sum(axis=-1, keepdims=True) * inv_n
            d = o - jnp.where(lo, m0, m1)
            d2 = d * d
            v0 = jnp.where(lo, d2, 0.0).sum(axis=-1, keepdims=True) * inv_n
            v1 = jnp.where(lo, 0.0, d2).sum(axis=-1, keepdims=True) * inv_n
            y = d * lax.rsqrt(jnp.where(lo, v0, v1) + EPS)
            gt = g_ref[rs, cols]
            o_ref[rs, cols] = (gt * jax.nn.sigmoid(gt) * y).astype(o_ref.dtype)


def _ret_tables(lg_f, lg_b):
    c = RET_CHUNK
    lf = jax.nn.log_sigmoid(lg_f.astype(f32))
    lb = jax.nn.log_sigmoid(lg_b.astype(f32))
    idx = jnp.arange(c, dtype=f32)

    def lanes(per_head):
        r = per_head.shape[1]
        x = jnp.repeat(per_head[:, :, None], HEAD_DIM, axis=2)
        x = x.reshape(B_HEADS // 2, 2, r, HEAD_DIM).transpose(0, 2, 1, 3)
        return x.reshape(B_HEADS // 2, r, LANES)

    qdf = lanes(jnp.exp(lf[:, None] * (idx + 1.0)))
    kdf = lanes(jnp.exp(lf[:, None] * (c - 1.0 - idx)))
    qdb = lanes(jnp.exp(lb[:, None] * (c - idx)))
    kdb = lanes(jnp.exp(lb[:, None] * idx))
    diff = idx[:, None] - idx[None, :]
    low = jnp.where(diff >= 0, jnp.exp(lf[:, None, None] * jnp.maximum(diff, 0.0)), 0.0)
    upp = jnp.where(diff <= 0, jnp.exp(lb[:, None, None] * jnp.maximum(-diff, 0.0)), 0.0)
    dm = low + upp
    cdf = lanes(jnp.exp(lf * c)[:, None])
    cdb = lanes(jnp.exp(lb * c)[:, None])
    r = jnp.arange(LANES)
    bd = ((r[:, None] < HEAD_DIM) == (r[None, :] < HEAD_DIM)).astype(f32)
    return qdf, qdb, kdf, kdb, dm, cdf, cdb, bd


def _blockdiag_states(s):
    b = s.shape[0]
    s = s.astype(f32).reshape(b, B_HEADS // 2, 2, HEAD_DIM, HEAD_DIM)
    z = jnp.zeros_like(s[:, :, 0])
    top = jnp.concatenate([s[:, :, 0], z], axis=-1)
    bot = jnp.concatenate([z, s[:, :, 1]], axis=-1)
    return jnp.concatenate([top, bot], axis=-2)


def _diag_states(s):
    b = s.shape[0]
    h0 = s[:, :, :HEAD_DIM, :HEAD_DIM]
    h1 = s[:, :, HEAD_DIM:, HEAD_DIM:]
    return jnp.stack([h0, h1], axis=2).reshape(b, B_HEADS, HEAD_DIM, HEAD_DIM)


def _retention(q, k, v, g, s0f, s0b, tables, batch):
    m = q.shape[0]
    t = m // batch
    w = B_HEADS * HEAD_DIM
    np_ = B_HEADS // 2
    c = RET_CHUNK
    kern = functools.partial(_ret_kernel, t=t)
    tok = pl.BlockSpec((t, w), lambda b: (b, 0))
    st = pl.BlockSpec((1, np_, LANES, LANES), lambda b: (b, 0, 0, 0))

    def full(a):
        nd = a.ndim
        return pl.BlockSpec(a.shape, lambda b: (0,) * nd)

    o, sf, sb = pl.pallas_call(
        kern,
        grid=(batch,),
        in_specs=[tok, tok, tok, tok, st, st] + [full(a) for a in tables],
        out_specs=[tok, st, st],
        out_shape=[jax.ShapeDtypeStruct((m, w), bf16),
                   jax.ShapeDtypeStruct((batch, np_, LANES, LANES), f32),
                   jax.ShapeDtypeStruct((batch, np_, LANES, LANES), f32)],
        scratch_shapes=[pltpu.VMEM((t // c, LANES, LANES), f32), pltpu.VMEM((t // c, LANES, LANES), f32)],
        compiler_params=_cparams(("arbitrary",)),
        name="retention",
    )(q, k, v, g, _blockdiag_states(s0f), _blockdiag_states(s0b), *tables)
    return o, _diag_states(sf), _diag_states(sb)


def _route(h, wr):
    logits = jnp.dot(h, wr, preferred_element_type=f32, precision=lax.Precision.HIGHEST)
    lane = lax.broadcasted_iota(jnp.int32, logits.shape, 1).astype(f32)
    logits = jnp.where(lane < N_EXPERTS, logits, -jnp.inf)
    m1 = logits.max(axis=-1, keepdims=True)
    i1 = jnp.where(logits == m1, lane, float(LANES)).min(axis=-1, keepdims=True)
    rest = jnp.where(lane == i1, -jnp.inf, logits)
    m2 = rest.max(axis=-1, keepdims=True)
    i2 = jnp.where(rest == m2, lane, float(LANES)).min(axis=-1, keepdims=True)
    e2 = jnp.exp(m2 - m1)
    g1 = 1.0 / (1.0 + e2)
    g2 = e2 * g1
    return jnp.where(lane == i1, g1, 0.0) + jnp.where(lane == i2, g2, 0.0)


def _outproj_kernel(a_ref, b_ref, w_ref, x_ref, gate_ref, g2_ref, sh_ref, sc_ref, *rest, has_router):
    if has_router:
        wr_ref, xo_ref, h_ref, comb_ref = rest
    else:
        xo_ref, h_ref = rest
    half = a_ref.shape[1]
    acc = _dot(a_ref[...], w_ref[:half, :]) + _dot(b_ref[...], w_ref[half:, :])
    xn = x_ref[...] + gate_ref[0] * acc
    xo_ref[...] = xn
    h = _norm_mod(xn, g2_ref[...], sh_ref[0], sc_ref[0])
    h_ref[...] = h.astype(bf16)
    if has_router:
        comb_ref[...] = _route(h, wr_ref[...])


def _outproj(a, a_blk, b, b_blk, w, x, g2, mod3, layer, rows_per_batch, is_sample, w_router=None, tm=512):
    m = x.shape[0]
    tm = min(tm, m)
    half = w.shape[0] // 2
    has_router = w_router is not None
    kern = functools.partial(_outproj_kernel, has_router=has_router)
    row = pl.BlockSpec((tm, D), lambda i: (i, 0))
    in_specs = [pl.BlockSpec((tm, half), lambda i: (i, a_blk)),
                pl.BlockSpec((tm, half), lambda i: (i, b_blk)),
                pl.BlockSpec(w.shape, lambda i: (0, 0)),
                row,
                _mod_spec(layer, 2, tm, rows_per_batch, is_sample),
                pl.BlockSpec((1, D), lambda i: (0, 0)),
                _mod_spec(layer, 3, tm, rows_per_batch, is_sample),
                _mod_spec(layer, 4, tm, rows_per_batch, is_sample)]
    args = [a, b, w, x, mod3, g2, mod3, mod3]
    out_specs = [row, row]
    out_shape = [jax.ShapeDtypeStruct((m, D), f32), jax.ShapeDtypeStruct((m, D), bf16)]
    if has_router:
        in_specs.append(pl.BlockSpec((D, LANES), lambda i: (0, 0)))
        args.append(w_router)
        out_specs.append(pl.BlockSpec((tm, LANES), lambda i: (i, 0)))
        out_shape.append(jax.ShapeDtypeStruct((m, LANES), f32))
    return pl.pallas_call(
        kern, grid=(m // tm,), in_specs=in_specs, out_specs=out_specs, out_shape=out_shape,
        compiler_params=_cparams(("arbitrary",)), name="out_proj",
    )(*args)


def _ffn_kernel(*refs, has_comb, final_norm):
    h_ref, wg_ref, wu_ref, wd_ref, x_ref, gate_ref = refs[:6]
    rest = list(refs[6:])
    comb_ref = rest.pop(0) if has_comb else None
    fg_ref = rest.pop(0) if final_norm else None
    o_ref, acc = rest
    e = pl.program_id(1)
    j = pl.program_id(2)

    @pl.when((e == 0) & (j == 0))
    def _():
        acc[...] = jnp.zeros_like(acc)

    h = h_ref[...]
    g = _dot(h, wg_ref[0].astype(bf16))
    u = _dot(h, wu_ref[0].astype(bf16))
    act = g * jax.nn.sigmoid(g) * u
    if has_comb:
        lane = lax.broadcasted_iota(jnp.int32, (1, LANES), 1)
        act = act * jnp.where(lane == e, comb_ref[...], 0.0).sum(axis=-1, keepdims=True)
    acc[...] += _dot(act.astype(bf16), wd_ref[0].astype(bf16))

    @pl.when((e == pl.num_programs(1) - 1) & (j == pl.num_programs(2) - 1))
    def _():
        out = x_ref[...] + gate_ref[0] * acc[...]
        if final_norm:
            out = out * lax.rsqrt(jnp.mean(out * out, axis=-1, keepdims=True) + EPS) * fg_ref[...]
        o_ref[...] = out


def _ffn(h, w_gu, w_down, x, mod3, layer, rows_per_batch, is_sample, comb=None, final_g=None, tm=1024, tf=256):
    m = x.shape[0]
    tm = min(tm, m)
    ne, _, f2 = w_gu.shape
    f = f2 // 2
    nf = f // tf
    has_comb = comb is not None
    final_norm = final_g is not None
    kern = functools.partial(_ffn_kernel, has_comb=has_comb, final_norm=final_norm)
    row = pl.BlockSpec((tm, D), lambda i, e, j: (i, 0))
    in_specs = [row,
                pl.BlockSpec((1, D, tf), lambda i, e, j: (e, 0, j)),
                pl.BlockSpec((1, D, tf), lambda i, e, j: (e, 0, j + nf)),
                pl.BlockSpec((1, tf, D), lambda i, e, j: (e, j, 0)),
                row,
                _mod_spec(layer, 5, tm, rows_per_batch, is_sample)]
    args = [h, w_gu, w_gu, w_down, x, mod3]
    if has_comb:
        in_specs.append(pl.BlockSpec((tm, LANES), lambda i, e, j: (i, 0)))
        args.append(comb)
    if final_norm:
        in_specs.append(pl.BlockSpec((1, D), lambda i, e, j: (0, 0)))
        args.append(final_g)
    return pl.pallas_call(
        kern, grid=(m // tm, ne, nf), in_specs=in_specs, out_specs=row,
        out_shape=jax.ShapeDtypeStruct((m, D), f32),
        scratch_shapes=[pltpu.VMEM((tm, D), f32)],
        compiler_params=_cparams(("arbitrary", "arbitrary", "arbitrary")), name="ffn",
    )(*args)


def kernel(x_prompt, x_sample, cache_a_k, cache_a_v, state_ret_fwd, state_ret_bwd, cache_c_k, cache_c_v, c, c_ctx,
           mod_w, mod_b, norm1_g, norm2_g, final_g, ev_w_in, ev_w_out, ev_sink, ev_ret_logit_fwd, ev_ret_logit_bwd,
           ev_ffn_w_gu, ev_ffn_w_down, od_w_qkv, od_w_out, od_rpb, od_w_router, od_moe_w_gu, od_moe_w_down):
    bp, tp, _ = x_prompt.shape
    bs, ts, _ = x_sample.shape
    depth = mod_w.shape[0]
    assert 1 + bs <= MOD_ROWS
    xp = x_prompt.reshape(bp * tp, D)
    xs = x_sample.reshape(bs * ts, D)

    cvecs = jnp.zeros((MOD_ROWS, D), f32).at[0].set(c_ctx).at[1:1 + bs].set(c)
    mod3 = _adaln(cvecs, mod_w, mod_b)

    a_q = A_HEADS * HEAD_DIM
    a_kv = A_KV_HEADS * HEAD_DIM
    b_w = B_HEADS * HEAD_DIM
    c_w = C_HEADS * HEAD_DIM
    streams = ((False, tp), (True, ts))
    outs = {}
    for i in range(depth):
        j = i // 2
        g1 = norm1_g[i].reshape(1, D)
        g2 = norm2_g[i].reshape(1, D)
        last = i == depth - 1
        fg = final_g.reshape(1, D) if last else None
        if i % 2 == 0:
            w_in = ev_w_in[j].astype(bf16)
            w_out = ev_w_out[j].astype(bf16)
            splits = (a_q, a_kv, a_kv, b_w, b_w, b_w, b_w)
            dts = (f32,) * 7
            tables = _ret_tables(ev_ret_logit_fwd[j], ev_ret_logit_bwd[j])
            qa_p, ka_p, va_p, qr_p, kr_p, vr_p, gr_p = _proj(xp, g1, mod3, i, w_in, splits, dts, tp, False)
            qa_s, ka_s, va_s, qr_s, kr_s, vr_s, gr_s = _proj(xs, g1, mod3, i, w_in, splits, dts, ts, True)
            oa_p = _ctx_attn(qa_p, ka_p, va_p, ev_sink[j], bp, A_HEADS, A_KV_HEADS)
            zero = jnp.zeros((bp, B_HEADS, HEAD_DIM, HEAD_DIM), f32)
            or_p, sf, sb = _retention(qr_p, kr_p, vr_p, gr_p, zero, zero, tables, bp)
            lc = cache_a_k.shape[2]
            oa_s = _win_attn(qa_s, ka_s, va_s, cache_a_k[:, j].reshape(bs, lc, a_kv),
                             cache_a_v[:, j].reshape(bs, lc, a_kv), ev_sink[j], bs)
            or_s, _, _ = _retention(qr_s, kr_s, vr_s, gr_s, state_ret_fwd[:, j], state_ret_bwd[:, j], tables, bs)
            outs.setdefault('a_k', []).append(ka_p.reshape(bp, tp, A_KV_HEADS, HEAD_DIM))
            outs.setdefault('a_v', []).append(va_p.reshape(bp, tp, A_KV_HEADS, HEAD_DIM))
            outs.setdefault('r_f', []).append(sf)
            outs.setdefault('r_b', []).append(sb)
            xp, hp = _outproj(oa_p, 0, or_p, 0, w_out, xp, g2, mod3, i, tp, False)
            xs, hs = _outproj(oa_s, 0, or_s, 0, w_out, xs, g2, mod3, i, ts, True)
            w_gu = ev_ffn_w_gu[j][None]
            w_dn = ev_ffn_w_down[j][None]
            xp = _ffn(hp, w_gu, w_dn, xp, mod3, i, tp, False, final_g=fg)
            xs = _ffn(hs, w_gu, w_dn, xs, mod3, i, ts, True, final_g=fg)
        else:
            w_qkv = od_w_qkv[j].astype(bf16)
            w_out = od_w_out[j].astype(bf16)
            w_router = jnp.zeros((D, LANES), f32).at[:, :N_EXPERTS].set(od_w_router[j])
            splits = (c_w, c_w, c_w)
            dts = (f32,) * 3
            q_p, k_p, v_p = _proj(xp, g1, mod3, i, w_qkv, splits, dts, tp, False)
            q_s, k_s, v_s = _proj(xs, g1, mod3, i, w_qkv, splits, dts, ts, True)
            o_p = _ctx_attn(q_p, k_p, v_p, None, bp, C_HEADS, C_HEADS)
            lc = cache_c_k.shape[2]
            o_s = _nb_attn(q_s, k_s, v_s, cache_c_k[:, j].reshape(bs, lc, c_w),
                           cache_c_v[:, j].reshape(bs, lc, c_w), od_rpb[j], bs)
            outs.setdefault('c_k', []).append(k_p.reshape(bp, tp, C_HEADS, HEAD_DIM))
            outs.setdefault('c_v', []).append(v_p.reshape(bp, tp, C_HEADS, HEAD_DIM))
            xp, hp, comb_p = _outproj(o_p, 0, o_p, 1, w_out, xp, g2, mod3, i, tp, False, w_router=w_router)
            xs, hs, comb_s = _outproj(o_s, 0, o_s, 1, w_out, xs, g2, mod3, i, ts, True, w_router=w_router)
            xp = _ffn(hp, od_moe_w_gu[j], od_moe_w_down[j], xp, mod3, i, tp, False, comb=comb_p, final_g=fg)
            xs = _ffn(hs, od_moe_w_gu[j], od_moe_w_down[j], xs, mod3, i, ts, True, comb=comb_s, final_g=fg)
    y_prompt = xp.reshape(bp, tp, D)
    y_sample = xs.reshape(bs, ts, D)
    return (y_prompt, y_sample,
            jnp.stack(outs['a_k'], axis=1), jnp.stack(outs['a_v'], axis=1),
            jnp.stack(outs['r_f'], axis=1), jnp.stack(outs['r_b'], axis=1),
            jnp.stack(outs['c_k'], axis=1), jnp.stack(outs['c_v'], axis=1))
```

```python
import functools

import numpy as np
import jax
import jax.numpy as jnp
from jax import lax
from jax.experimental import pallas as pl
from jax.experimental.pallas import tpu as pltpu

f32 = jnp.float32
bf16 = jnp.bfloat16

D = 1024
HEAD_DIM = 64
GRID_W = 64
ROPE_BASE = 10000.0
A_HEADS = 8
A_KV_HEADS = 2
WINDOW = 128
B_HEADS = 8
RET_CHUNK = 128
C_HEADS = 16
NB_ROWS = 8
NB_COLS = 16
N_EXPERTS = 8
EPS = 1e-6
NEG = -1e30

LANES = 128
MOD_ROWS = 16
VMEM_LIMIT_MB = 56


def _cparams(sem, vmem_mb=VMEM_LIMIT_MB):
    return pltpu.CompilerParams(dimension_semantics=sem, vmem_limit_bytes=vmem_mb * 1024 * 1024)


def _dot(a, b):
    return jnp.dot(a, b, preferred_element_type=f32)


def _dot_nt(a, b):
    return lax.dot_general(a, b, (((1,), (1,)), ((), ())), preferred_element_type=f32)


def _dot_tn(a, b):
    return lax.dot_general(a, b, (((0,), (0,)), ((), ())), preferred_element_type=f32)


def _lane_lo():
    return lax.broadcasted_iota(jnp.int32, (1, LANES), 1) < HEAD_DIM


def _swap_halves(x):
    return pltpu.roll(x, HEAD_DIM, 1)


def _mod_imap(layer, which, tm, rows_per_batch, is_sample):
    def imap(i, *_):
        r = (1 + (i * tm) // rows_per_batch) if is_sample else 0
        return ((layer * MOD_ROWS + r) * 6 + which, 0, 0)

    return imap


def _mod_spec(layer, which, tm, rows_per_batch, is_sample):
    return pl.BlockSpec((1, 1, D), _mod_imap(layer, which, tm, rows_per_batch, is_sample))


def _adaln_kernel(cv_ref, w_ref, b_ref, o_ref):
    cv = cv_ref[...]
    s = (cv * jax.nn.sigmoid(cv)).astype(bf16)
    o_ref[0] = _dot(s, w_ref[0].astype(bf16)) + b_ref[0]


def _adaln(cvecs, mod_w, mod_b):
    depth = mod_w.shape[0]
    tn = 1024
    out = pl.pallas_call(
        _adaln_kernel,
        grid=(depth, 6 * D // tn),
        in_specs=[pl.BlockSpec((MOD_ROWS, D), lambda l, j: (0, 0)),
                  pl.BlockSpec((1, D, tn), lambda l, j: (l, 0, j)),
                  pl.BlockSpec((1, 1, tn), lambda l, j: (l, 0, j))],
        out_specs=pl.BlockSpec((1, MOD_ROWS, tn), lambda l, j: (l, 0, j)),
        out_shape=jax.ShapeDtypeStruct((depth, MOD_ROWS, 6 * D), f32),
        compiler_params=_cparams(("arbitrary", "arbitrary")),
        name="adaln",
    )(cvecs, mod_w, mod_b.reshape(depth, 1, 6 * D))
    return out.reshape(depth * MOD_ROWS * 6, 1, D)


def _norm_mod(x, g, shift, scale):
    y = x * lax.rsqrt(jnp.mean(x * x, axis=-1, keepdims=True) + EPS) * g
    return y * (1.0 + scale) + shift


def _proj_kernel(x_ref, g_ref, sh_ref, sc_ref, w_ref, *out_refs, splits, chunk):
    h = _norm_mod(x_ref[...], g_ref[...], sh_ref[0], sc_ref[0]).astype(bf16)
    off = 0
    for o_ref, width in zip(out_refs, splits):
        for c0 in range(0, width, chunk):
            cw = min(chunk, width - c0)
            o_ref[:, c0:c0 + cw] = _dot(h, w_ref[:, off + c0:off + c0 + cw]).astype(o_ref.dtype)
        off += width


def _proj(x, g, mod3, layer, w, splits, dtypes, rows_per_batch, is_sample, tm=512):
    m = x.shape[0]
    tm = min(tm, m)
    n = w.shape[1]
    kern = functools.partial(_proj_kernel, splits=tuple(splits), chunk=512)
    return pl.pallas_call(
        kern,
        grid=(m // tm,),
        in_specs=[pl.BlockSpec((tm, D), lambda i: (i, 0)),
                  pl.BlockSpec((1, D), lambda i: (0, 0)),
                  _mod_spec(layer, 0, tm, rows_per_batch, is_sample),
                  _mod_spec(layer, 1, tm, rows_per_batch, is_sample),
                  pl.BlockSpec((D, n), lambda i: (0, 0))],
        out_specs=[pl.BlockSpec((tm, s), lambda i: (i, 0)) for s in splits],
        out_shape=[jax.ShapeDtypeStruct((m, s), dt) for s, dt in zip(splits, dtypes)],
        compiler_params=_cparams(("arbitrary",)),
        name="norm_proj",
    )(x, g, mod3, mod3, w)


def _softmax_parts(parts, sink):
    m = parts[0].max(axis=-1, keepdims=True)
    for s in parts[1:]:
        m = jnp.maximum(m, s.max(axis=-1, keepdims=True))
    if sink is not None:
        m = jnp.maximum(m, sink)
    es = [jnp.exp(s - m) for s in parts]
    den = es[0].sum(axis=-1, keepdims=True)
    for e in es[1:]:
        den = den + e.sum(axis=-1, keepdims=True)
    if sink is not None:
        den = den + jnp.exp(sink - m)
    inv = 1.0 / den
    return [(e * inv).astype(bf16) for e in es]


def _ctx_attn_kernel(sink_ref, q_ref, k_ref, v_ref, o_ref, *, heads, kv_heads, has_sink):
    rep = heads // kv_heads
    lo = _lane_lo()
    scale = HEAD_DIM ** -0.5
    for pi in range(heads // 2):
        qp = q_ref[:, pi * LANES:(pi + 1) * LANES] * scale
        outs = []
        for hh in range(2):
            h = 2 * pi + hh
            g = h // rep
            kp, kh = g // 2, g % 2
            qv = (qp if kh == hh else _swap_halves(qp)).astype(bf16)
            kv = k_ref[:, kp * LANES:(kp + 1) * LANES]
            km = (jnp.where(lo, kv, 0.0) if kh == 0 else jnp.where(lo, 0.0, kv)).astype(bf16)
            s = _dot_nt(qv, km)
            (p,) = _softmax_parts([s], sink_ref[h] if has_sink else None)
            o = _dot(p, v_ref[:, kp * LANES:(kp + 1) * LANES].astype(bf16))
            outs.append(o if kh == hh else _swap_halves(o))
        o_ref[:, pi * LANES:(pi + 1) * LANES] = jnp.where(lo, outs[0], outs[1]).astype(o_ref.dtype)


def _ctx_attn(q, k, v, sink, batch, heads, kv_heads):
    m = q.shape[0]
    t = m // batch
    has_sink = sink is not None
    if sink is None:
        sink = jnp.zeros((heads,), f32)
    kern = functools.partial(_ctx_attn_kernel, heads=heads, kv_heads=kv_heads, has_sink=has_sink)
    return pl.pallas_call(
        kern,
        grid=(batch,),
        in_specs=[pl.BlockSpec(memory_space=pltpu.SMEM),
                  pl.BlockSpec((t, heads * HEAD_DIM), lambda b: (b, 0)),
                  pl.BlockSpec((t, kv_heads * HEAD_DIM), lambda b: (b, 0)),
                  pl.BlockSpec((t, kv_heads * HEAD_DIM), lambda b: (b, 0))],
        out_specs=pl.BlockSpec((t, heads * HEAD_DIM), lambda b: (b, 0)),
        out_shape=jax.ShapeDtypeStruct((m, heads * HEAD_DIM), bf16),
        compiler_params=_cparams(("arbitrary",)),
        name="ctx_attn",
    )(sink.astype(f32), q, k, v)


def _rope(x, cos, sin_signed):
    lane = lax.broadcasted_iota(jnp.int32, (1, LANES), 1)
    first = (lane % 32) < 16
    rot = jnp.where(first, pltpu.roll(x, LANES - 16, 1), pltpu.roll(x, 16, 1))
    return x * cos + rot * sin_signed


def _win_attn_kernel(sink_ref, q_ref, k_ref, v_ref, ck_ref, cv_ref, cosq_ref, sinq_ref, cosk_ref, sink_k_ref,
                     o_ref, klm, vl, ckm, cvb, *, qblk, t_lat):
    n = pl.program_id(1)
    lo = _lane_lo()
    span = 3 * qblk

    @pl.when(n == 0)
    def _():
        kr = _rope(k_ref[...], cosk_ref[...], sink_k_ref[...])
        klm[0] = jnp.where(lo, kr, 0.0).astype(bf16)
        klm[1] = jnp.where(lo, 0.0, kr).astype(bf16)
        vl[...] = v_ref[...].astype(bf16)
        c = ck_ref[0]
        ckm[0] = jnp.where(lo, c, 0.0).astype(bf16)
        ckm[1] = jnp.where(lo, 0.0, c).astype(bf16)
        cvb[...] = cv_ref[0].astype(bf16)

    start = pl.multiple_of(jnp.clip(n * qblk - qblk, 0, t_lat - span), qblk)
    qpos = n * qblk + lax.broadcasted_iota(jnp.int32, (qblk, 1), 0)
    kpos = start + lax.broadcasted_iota(jnp.int32, (1, span), 1)
    valid = jnp.abs(kpos - qpos) <= WINDOW
    rep = A_HEADS // A_KV_HEADS
    scale = HEAD_DIM ** -0.5
    cq = cosq_ref[...]
    sq = sinq_ref[...]
    for pi in range(A_HEADS // 2):
        qp = _rope(q_ref[:, pi * LANES:(pi + 1) * LANES], cq, sq) * scale
        outs = []
        for hh in range(2):
            h = 2 * pi + hh
            g = h // rep
            qv = (qp if g == hh else _swap_halves(qp)).astype(bf16)
            s_c = _dot_nt(qv, ckm[g])
            s_l = _dot_nt(qv, klm[g, pl.ds(start, span), :])
            s_l = jnp.where(valid, s_l, NEG)
            p_c, p_l = _softmax_parts([s_c, s_l], sink_ref[h])
            o = _dot(p_c, cvb[...]) + _dot(p_l, vl[pl.ds(start, span), :])
            outs.append(o if g == hh else _swap_halves(o))
        o_ref[:, pi * LANES:(pi + 1) * LANES] = jnp.where(lo, outs[0], outs[1]).astype(o_ref.dtype)


def _rope_tables(t):
    half = HEAD_DIM // 2
    nf = half // 2
    pos = jnp.arange(t)
    row = (pos // GRID_W).astype(f32)
    col = (pos % GRID_W).astype(f32)
    inv = ROPE_BASE ** (-jnp.arange(nf, dtype=f32) / nf)
    ang_r = row[:, None] * inv
    ang_c = col[:, None] * inv
    ang = jnp.concatenate([ang_r, ang_r, ang_c, ang_c], axis=-1)
    sign = jnp.concatenate([-jnp.ones((nf,), f32), jnp.ones((nf,), f32)] * 2)
    cos = jnp.cos(ang)
    sin_signed = jnp.sin(ang) * sign
    return jnp.tile(cos, (1, 2)), jnp.tile(sin_signed, (1, 2))


def _win_attn(q, k, v, ck, cv, sink, batch):
    m = q.shape[0]
    t = m // batch
    qblk = 128
    nb = t // qblk
    lc = ck.shape[1]
    cos, sin_s = _rope_tables(t)
    kern = functools.partial(_win_attn_kernel, qblk=qblk, t_lat=t)
    return pl.pallas_call(
        kern,
        grid=(batch, nb),
        in_specs=[pl.BlockSpec(memory_space=pltpu.SMEM),
                  pl.BlockSpec((qblk, A_HEADS * HEAD_DIM), lambda b, n: (b * nb + n, 0)),
                  pl.BlockSpec((t, LANES), lambda b, n: (b, 0)),
                  pl.BlockSpec((t, LANES), lambda b, n: (b, 0)),
                  pl.BlockSpec((1, lc, LANES), lambda b, n: (b, 0, 0)),
                  pl.BlockSpec((1, lc, LANES), lambda b, n: (b, 0, 0)),
                  pl.BlockSpec((qblk, LANES), lambda b, n: (n, 0)),
                  pl.BlockSpec((qblk, LANES), lambda b, n: (n, 0)),
                  pl.BlockSpec((t, LANES), lambda b, n: (0, 0)),
                  pl.BlockSpec((t, LANES), lambda b, n: (0, 0))],
        out_specs=pl.BlockSpec((qblk, A_HEADS * HEAD_DIM), lambda b, n: (b * nb + n, 0)),
        out_shape=jax.ShapeDtypeStruct((m, A_HEADS * HEAD_DIM), bf16),
        scratch_shapes=[pltpu.VMEM((2, t, LANES), bf16), pltpu.VMEM((t, LANES), bf16),
                        pltpu.VMEM((2, lc, LANES), bf16), pltpu.VMEM((lc, LANES), bf16)],
        compiler_params=_cparams(("arbitrary", "arbitrary")),
        name="win_attn",
    )(sink.astype(f32), q, k, v, ck, cv, cos, sin_s, cos, sin_s)


def _nb_attn_kernel(q_ref, k_ref, v_ref, ck_ref, cv_ref, tl_ref, tr_ref, o_ref, bias, *, t_lat, qblk):
    b = pl.program_id(1)
    lo = _lane_lo()
    rows = t_lat // GRID_W
    wr = min(NB_ROWS, rows)
    neg_slot = 2 * NB_ROWS - 1

    @pl.when(b == 0)
    def _():
        for hh in range(2):
            for rq in range(rows):
                k0 = min(max(rq - wr // 2, 0), rows - wr)
                for p in range(rows // 2):
                    idx = []
                    for rk in (2 * p, 2 * p + 1):
                        idx.append(rk - rq + NB_ROWS - 1 if k0 <= rk < k0 + wr else neg_slot)
                    bias[hh, rq * GRID_W:(rq + 1) * GRID_W, p * LANES:(p + 1) * LANES] = (
                        tl_ref[hh, idx[0]] + tr_ref[hh, idx[1]])

    kb = k_ref[...]
    km = [jnp.where(lo, kb, 0.0).astype(bf16), jnp.where(lo, 0.0, kb).astype(bf16)]
    vb = v_ref[...].astype(bf16)
    cb = ck_ref[0]
    ckm = [jnp.where(lo, cb, 0.0).astype(bf16), jnp.where(lo, 0.0, cb).astype(bf16)]
    cvb = cv_ref[0].astype(bf16)
    scale = HEAD_DIM ** -0.5
    for qb in range(t_lat // qblk):
        rs = slice(qb * qblk, (qb + 1) * qblk)
        qv = (q_ref[rs, :] * scale).astype(bf16)
        outs = []
        for hh in range(2):
            s_c = _dot_nt(qv, ckm[hh])
            s_n = _dot_nt(qv, km[hh]) + bias[hh, rs, :]
            p_c, p_n = _softmax_parts([s_c, s_n], None)
            outs.append(_dot(p_c, cvb) + _dot(p_n, vb))
        o_ref[rs, :] = jnp.where(lo, outs[0], outs[1]).astype(o_ref.dtype)


def _nb_bias_tiles(rpb):
    heads = rpb.shape[0]
    cq = jnp.arange(GRID_W)
    cstart = jnp.clip(cq - NB_COLS // 2, 0, GRID_W - NB_COLS)
    col_ok = (cq[None, :] >= cstart[:, None]) & (cq[None, :] < cstart[:, None] + NB_COLS)
    dc = jnp.clip(cq[None, :] - cq[:, None], 1 - NB_COLS, NB_COLS - 1) + NB_COLS - 1
    onehot = jax.nn.one_hot(dc.reshape(-1), 2 * NB_COLS - 1, dtype=f32)
    tm = jnp.einsum('hab,kb->hak', rpb.astype(f32), onehot, precision=lax.Precision.HIGHEST)
    tm = tm.reshape(heads, 2 * NB_ROWS - 1, GRID_W, GRID_W)
    tm = jnp.where(col_ok[None, None], tm, NEG)
    tm = jnp.concatenate([tm, jnp.full((heads, 1, GRID_W, GRID_W), NEG, f32)], axis=1)
    z = jnp.zeros_like(tm)
    return jnp.concatenate([tm, z], axis=-1), jnp.concatenate([z, tm], axis=-1)


def _nb_attn(q, k, v, ck, cv, rpb, batch):
    m = q.shape[0]
    t = m // batch
    lc = ck.shape[1]
    npairs = C_HEADS // 2
    tl, tr = _nb_bias_tiles(rpb)
    kern = functools.partial(_nb_attn_kernel, t_lat=t, qblk=256)
    return pl.pallas_call(
        kern,
        grid=(npairs, batch),
        in_specs=[pl.BlockSpec((t, LANES), lambda p, b: (b, p)),
                  pl.BlockSpec((t, LANES), lambda p, b: (b, p)),
                  pl.BlockSpec((t, LANES), lambda p, b: (b, p)),
                  pl.BlockSpec((1, lc, LANES), lambda p, b: (b, 0, p)),
                  pl.BlockSpec((1, lc, LANES), lambda p, b: (b, 0, p)),
                  pl.BlockSpec((2, 2 * NB_ROWS, GRID_W, LANES), lambda p, b: (p, 0, 0, 0)),
                  pl.BlockSpec((2, 2 * NB_ROWS, GRID_W, LANES), lambda p, b: (p, 0, 0, 0))],
        out_specs=pl.BlockSpec((t, LANES), lambda p, b: (b, p)),
        out_shape=jax.ShapeDtypeStruct((m, C_HEADS * HEAD_DIM), bf16),
        scratch_shapes=[pltpu.VMEM((2, t, t), f32)],
        compiler_params=_cparams(("arbitrary", "arbitrary")),
        name="nb_attn",
    )(q, k, v, ck, cv, tl, tr)


def _ret_kernel(q_ref, k_ref, v_ref, g_ref, s0f_ref, s0b_ref, qdf_ref, qdb_ref, kdf_ref, kdb_ref, dm_ref,
                cdf_ref, cdb_ref, bd_ref, o_ref, sf_ref, sb_ref, kvf, kvb, *, t):
    c = RET_CHUNK
    n = t // c
    lo = _lane_lo()
    bd = bd_ref[...]
    for pi in range(B_HEADS // 2):
        cols = slice(pi * LANES, (pi + 1) * LANES)
        qdf, qdb, kdf, kdb = qdf_ref[pi], qdb_ref[pi], kdf_ref[pi], kdb_ref[pi]
        cdf, cdb = cdf_ref[pi], cdb_ref[pi]
        for ci in range(n):
            rs = slice(ci * c, (ci + 1) * c)
            kc = k_ref[rs, cols] * (HEAD_DIM ** -0.5)
            vc = v_ref[rs, cols].astype(bf16)
            kvf[ci] = _dot_tn((kc * kdf).astype(bf16), vc) * bd
            kvb[ci] = _dot_tn((kc * kdb).astype(bf16), vc) * bd
        s = s0f_ref[0, pi]
        for ci in range(n):
            upd = kvf[ci]
            kvf[ci] = s
            s = s * cdf + upd
        sf_ref[0, pi] = s
        s = s0b_ref[0, pi]
        for ci in range(n - 1, -1, -1):
            upd = kvb[ci]
            kvb[ci] = s
            s = s * cdb + upd
        sb_ref[0, pi] = s
        for ci in range(n):
            rs = slice(ci * c, (ci + 1) * c)
            qc = q_ref[rs, cols]
            kc = k_ref[rs, cols] * (HEAD_DIM ** -0.5)
            vc = v_ref[rs, cols].astype(bf16)
            qb = qc.astype(bf16)
            a0 = _dot_nt(qb, jnp.where(lo, kc, 0.0).astype(bf16)) * dm_ref[2 * pi]
            a1 = _dot_nt(qb, jnp.where(lo, 0.0, kc).astype(bf16)) * dm_ref[2 * pi + 1]
            o = jnp.where(lo, _dot(a0.astype(bf16), vc), _dot(a1.astype(bf16), vc))
            o = o + _dot((qc * qdf).astype(bf16), kvf[ci].astype(bf16))
            o = o + _dot((qc * qdb).astype(bf16), kvb[ci].astype(bf16))
            inv_n = 1.0 / HEAD_DIM
            m0 = jnp.where(lo, o, 0.0).sum(axis=-1, keepdims=True) * inv_n
            m1 = jnp.where(lo, 0.0, o).sum(axis=-1, keepdims=True) * inv_n
            d = o - jnp.where(lo, m0, m1)
            d2 = d * d
            v0 = jnp.where(lo, d2, 0.0).sum(axis=-1, keepdims=True) * inv_n
            v1 = jnp.where(lo, 0.0, d2).sum(axis=-1, keepdims=True) * inv_n
            y = d * lax.rsqrt(jnp.where(lo, v0, v1) + EPS)
            gt = g_ref[rs, cols]
            o_ref[rs, cols] = (gt * jax.nn.sigmoid(gt) * y).astype(o_ref.dtype)


def _ret_tables(lg_f, lg_b):
    c = RET_CHUNK
    lf = jax.nn.log_sigmoid(lg_f.astype(f32))
    lb = jax.nn.log_sigmoid(lg_b.astype(f32))
    idx = jnp.arange(c, dtype=f32)

    def lanes(per_head):
        r = per_head.shape[1]
        x = jnp.repeat(per_head[:, :, None], HEAD_DIM, axis=2)
        x = x.reshape(B_HEADS // 2, 2, r, HEAD_DIM).transpose(0, 2, 1, 3)
        return x.reshape(B_HEADS // 2, r, LANES)

    qdf = lanes(jnp.exp(lf[:, None] * (idx + 1.0)))
    kdf = lanes(jnp.exp(lf[:, None] * (c - 1.0 - idx)))
    qdb = lanes(jnp.exp(lb[:, None] * (c - idx)))
    kdb = lanes(jnp.exp(lb[:, None] * idx))
    diff = idx[:, None] - idx[None, :]
    low = jnp.where(diff >= 0, jnp.exp(lf[:, None, None] * jnp.maximum(diff, 0.0)), 0.0)
    upp = jnp.where(diff <= 0, jnp.exp(lb[:, None, None] * jnp.maximum(-diff, 0.0)), 0.0)
    dm = low + upp
    cdf = lanes(jnp.exp(lf * c)[:, None])
    cdb = lanes(jnp.exp(lb * c)[:, None])
    r = jnp.arange(LANES)
    bd = ((r[:, None] < HEAD_DIM) == (r[None, :] < HEAD_DIM)).astype(f32)
    return qdf, qdb, kdf, kdb, dm, cdf, cdb, bd


def _blockdiag_states(s):
    b = s.shape[0]
    s = s.astype(f32).reshape(b, B_HEADS // 2, 2, HEAD_DIM, HEAD_DIM)
    z = jnp.zeros_like(s[:, :, 0])
    top = jnp.concatenate([s[:, :, 0], z], axis=-1)
    bot = jnp.concatenate([z, s[:, :, 1]], axis=-1)
    return jnp.concatenate([top, bot], axis=-2)


def _diag_states(s):
    b = s.shape[0]
    h0 = s[:, :, :HEAD_DIM, :HEAD_DIM]
    h1 = s[:, :, HEAD_DIM:, HEAD_DIM:]
    return jnp.stack([h0, h1], axis=2).reshape(b, B_HEADS, HEAD_DIM, HEAD_DIM)


def _retention(q, k, v, g, s0f, s0b, tables, batch):
    m = q.shape[0]
    t = m // batch
    w = B_HEADS * HEAD_DIM
    np_ = B_HEADS // 2
    c = RET_CHUNK
    kern = functools.partial(_ret_kernel, t=t)
    tok = pl.BlockSpec((t, w), lambda b: (b, 0))
    st = pl.BlockSpec((1, np_, LANES, LANES), lambda b: (b, 0, 0, 0))

    def full(a):
        nd = a.ndim
        return pl.BlockSpec(a.shape, lambda b: (0,) * nd)

    o, sf, sb = pl.pallas_call(
        kern,
        grid=(batch,),
        in_specs=[tok, tok, tok, tok, st, st] + [full(a) for a in tables],
        out_specs=[tok, st, st],
        out_shape=[jax.ShapeDtypeStruct((m, w), bf16),
                   jax.ShapeDtypeStruct((batch, np_, LANES, LANES), f32),
                   jax.ShapeDtypeStruct((batch, np_, LANES, LANES), f32)],
        scratch_shapes=[pltpu.VMEM((t // c, LANES, LANES), f32), pltpu.VMEM((t // c, LANES, LANES), f32)],
        compiler_params=_cparams(("arbitrary",)),
        name="retention",
    )(q, k, v, g, _blockdiag_states(s0f), _blockdiag_states(s0b), *tables)
    return o, _diag_states(sf), _diag_states(sb)


def _route(h, wr):
    logits = jnp.dot(h, wr, preferred_element_type=f32, precision=lax.Precision.HIGHEST)
    lane = lax.broadcasted_iota(jnp.int32, logits.shape, 1).astype(f32)
    logits = jnp.where(lane < N_EXPERTS, logits, -jnp.inf)
    m1 = logits.max(axis=-1, keepdims=True)
    i1 = jnp.where(logits == m1, lane, float(LANES)).min(axis=-1, keepdims=True)
    rest = jnp.where(lane == i1, -jnp.inf, logits)
    m2 = rest.max(axis=-1, keepdims=True)
    i2 = jnp.where(rest == m2, lane, float(LANES)).min(axis=-1, keepdims=True)
    e2 = jnp.exp(m2 - m1)
    g1 = 1.0 / (1.0 + e2)
    g2 = e2 * g1
    return jnp.where(lane == 0.0, i1, jnp.where(lane == 1.0, i2, jnp.where(lane == 2.0, g1, jnp.where(lane == 3.0, g2, 0.0))))


def _outproj_kernel(a_ref, b_ref, w_ref, x_ref, gate_ref, g2_ref, sh_ref, sc_ref, *rest, has_router):
    if has_router:
        wr_ref, xo_ref, h_ref, info_ref = rest
    else:
        xo_ref, h_ref = rest
    half = a_ref.shape[1]
    acc = _dot(a_ref[...], w_ref[:half, :]) + _dot(b_ref[...], w_ref[half:, :])
    xn = x_ref[...] + gate_ref[0] * acc
    xo_ref[...] = xn
    h = _norm_mod(xn, g2_ref[...], sh_ref[0], sc_ref[0])
    h_ref[...] = h.astype(h_ref.dtype)
    if has_router:
        info_ref[...] = _route(h, wr_ref[...])


def _outproj(a, a_blk, b, b_blk, w, x, g2, mod3, layer, rows_per_batch, is_sample, w_router=None, tm=512):
    m = x.shape[0]
    tm = min(tm, m)
    half = w.shape[0] // 2
    has_router = w_router is not None
    kern = functools.partial(_outproj_kernel, has_router=has_router)
    row = pl.BlockSpec((tm, D), lambda i: (i, 0))
    in_specs = [pl.BlockSpec((tm, half), lambda i: (i, a_blk)),
                pl.BlockSpec((tm, half), lambda i: (i, b_blk)),
                pl.BlockSpec(w.shape, lambda i: (0, 0)),
                row,
                _mod_spec(layer, 2, tm, rows_per_batch, is_sample),
                pl.BlockSpec((1, D), lambda i: (0, 0)),
                _mod_spec(layer, 3, tm, rows_per_batch, is_sample),
                _mod_spec(layer, 4, tm, rows_per_batch, is_sample)]
    args = [a, b, w, x, mod3, g2, mod3, mod3]
    out_specs = [row, row]
    out_shape = [jax.ShapeDtypeStruct((m, D), f32), jax.ShapeDtypeStruct((m, D), f32 if has_router else bf16)]
    if has_router:
        in_specs.append(pl.BlockSpec((D, LANES), lambda i: (0, 0)))
        args.append(w_router)
        out_specs.append(pl.BlockSpec((tm, LANES), lambda i: (i, 0)))
        out_shape.append(jax.ShapeDtypeStruct((m, LANES), f32))
    return pl.pallas_call(
        kern, grid=(m // tm,), in_specs=in_specs, out_specs=out_specs, out_shape=out_shape,
        compiler_params=_cparams(("arbitrary",)), name="out_proj",
    )(*args)


def _rms(x, g):
    return x * lax.rsqrt(jnp.mean(x * x, axis=-1, keepdims=True) + EPS) * g


def _swiglu_step(h, wg, wu, wd):
    g = _dot(h, wg.astype(bf16))
    u = _dot(h, wu.astype(bf16))
    act = g * jax.nn.sigmoid(g) * u
    return _dot(act.astype(bf16), wd.astype(bf16))


def _ffn_kernel(*refs, final_norm):
    h_ref, wg_ref, wu_ref, wd_ref, x_ref, gate_ref = refs[:6]
    rest = list(refs[6:])
    fg_ref = rest.pop(0) if final_norm else None
    o_ref, acc = rest
    j = pl.program_id(1)

    @pl.when(j == 0)
    def _():
        acc[...] = jnp.zeros_like(acc)

    acc[...] += _swiglu_step(h_ref[...], wg_ref[...], wu_ref[...], wd_ref[...])

    @pl.when(j == pl.num_programs(1) - 1)
    def _():
        out = x_ref[...] + gate_ref[0] * acc[...]
        o_ref[...] = _rms(out, fg_ref[...]) if final_norm else out


def _ffn(h, w_gu, w_down, x, mod3, layer, rows_per_batch, is_sample, final_g=None, tm=1024, tf=256):
    m = x.shape[0]
    tm = min(tm, m)
    f = w_down.shape[0]
    nf = f // tf
    final_norm = final_g is not None
    kern = functools.partial(_ffn_kernel, final_norm=final_norm)
    row = pl.BlockSpec((tm, D), lambda i, j: (i, 0))
    in_specs = [row,
                pl.BlockSpec((D, tf), lambda i, j: (0, j)),
                pl.BlockSpec((D, tf), lambda i, j: (0, j + nf)),
                pl.BlockSpec((tf, D), lambda i, j: (j, 0)),
                row,
                _mod_spec(layer, 5, tm, rows_per_batch, is_sample)]
    args = [h, w_gu, w_gu, w_down, x, mod3]
    if final_norm:
        in_specs.append(pl.BlockSpec((1, D), lambda i, j: (0, 0)))
        args.append(final_g)
    return pl.pallas_call(
        kern, grid=(m // tm, nf), in_specs=in_specs, out_specs=row,
        out_shape=jax.ShapeDtypeStruct((m, D), f32),
        scratch_shapes=[pltpu.VMEM((tm, D), f32)],
        compiler_params=_cparams(("arbitrary", "arbitrary")), name="ffn",
    )(*args)


MOE_TM = 1024
MOE_TF = 256
DMA_UNROLL = 8


def _moe_pos_kernel(info_ref, pos_ref, tot_ref, carry, *, tb):
    ph = pl.program_id(0)
    i = pl.program_id(1)
    lane = lax.broadcasted_iota(jnp.int32, (1, LANES), 1).astype(f32)
    info = info_ref[...]
    oh1 = (lane == info[:, 0:1]).astype(f32)
    oh2 = (lane == info[:, 1:2]).astype(f32)
    cnt = oh1 + oh2

    @pl.when((ph == 0) & (i == 0))
    def _():
        carry[...] = jnp.zeros_like(carry)

    @pl.when(ph == 0)
    def _():
        carry[0:1, :] += cnt.sum(axis=0, keepdims=True)

    @pl.when((ph == 1) & (i == 0))
    def _():
        tot = carry[0:1, :]
        tot_ref[...] = jnp.broadcast_to(tot, tot_ref.shape)
        padded = jnp.floor((tot + (MOE_TM - 1)) * (1.0 / MOE_TM)) * MOE_TM
        start = jnp.zeros_like(tot)
        for e in range(N_EXPERTS - 1):
            pe = jnp.where(lane == e, padded, 0.0).sum(axis=-1, keepdims=True)
            start = start + jnp.where(lane > e, pe, 0.0)
        carry[1:2, :] = start

    @pl.when(ph == 1)
    def _():
        r = lax.broadcasted_iota(jnp.int32, (tb, tb), 0)
        c = lax.broadcasted_iota(jnp.int32, (tb, tb), 1)
        tri = jnp.where(r > c, 1.0, 0.0).astype(bf16)
        base = carry[1:2, :] + _dot(tri, cnt.astype(bf16))
        p1 = (oh1 * base).sum(axis=-1, keepdims=True)
        p2 = (oh2 * base).sum(axis=-1, keepdims=True)
        pos_ref[...] = jnp.where(lane == 0.0, p1, jnp.where(lane == 1.0, p2, 0.0)).astype(jnp.int32)
        carry[1:2, :] += cnt.sum(axis=0, keepdims=True)


def _moe_positions(info, tb=512):
    m = info.shape[0]
    tb = min(tb, m)
    nblk = m // tb
    pos, tot = pl.pallas_call(
        functools.partial(_moe_pos_kernel, tb=tb),
        grid=(2, nblk),
        in_specs=[pl.BlockSpec((tb, LANES), lambda p, i: (i, 0))],
        out_specs=[pl.BlockSpec((tb, LANES), lambda p, i: (i * p, 0)),
                   pl.BlockSpec((8, LANES), lambda p, i: (0, 0))],
        out_shape=[jax.ShapeDtypeStruct((m, LANES), jnp.int32), jax.ShapeDtypeStruct((8, LANES), f32)],
        scratch_shapes=[pltpu.VMEM((8, LANES), f32)],
        compiler_params=_cparams(("arbitrary", "arbitrary")), name="moe_pos",
    )(info)
    return pos[:, :2].reshape(-1), tot[0, :N_EXPERTS].astype(jnp.int32)


def _row_copy(src, s_row, dst, d_row, sem):
    return pltpu.make_async_copy(src.at[pl.ds(s_row, 1)], dst.at[pl.ds(d_row, 1)], sem)


def _moe_dispatch_kernel(pos_ref, zrow_ref, zflag_ref, hp_ref, hs_ref, o_ref, zeros, sem, zsem, *, mp, ms, chunk):
    zeros[...] = jnp.zeros_like(zeros)
    nz = zrow_ref.shape[0]

    def zero_tile(z):
        return pltpu.make_async_copy(zeros, o_ref.at[pl.ds(pl.multiple_of(zrow_ref[z], MOE_TM), MOE_TM)], zsem)

    for z in range(nz):
        @pl.when(zflag_ref[z] != 0)
        def _():
            zero_tile(z).start()
    for z in range(nz):
        @pl.when(zflag_ref[z] != 0)
        def _():
            zero_tile(z).wait()

    def run(src, base, m):
        def issue(ci, slot):
            def body(r, _):
                t = ci * chunk + r
                g = 2 * (base + t)
                _row_copy(src, t, o_ref, pos_ref[g], sem.at[slot]).start()
                _row_copy(src, t, o_ref, pos_ref[g + 1], sem.at[slot]).start()
                return 0
            lax.fori_loop(0, chunk, body, 0, unroll=DMA_UNROLL)

        def drain(slot):
            def body(r, _):
                _row_copy(src, 0, o_ref, 0, sem.at[slot]).wait()
                _row_copy(src, 0, o_ref, 0, sem.at[slot]).wait()
                return 0
            lax.fori_loop(0, chunk, body, 0, unroll=DMA_UNROLL)

        n = m // chunk
        issue(0, 0)

        def step(ci, _):
            issue(ci, ci % 2)
            drain((ci - 1) % 2)
            return 0
        lax.fori_loop(1, n, step, 0)
        drain((n - 1) % 2)

    run(hp_ref, 0, mp)
    run(hs_ref, mp, ms)


def _moe_dispatch(pos, zrow, zflag, hp, hs, n_rows, chunk=256):
    mp, ms = hp.shape[0], hs.shape[0]
    kern = functools.partial(_moe_dispatch_kernel, mp=mp, ms=ms, chunk=min(chunk, mp, ms))
    return pl.pallas_call(
        kern,
        grid_spec=pltpu.PrefetchScalarGridSpec(
            num_scalar_prefetch=3, grid=(1,),
            in_specs=[pl.BlockSpec(memory_space=pl.ANY), pl.BlockSpec(memory_space=pl.ANY)],
            out_specs=pl.BlockSpec(memory_space=pl.ANY),
            scratch_shapes=[pltpu.VMEM((MOE_TM, D), f32), pltpu.SemaphoreType.DMA((2,)),
                            pltpu.SemaphoreType.DMA(())]),
        out_shape=jax.ShapeDtypeStruct((n_rows, D), f32),
        compiler_params=_cparams(("arbitrary",)), name="moe_dispatch",
    )(pos, zrow, zflag, hp, hs)


def _moe_ffn_kernel(te_ref, na_ref, h_ref, wg_ref, wu_ref, wd_ref, o_ref, hb, acc):
    i = pl.program_id(0)
    j = pl.program_id(1)
    last = j == pl.num_programs(1) - 1

    @pl.when(i < na_ref[0])
    def _():
        @pl.when(j == 0)
        def _():
            hb[...] = h_ref[...].astype(bf16)
            acc[...] = jnp.zeros_like(acc)

        acc[...] += _swiglu_step(hb[...], wg_ref[0], wu_ref[0], wd_ref[0])

        @pl.when(last)
        def _():
            o_ref[...] = acc[...]

    @pl.when((i >= na_ref[0]) & last)
    def _():
        o_ref[...] = jnp.zeros_like(o_ref)


def _moe_ffn(h_sorted, w_gu, w_down, tile_expert, n_active):
    n_rows = h_sorted.shape[0]
    tm, tf = MOE_TM, MOE_TF
    f = w_down.shape[1]
    nf = f // tf
    n_tiles = n_rows // tm

    def tile(i, na):
        return jnp.minimum(i, na[0] - 1)

    def col(i, j, na):
        return jnp.where(i < na[0], j, nf - 1)

    return pl.pallas_call(
        _moe_ffn_kernel,
        grid_spec=pltpu.PrefetchScalarGridSpec(
            num_scalar_prefetch=2, grid=(n_tiles, nf),
            in_specs=[pl.BlockSpec((tm, D), lambda i, j, te, na: (tile(i, na), 0)),
                      pl.BlockSpec((1, D, tf), lambda i, j, te, na: (te[i], 0, col(i, j, na))),
                      pl.BlockSpec((1, D, tf), lambda i, j, te, na: (te[i], 0, col(i, j, na) + nf)),
                      pl.BlockSpec((1, tf, D), lambda i, j, te, na: (te[i], col(i, j, na), 0))],
            out_specs=pl.BlockSpec((tm, D), lambda i, j, te, na: (i, 0)),
            scratch_shapes=[pltpu.VMEM((tm, D), bf16), pltpu.VMEM((tm, D), f32)]),
        out_shape=jax.ShapeDtypeStruct((n_rows, D), f32),
        compiler_params=_cparams(("arbitrary", "arbitrary")), name="moe_ffn",
    )(tile_expert, n_active, h_sorted, w_gu, w_gu, w_down)


def _moe_combine_kernel(pos_ref, info_ref, x_ref, gate_ref, fg_ref, y_ref, o_ref, ybuf, sem, *, tb, base,
                        final_norm):
    i = pl.program_id(0)
    n = pl.num_programs(0)

    def issue(blk, slot):
        def body(r, _):
            g = 2 * (base + blk * tb + r)
            _row_copy(y_ref, pos_ref[g], ybuf.at[slot, 0], r, sem.at[slot]).start()
            _row_copy(y_ref, pos_ref[g + 1], ybuf.at[slot, 1], r, sem.at[slot]).start()
            return 0
        lax.fori_loop(0, tb, body, 0, unroll=DMA_UNROLL)

    @pl.when(i == 0)
    def _():
        issue(0, 0)

    @pl.when(i + 1 < n)
    def _():
        issue(i + 1, (i + 1) % 2)

    slot = i % 2

    def drain(r, _):
        _row_copy(y_ref, 0, ybuf.at[slot, 0], 0, sem.at[slot]).wait()
        _row_copy(y_ref, 0, ybuf.at[slot, 1], 0, sem.at[slot]).wait()
        return 0
    lax.fori_loop(0, tb, drain, 0, unroll=DMA_UNROLL)

    info = info_ref[...]
    y = info[:, 2:3] * ybuf[slot, 0] + info[:, 3:4] * ybuf[slot, 1]
    out = x_ref[...] + gate_ref[0] * y
    o_ref[...] = _rms(out, fg_ref[...]) if final_norm else out


def _moe_combine(pos, info, x, mod3, layer, y_sorted, base, rows_per_batch, is_sample, final_g, tb=256):
    m = x.shape[0]
    tb = min(tb, m)
    final_norm = final_g is not None
    if final_g is None:
        final_g = jnp.ones((1, D), f32)
    kern = functools.partial(_moe_combine_kernel, tb=tb, base=base, final_norm=final_norm)
    return pl.pallas_call(
        kern,
        grid_spec=pltpu.PrefetchScalarGridSpec(
            num_scalar_prefetch=1, grid=(m // tb,),
            in_specs=[pl.BlockSpec((tb, LANES), lambda i, p: (i, 0)),
                      pl.BlockSpec((tb, D), lambda i, p: (i, 0)),
                      _mod_spec(layer, 5, tb, rows_per_batch, is_sample),
                      pl.BlockSpec((1, D), lambda i, p: (0, 0)),
                      pl.BlockSpec(memory_space=pl.ANY)],
            out_specs=pl.BlockSpec((tb, D), lambda i, p: (i, 0)),
            scratch_shapes=[pltpu.VMEM((2, 2, tb, D), f32), pltpu.SemaphoreType.DMA((2,))]),
        out_shape=jax.ShapeDtypeStruct((m, D), f32),
        compiler_params=_cparams(("arbitrary",)), name="moe_combine",
    )(pos, info, x, mod3, final_g, y_sorted)


def _moe(hp, hs, info_p, info_s, xp, xs, w_gu, w_down, mod3, layer, tp, ts, final_g):
    mp, ms = hp.shape[0], hs.shape[0]
    tm = MOE_TM
    n_tiles = (2 * (mp + ms)) // tm + N_EXPERTS
    pos, counts = _moe_positions(jnp.concatenate([info_p, info_s], axis=0))
    tiles_per = (counts + tm - 1) // tm
    cum = jnp.cumsum(tiles_per)
    n_active = cum[-1:].astype(jnp.int32)
    ids = jnp.arange(n_tiles, dtype=jnp.int32)
    te = jnp.minimum((ids[:, None] >= cum[None, :]).astype(jnp.int32).sum(axis=1), N_EXPERTS - 1)
    last_e = te[jnp.maximum(n_active[0] - 1, 0)]
    te = jnp.where(ids < n_active[0], te, last_e)
    tail = jnp.arange(n_tiles - N_EXPERTS, n_tiles, dtype=jnp.int32)
    zrow = jnp.concatenate([jnp.maximum(cum - 1, 0), tail]).astype(jnp.int32) * tm
    zflag = jnp.concatenate([counts % tm != 0, tail >= n_active[0]]).astype(jnp.int32)
    h_sorted = _moe_dispatch(pos, zrow, zflag, hp, hs, n_tiles * tm)
    y_sorted = _moe_ffn(h_sorted, w_gu, w_down, te, n_active)
    yp = _moe_combine(pos, info_p, xp, mod3, layer, y_sorted, 0, tp, False, final_g)
    ys = _moe_combine(pos, info_s, xs, mod3, layer, y_sorted, mp, ts, True, final_g)
    return yp, ys


def kernel(x_prompt, x_sample, cache_a_k, cache_a_v, state_ret_fwd, state_ret_bwd, cache_c_k, cache_c_v, c, c_ctx,
           mod_w, mod_b, norm1_g, norm2_g, final_g, ev_w_in, ev_w_out, ev_sink, ev_ret_logit_fwd, ev_ret_logit_bwd,
           ev_ffn_w_gu, ev_ffn_w_down, od_w_qkv, od_w_out, od_rpb, od_w_router, od_moe_w_gu, od_moe_w_down):
    bp, tp, _ = x_prompt.shape
    bs, ts, _ = x_sample.shape
    depth = mod_w.shape[0]
    assert 1 + bs <= MOD_ROWS
    xp = x_prompt.reshape(bp * tp, D)
    xs = x_sample.reshape(bs * ts, D)

    cvecs = jnp.zeros((MOD_ROWS, D), f32).at[0].set(c_ctx).at[1:1 + bs].set(c)
    mod3 = _adaln(cvecs, mod_w, mod_b)

    a_q = A_HEADS * HEAD_DIM
    a_kv = A_KV_HEADS * HEAD_DIM
    b_w = B_HEADS * HEAD_DIM
    c_w = C_HEADS * HEAD_DIM
    streams = ((False, tp), (True, ts))
    outs = {}
    for i in range(depth):
        j = i // 2
        g1 = norm1_g[i].reshape(1, D)
        g2 = norm2_g[i].reshape(1, D)
        last = i == depth - 1
        fg = final_g.reshape(1, D) if last else None
        if i % 2 == 0:
            w_in = ev_w_in[j].astype(bf16)
            w_out = ev_w_out[j].astype(bf16)
            splits = (a_q, a_kv, a_kv, b_w, b_w, b_w, b_w)
            dts = (f32,) * 7
            tables = _ret_tables(ev_ret_logit_fwd[j], ev_ret_logit_bwd[j])
            qa_p, ka_p, va_p, qr_p, kr_p, vr_p, gr_p = _proj(xp, g1, mod3, i, w_in, splits, dts, tp, False)
            qa_s, ka_s, va_s, qr_s, kr_s, vr_s, gr_s = _proj(xs, g1, mod3, i, w_in, splits, dts, ts, True)
            oa_p = _ctx_attn(qa_p, ka_p, va_p, ev_sink[j], bp, A_HEADS, A_KV_HEADS)
            zero = jnp.zeros((bp, B_HEADS, HEAD_DIM, HEAD_DIM), f32)
            or_p, sf, sb = _retention(qr_p, kr_p, vr_p, gr_p, zero, zero, tables, bp)
            lc = cache_a_k.shape[2]
            oa_s = _win_attn(qa_s, ka_s, va_s, cache_a_k[:, j].reshape(bs, lc, a_kv),
                             cache_a_v[:, j].reshape(bs, lc, a_kv), ev_sink[j], bs)
            or_s, _, _ = _retention(qr_s, kr_s, vr_s, gr_s, state_ret_fwd[:, j], state_ret_bwd[:, j], tables, bs)
            outs.setdefault('a_k', []).append(ka_p.reshape(bp, tp, A_KV_HEADS, HEAD_DIM))
            outs.setdefault('a_v', []).append(va_p.reshape(bp, tp, A_KV_HEADS, HEAD_DIM))
            outs.setdefault('r_f', []).append(sf)
            outs.setdefault('r_b', []).append(sb)
            xp, hp = _outproj(oa_p, 0, or_p, 0, w_out, xp, g2, mod3, i, tp, False)
            xs, hs = _outproj(oa_s, 0, or_s, 0, w_out, xs, g2, mod3, i, ts, True)
            xp = _ffn(hp, ev_ffn_w_gu[j], ev_ffn_w_down[j], xp, mod3, i, tp, False, final_g=fg)
            xs = _ffn(hs, ev_ffn_w_gu[j], ev_ffn_w_down[j], xs, mod3, i, ts, True, final_g=fg)
        else:
            w_qkv = od_w_qkv[j].astype(bf16)
            w_out = od_w_out[j].astype(bf16)
            w_router = jnp.zeros((D, LANES), f32).at[:, :N_EXPERTS].set(od_w_router[j])
            splits = (c_w, c_w, c_w)
            dts = (f32,) * 3
            q_p, k_p, v_p = _proj(xp, g1, mod3, i, w_qkv, splits, dts, tp, False)
            q_s, k_s, v_s = _proj(xs, g1, mod3, i, w_qkv, splits, dts, ts, True)
            o_p = _ctx_attn(q_p, k_p, v_p, None, bp, C_HEADS, C_HEADS)
            lc = cache_c_k.shape[2]
            o_s = _nb_attn(q_s, k_s, v_s, cache_c_k[:, j].reshape(bs, lc, c_w),
                           cache_c_v[:, j].reshape(bs, lc, c_w), od_rpb[j], bs)
            outs.setdefault('c_k', []).append(k_p.reshape(bp, tp, C_HEADS, HEAD_DIM))
            outs.setdefault('c_v', []).append(v_p.reshape(bp, tp, C_HEADS, HEAD_DIM))
            xp, hp, info_p = _outproj(o_p, 0, o_p, 1, w_out, xp, g2, mod3, i, tp, False, w_router=w_router)
            xs, hs, info_s = _outproj(o_s, 0, o_s, 1, w_out, xs, g2, mod3, i, ts, True, w_router=w_router)
            xp, xs = _moe(hp, hs, info_p, info_s, xp, xs, od_moe_w_gu[j], od_moe_w_down[j], mod3, i, tp, ts, fg)
    y_prompt = xp.reshape(bp, tp, D)
    y_sample = xs.reshape(bs, ts, D)
    return (y_prompt, y_sample,
            jnp.stack(outs['a_k'], axis=1), jnp.stack(outs['a_v'], axis=1),
            jnp.stack(outs['r_f'], axis=1), jnp.stack(outs['r_b'], axis=1),
            jnp.stack(outs['c_k'], axis=1), jnp.stack(outs['c_v'], axis=1))
```

```python
import functools

import numpy as np
import jax
import jax.numpy as jnp
from jax import lax
from jax.experimental import pallas as pl
from jax.experimental.pallas import tpu as pltpu

f32 = jnp.float32
bf16 = jnp.bfloat16

D = 1024
HEAD_DIM = 64
GRID_W = 64
ROPE_BASE = 10000.0
A_HEADS = 8
A_KV_HEADS = 2
WINDOW = 128
B_HEADS = 8
RET_CHUNK = 128
C_HEADS = 16
NB_ROWS = 8
NB_COLS = 16
N_EXPERTS = 8
EPS = 1e-6
NEG = -1e30

LANES = 128
MOD_ROWS = 16
VMEM_LIMIT_MB = 56


def _cparams(sem, vmem_mb=VMEM_LIMIT_MB):
    return pltpu.CompilerParams(dimension_semantics=sem, vmem_limit_bytes=vmem_mb * 1024 * 1024)


def _dot(a, b):
    return jnp.dot(a, b, preferred_element_type=f32)


def _dot_nt(a, b):
    return lax.dot_general(a, b, (((1,), (1,)), ((), ())), preferred_element_type=f32)


def _dot_tn(a, b):
    return lax.dot_general(a, b, (((0,), (0,)), ((), ())), preferred_element_type=f32)


def _lane_lo():
    return lax.broadcasted_iota(jnp.int32, (1, LANES), 1) < HEAD_DIM


def _swap_halves(x):
    return pltpu.roll(x, HEAD_DIM, 1)


def _mod_imap(layer, which, tm, rows_per_batch, is_sample):
    def imap(i, *_):
        r = (1 + (i * tm) // rows_per_batch) if is_sample else 0
        return ((layer * MOD_ROWS + r) * 6 + which, 0, 0)

    return imap


def _mod_spec(layer, which, tm, rows_per_batch, is_sample):
    return pl.BlockSpec((1, 1, D), _mod_imap(layer, which, tm, rows_per_batch, is_sample))


def _adaln_kernel(cv_ref, w_ref, b_ref, o_ref):
    cv = cv_ref[...]
    s = (cv * jax.nn.sigmoid(cv)).astype(bf16)
    o_ref[0] = _dot(s, w_ref[0].astype(bf16)) + b_ref[0]


def _adaln(cvecs, mod_w, mod_b):
    depth = mod_w.shape[0]
    tn = 1024
    out = pl.pallas_call(
        _adaln_kernel,
        grid=(depth, 6 * D // tn),
        in_specs=[pl.BlockSpec((MOD_ROWS, D), lambda l, j: (0, 0)),
                  pl.BlockSpec((1, D, tn), lambda l, j: (l, 0, j)),
                  pl.BlockSpec((1, 1, tn), lambda l, j: (l, 0, j))],
        out_specs=pl.BlockSpec((1, MOD_ROWS, tn), lambda l, j: (l, 0, j)),
        out_shape=jax.ShapeDtypeStruct((depth, MOD_ROWS, 6 * D), f32),
        compiler_params=_cparams(("arbitrary", "arbitrary")),
        name="adaln",
    )(cvecs, mod_w, mod_b.reshape(depth, 1, 6 * D))
    return out.reshape(depth * MOD_ROWS * 6, 1, D)


def _norm_mod(x, g, shift, scale):
    y = x * lax.rsqrt(jnp.mean(x * x, axis=-1, keepdims=True) + EPS) * g
    return y * (1.0 + scale) + shift


def _proj_kernel(x_ref, g_ref, sh_ref, sc_ref, w_ref, *out_refs, splits, chunk):
    h = _norm_mod(x_ref[...], g_ref[...], sh_ref[0], sc_ref[0]).astype(bf16)
    off = 0
    for o_ref, width in zip(out_refs, splits):
        for c0 in range(0, width, chunk):
            cw = min(chunk, width - c0)
            o_ref[:, c0:c0 + cw] = _dot(h, w_ref[:, off + c0:off + c0 + cw]).astype(o_ref.dtype)
        off += width


def _proj(x, g, mod3, layer, w, splits, dtypes, rows_per_batch, is_sample, tm=512):
    m = x.shape[0]
    tm = min(tm, m)
    n = w.shape[1]
    kern = functools.partial(_proj_kernel, splits=tuple(splits), chunk=512)
    return pl.pallas_call(
        kern,
        grid=(m // tm,),
        in_specs=[pl.BlockSpec((tm, D), lambda i: (i, 0)),
                  pl.BlockSpec((1, D), lambda i: (0, 0)),
                  _mod_spec(layer, 0, tm, rows_per_batch, is_sample),
                  _mod_spec(layer, 1, tm, rows_per_batch, is_sample),
                  pl.BlockSpec((D, n), lambda i: (0, 0))],
        out_specs=[pl.BlockSpec((tm, s), lambda i: (i, 0)) for s in splits],
        out_shape=[jax.ShapeDtypeStruct((m, s), dt) for s, dt in zip(splits, dtypes)],
        compiler_params=_cparams(("arbitrary",)),
        name="norm_proj",
    )(x, g, mod3, mod3, w)


def _softmax_parts(parts, sink):
    m = parts[0].max(axis=-1, keepdims=True)
    for s in parts[1:]:
        m = jnp.maximum(m, s.max(axis=-1, keepdims=True))
    if sink is not None:
        m = jnp.maximum(m, sink)
    es = [jnp.exp(s - m) for s in parts]
    den = es[0].sum(axis=-1, keepdims=True)
    for e in es[1:]:
        den = den + e.sum(axis=-1, keepdims=True)
    if sink is not None:
        den = den + jnp.exp(sink - m)
    inv = 1.0 / den
    return [(e * inv).astype(bf16) for e in es]


def _ctx_attn_kernel(sink_ref, q_ref, k_ref, v_ref, o_ref, *, heads, kv_heads, has_sink):
    rep = heads // kv_heads
    lo = _lane_lo()
    scale = HEAD_DIM ** -0.5
    for pi in range(heads // 2):
        qp = q_ref[:, pi * LANES:(pi + 1) * LANES] * scale
        outs = []
        for hh in range(2):
            h = 2 * pi + hh
            g = h // rep
            kp, kh = g // 2, g % 2
            qv = (qp if kh == hh else _swap_halves(qp)).astype(bf16)
            kv = k_ref[:, kp * LANES:(kp + 1) * LANES]
            km = (jnp.where(lo, kv, 0.0) if kh == 0 else jnp.where(lo, 0.0, kv)).astype(bf16)
            s = _dot_nt(qv, km)
            (p,) = _softmax_parts([s], sink_ref[h] if has_sink else None)
            o = _dot(p, v_ref[:, kp * LANES:(kp + 1) * LANES].astype(bf16))
            outs.append(o if kh == hh else _swap_halves(o))
        o_ref[:, pi * LANES:(pi + 1) * LANES] = jnp.where(lo, outs[0], outs[1]).astype(o_ref.dtype)


def _ctx_attn(q, k, v, sink, batch, heads, kv_heads):
    m = q.shape[0]
    t = m // batch
    has_sink = sink is not None
    if sink is None:
        sink = jnp.zeros((heads,), f32)
    kern = functools.partial(_ctx_attn_kernel, heads=heads, kv_heads=kv_heads, has_sink=has_sink)
    return pl.pallas_call(
        kern,
        grid=(batch,),
        in_specs=[pl.BlockSpec(memory_space=pltpu.SMEM),
                  pl.BlockSpec((t, heads * HEAD_DIM), lambda b: (b, 0)),
                  pl.BlockSpec((t, kv_heads * HEAD_DIM), lambda b: (b, 0)),
                  pl.BlockSpec((t, kv_heads * HEAD_DIM), lambda b: (b, 0))],
        out_specs=pl.BlockSpec((t, heads * HEAD_DIM), lambda b: (b, 0)),
        out_shape=jax.ShapeDtypeStruct((m, heads * HEAD_DIM), bf16),
        compiler_params=_cparams(("arbitrary",)),
        name="ctx_attn",
    )(sink.astype(f32), q, k, v)


def _rope(x, cos, sin_signed):
    lane = lax.broadcasted_iota(jnp.int32, (1, LANES), 1)
    first = (lane % 32) < 16
    rot = jnp.where(first, pltpu.roll(x, LANES - 16, 1), pltpu.roll(x, 16, 1))
    return x * cos + rot * sin_signed


def _win_attn_kernel(sink_ref, q_ref, k_ref, v_ref, ck_ref, cv_ref, cosq_ref, sinq_ref, cosk_ref, sink_k_ref,
                     o_ref, klm, vl, ckm, cvb, *, qblk, t_lat):
    n = pl.program_id(1)
    lo = _lane_lo()
    span = 3 * qblk

    @pl.when(n == 0)
    def _():
        kr = _rope(k_ref[...], cosk_ref[...], sink_k_ref[...])
        klm[0] = jnp.where(lo, kr, 0.0).astype(bf16)
        klm[1] = jnp.where(lo, 0.0, kr).astype(bf16)
        vl[...] = v_ref[...].astype(bf16)
        c = ck_ref[0]
        ckm[0] = jnp.where(lo, c, 0.0).astype(bf16)
        ckm[1] = jnp.where(lo, 0.0, c).astype(bf16)
        cvb[...] = cv_ref[0].astype(bf16)

    start = pl.multiple_of(jnp.clip(n * qblk - qblk, 0, t_lat - span), qblk)
    qpos = n * qblk + lax.broadcasted_iota(jnp.int32, (qblk, 1), 0)
    kpos = start + lax.broadcasted_iota(jnp.int32, (1, span), 1)
    valid = jnp.abs(kpos - qpos) <= WINDOW
    rep = A_HEADS // A_KV_HEADS
    scale = HEAD_DIM ** -0.5
    cq = cosq_ref[...]
    sq = sinq_ref[...]
    for pi in range(A_HEADS // 2):
        qp = _rope(q_ref[:, pi * LANES:(pi + 1) * LANES], cq, sq) * scale
        outs = []
        for hh in range(2):
            h = 2 * pi + hh
            g = h // rep
            qv = (qp if g == hh else _swap_halves(qp)).astype(bf16)
            s_c = _dot_nt(qv, ckm[g])
            s_l = _dot_nt(qv, klm[g, pl.ds(start, span), :])
            s_l = jnp.where(valid, s_l, NEG)
            p_c, p_l = _softmax_parts([s_c, s_l], sink_ref[h])
            o = _dot(p_c, cvb[...]) + _dot(p_l, vl[pl.ds(start, span), :])
            outs.append(o if g == hh else _swap_halves(o))
        o_ref[:, pi * LANES:(pi + 1) * LANES] = jnp.where(lo, outs[0], outs[1]).astype(o_ref.dtype)


def _rope_tables(t):
    half = HEAD_DIM // 2
    nf = half // 2
    pos = jnp.arange(t)
    row = (pos // GRID_W).astype(f32)
    col = (pos % GRID_W).astype(f32)
    inv = ROPE_BASE ** (-jnp.arange(nf, dtype=f32) / nf)
    ang_r = row[:, None] * inv
    ang_c = col[:, None] * inv
    ang = jnp.concatenate([ang_r, ang_r, ang_c, ang_c], axis=-1)
    sign = jnp.concatenate([-jnp.ones((nf,), f32), jnp.ones((nf,), f32)] * 2)
    cos = jnp.cos(ang)
    sin_signed = jnp.sin(ang) * sign
    return jnp.tile(cos, (1, 2)), jnp.tile(sin_signed, (1, 2))


def _win_attn(q, k, v, ck, cv, sink, batch):
    m = q.shape[0]
    t = m // batch
    qblk = 128
    nb = t // qblk
    lc = ck.shape[1]
    cos, sin_s = _rope_tables(t)
    kern = functools.partial(_win_attn_kernel, qblk=qblk, t_lat=t)
    return pl.pallas_call(
        kern,
        grid=(batch, nb),
        in_specs=[pl.BlockSpec(memory_space=pltpu.SMEM),
                  pl.BlockSpec((qblk, A_HEADS * HEAD_DIM), lambda b, n: (b * nb + n, 0)),
                  pl.BlockSpec((t, LANES), lambda b, n: (b, 0)),
                  pl.BlockSpec((t, LANES), lambda b, n: (b, 0)),
                  pl.BlockSpec((1, lc, LANES), lambda b, n: (b, 0, 0)),
                  pl.BlockSpec((1, lc, LANES), lambda b, n: (b, 0, 0)),
                  pl.BlockSpec((qblk, LANES), lambda b, n: (n, 0)),
                  pl.BlockSpec((qblk, LANES), lambda b, n: (n, 0)),
                  pl.BlockSpec((t, LANES), lambda b, n: (0, 0)),
                  pl.BlockSpec((t, LANES), lambda b, n: (0, 0))],
        out_specs=pl.BlockSpec((qblk, A_HEADS * HEAD_DIM), lambda b, n: (b * nb + n, 0)),
        out_shape=jax.ShapeDtypeStruct((m, A_HEADS * HEAD_DIM), bf16),
        scratch_shapes=[pltpu.VMEM((2, t, LANES), bf16), pltpu.VMEM((t, LANES), bf16),
                        pltpu.VMEM((2, lc, LANES), bf16), pltpu.VMEM((lc, LANES), bf16)],
        compiler_params=_cparams(("arbitrary", "arbitrary")),
        name="win_attn",
    )(sink.astype(f32), q, k, v, ck, cv, cos, sin_s, cos, sin_s)


def _nb_attn_kernel(q_ref, k_ref, v_ref, ck_ref, cv_ref, tl_ref, tr_ref, o_ref, bias, *, t_lat, qblk):
    b = pl.program_id(1)
    lo = _lane_lo()
    rows = t_lat // GRID_W
    wr = min(NB_ROWS, rows)
    neg_slot = 2 * NB_ROWS - 1

    @pl.when(b == 0)
    def _():
        for hh in range(2):
            for rq in range(rows):
                k0 = min(max(rq - wr // 2, 0), rows - wr)
                for p in range(rows // 2):
                    idx = []
                    for rk in (2 * p, 2 * p + 1):
                        idx.append(rk - rq + NB_ROWS - 1 if k0 <= rk < k0 + wr else neg_slot)
                    bias[hh, rq * GRID_W:(rq + 1) * GRID_W, p * LANES:(p + 1) * LANES] = (
                        tl_ref[hh, idx[0]] + tr_ref[hh, idx[1]])

    kb = k_ref[...]
    km = [jnp.where(lo, kb, 0.0).astype(bf16), jnp.where(lo, 0.0, kb).astype(bf16)]
    vb = v_ref[...].astype(bf16)
    cb = ck_ref[0]
    ckm = [jnp.where(lo, cb, 0.0).astype(bf16), jnp.where(lo, 0.0, cb).astype(bf16)]
    cvb = cv_ref[0].astype(bf16)
    scale = HEAD_DIM ** -0.5
    for qb in range(t_lat // qblk):
        rs = slice(qb * qblk, (qb + 1) * qblk)
        qv = (q_ref[rs, :] * scale).astype(bf16)
        outs = []
        for hh in range(2):
            s_c = _dot_nt(qv, ckm[hh])
            s_n = _dot_nt(qv, km[hh]) + bias[hh, rs, :]
            p_c, p_n = _softmax_parts([s_c, s_n], None)
            outs.append(_dot(p_c, cvb) + _dot(p_n, vb))
        o_ref[rs, :] = jnp.where(lo, outs[0], outs[1]).astype(o_ref.dtype)


def _nb_bias_tiles(rpb):
    heads = rpb.shape[0]
    cq = jnp.arange(GRID_W)
    cstart = jnp.clip(cq - NB_COLS // 2, 0, GRID_W - NB_COLS)
    col_ok = (cq[None, :] >= cstart[:, None]) & (cq[None, :] < cstart[:, None] + NB_COLS)
    dc = jnp.clip(cq[None, :] - cq[:, None], 1 - NB_COLS, NB_COLS - 1) + NB_COLS - 1
    onehot = jax.nn.one_hot(dc.reshape(-1), 2 * NB_COLS - 1, dtype=f32)
    tm = jnp.einsum('hab,kb->hak', rpb.astype(f32), onehot, precision=lax.Precision.HIGHEST)
    tm = tm.reshape(heads, 2 * NB_ROWS - 1, GRID_W, GRID_W)
    tm = jnp.where(col_ok[None, None], tm, NEG)
    tm = jnp.concatenate([tm, jnp.full((heads, 1, GRID_W, GRID_W), NEG, f32)], axis=1)
    z = jnp.zeros_like(tm)
    return jnp.concatenate([tm, z], axis=-1), jnp.concatenate([z, tm], axis=-1)


def _nb_attn(q, k, v, ck, cv, rpb, batch):
    m = q.shape[0]
    t = m // batch
    lc = ck.shape[1]
    npairs = C_HEADS // 2
    tl, tr = _nb_bias_tiles(rpb)
    kern = functools.partial(_nb_attn_kernel, t_lat=t, qblk=256)
    return pl.pallas_call(
        kern,
        grid=(npairs, batch),
        in_specs=[pl.BlockSpec((t, LANES), lambda p, b: (b, p)),
                  pl.BlockSpec((t, LANES), lambda p, b: (b, p)),
                  pl.BlockSpec((t, LANES), lambda p, b: (b, p)),
                  pl.BlockSpec((1, lc, LANES), lambda p, b: (b, 0, p)),
                  pl.BlockSpec((1, lc, LANES), lambda p, b: (b, 0, p)),
                  pl.BlockSpec((2, 2 * NB_ROWS, GRID_W, LANES), lambda p, b: (p, 0, 0, 0)),
                  pl.BlockSpec((2, 2 * NB_ROWS, GRID_W, LANES), lambda p, b: (p, 0, 0, 0))],
        out_specs=pl.BlockSpec((t, LANES), lambda p, b: (b, p)),
        out_shape=jax.ShapeDtypeStruct((m, C_HEADS * HEAD_DIM), bf16),
        scratch_shapes=[pltpu.VMEM((2, t, t), f32)],
        compiler_params=_cparams(("arbitrary", "arbitrary")),
        name="nb_attn",
    )(q, k, v, ck, cv, tl, tr)


def _ret_kernel(q_ref, k_ref, v_ref, g_ref, s0f_ref, s0b_ref, qdf_ref, qdb_ref, kdf_ref, kdb_ref, dm_ref,
                cdf_ref, cdb_ref, bd_ref, o_ref, sf_ref, sb_ref, kvf, kvb, *, t):
    c = RET_CHUNK
    n = t // c
    lo = _lane_lo()
    bd = bd_ref[...]
    for pi in range(B_HEADS // 2):
        cols = slice(pi * LANES, (pi + 1) * LANES)
        qdf, qdb, kdf, kdb = qdf_ref[pi], qdb_ref[pi], kdf_ref[pi], kdb_ref[pi]
        cdf, cdb = cdf_ref[pi], cdb_ref[pi]
        for ci in range(n):
            rs = slice(ci * c, (ci + 1) * c)
            kc = k_ref[rs, cols] * (HEAD_DIM ** -0.5)
            vc = v_ref[rs, cols].astype(bf16)
            kvf[ci] = _dot_tn((kc * kdf).astype(bf16), vc) * bd
            kvb[ci] = _dot_tn((kc * kdb).astype(bf16), vc) * bd
        s = s0f_ref[0, pi]
        for ci in range(n):
            upd = kvf[ci]
            kvf[ci] = s
            s = s * cdf + upd
        sf_ref[0, pi] = s
        s = s0b_ref[0, pi]
        for ci in range(n - 1, -1, -1):
            upd = kvb[ci]
            kvb[ci] = s
            s = s * cdb + upd
        sb_ref[0, pi] = s
        for ci in range(n):
            rs = slice(ci * c, (ci + 1) * c)
            qc = q_ref[rs, cols]
            kc = k_ref[rs, cols] * (HEAD_DIM ** -0.5)
            vc = v_ref[rs, cols].astype(bf16)
            qb = qc.astype(bf16)
            a0 = _dot_nt(qb, jnp.where(lo, kc, 0.0).astype(bf16)) * dm_ref[2 * pi]
            a1 = _dot_nt(qb, jnp.where(lo, 0.0, kc).astype(bf16)) * dm_ref[2 * pi + 1]
            o = jnp.where(lo, _dot(a0.astype(bf16), vc), _dot(a1.astype(bf16), vc))
            o = o + _dot((qc * qdf).astype(bf16), kvf[ci].astype(bf16))
            o = o + _dot((qc * qdb).astype(bf16), kvb[ci].astype(bf16))
            inv_n = 1.0 / HEAD_DIM
            m0 = jnp.where(lo, o, 0.0).sum(axis=-1, keepdims=True) * inv_n
            m1 = jnp.where(lo, 0.0, o).sum(axis=-1, keepdims=True) * inv_n
            d = o - jnp.where(lo, m0, m1)
            d2 = d * d
            v0 = jnp.where(lo, d2, 0.0).sum(axis=-1, keepdims=True) * inv_n
            v1 = jnp.where(lo, 0.0, d2).sum(axis=-1, keepdims=True) * inv_n
            y = d * lax.rsqrt(jnp.where(lo, v0, v1) + EPS)
            gt = g_ref[rs, cols]
            o_ref[rs, cols] = (gt * jax.nn.sigmoid(gt) * y).astype(o_ref.dtype)


def _ret_tables(lg_f, lg_b):
    c = RET_CHUNK
    lf = jax.nn.log_sigmoid(lg_f.astype(f32))
    lb = jax.nn.log_sigmoid(lg_b.astype(f32))
    idx = jnp.arange(c, dtype=f32)

    def lanes(per_head):
        r = per_head.shape[1]
        x = jnp.repeat(per_head[:, :, None], HEAD_DIM, axis=2)
        x = x.reshape(B_HEADS // 2, 2, r, HEAD_DIM).transpose(0, 2, 1, 3)
        return x.reshape(B_HEADS // 2, r, LANES)

    qdf = lanes(jnp.exp(lf[:, None] * (idx + 1.0)))
    kdf = lanes(jnp.exp(lf[:, None] * (c - 1.0 - idx)))
    qdb = lanes(jnp.exp(lb[:, None] * (c - idx)))
    kdb = lanes(jnp.exp(lb[:, None] * idx))
    diff = idx[:, None] - idx[None, :]
    low = jnp.where(diff >= 0, jnp.exp(lf[:, None, None] * jnp.maximum(diff, 0.0)), 0.0)
    upp = jnp.where(diff <= 0, jnp.exp(lb[:, None, None] * jnp.maximum(-diff, 0.0)), 0.0)
    dm = low + upp
    cdf = lanes(jnp.exp(lf * c)[:, None])
    cdb = lanes(jnp.exp(lb * c)[:, None])
    r = jnp.arange(LANES)
    bd = ((r[:, None] < HEAD_DIM) == (r[None, :] < HEAD_DIM)).astype(f32)
    return qdf, qdb, kdf, kdb, dm, cdf, cdb, bd


def _blockdiag_states(s):
    b = s.shape[0]
    s = s.astype(f32).reshape(b, B_HEADS // 2, 2, HEAD_DIM, HEAD_DIM)
    z = jnp.zeros_like(s[:, :, 0])
    top = jnp.concatenate([s[:, :, 0], z], axis=-1)
    bot = jnp.concatenate([z, s[:, :, 1]], axis=-1)
    return jnp.concatenate([top, bot], axis=-2)


def _diag_states(s):
    b = s.shape[0]
    h0 = s[:, :, :HEAD_DIM, :HEAD_DIM]
    h1 = s[:, :, HEAD_DIM:, HEAD_DIM:]
    return jnp.stack([h0, h1], axis=2).reshape(b, B_HEADS, HEAD_DIM, HEAD_DIM)


def _retention(q, k, v, g, s0f, s0b, tables, batch):
    m = q.shape[0]
    t = m // batch
    w = B_HEADS * HEAD_DIM
    np_ = B_HEADS // 2
    c = RET_CHUNK
    kern = functools.partial(_ret_kernel, t=t)
    tok = pl.BlockSpec((t, w), lambda b: (b, 0))
    st = pl.BlockSpec((1, np_, LANES, LANES), lambda b: (b, 0, 0, 0))

    def full(a):
        nd = a.ndim
        return pl.BlockSpec(a.shape, lambda b: (0,) * nd)

    o, sf, sb = pl.pallas_call(
        kern,
        grid=(batch,),
        in_specs=[tok, tok, tok, tok, st, st] + [full(a) for a in tables],
        out_specs=[tok, st, st],
        out_shape=[jax.ShapeDtypeStruct((m, w), bf16),
                   jax.ShapeDtypeStruct((batch, np_, LANES, LANES), f32),
                   jax.ShapeDtypeStruct((batch, np_, LANES, LANES), f32)],
        scratch_shapes=[pltpu.VMEM((t // c, LANES, LANES), f32), pltpu.VMEM((t // c, LANES, LANES), f32)],
        compiler_params=_cparams(("arbitrary",)),
        name="retention",
    )(q, k, v, g, _blockdiag_states(s0f), _blockdiag_states(s0b), *tables)
    return o, _diag_states(sf), _diag_states(sb)


def _route(h, wr):
    logits = jnp.dot(h, wr, preferred_element_type=f32, precision=lax.Precision.HIGHEST)
    lane = lax.broadcasted_iota(jnp.int32, logits.shape, 1).astype(f32)
    logits = jnp.where(lane < N_EXPERTS, logits, -jnp.inf)
    m1 = logits.max(axis=-1, keepdims=True)
    i1 = jnp.where(logits == m1, lane, float(LANES)).min(axis=-1, keepdims=True)
    rest = jnp.where(lane == i1, -jnp.inf, logits)
    m2 = rest.max(axis=-1, keepdims=True)
    i2 = jnp.where(rest == m2, lane, float(LANES)).min(axis=-1, keepdims=True)
    e2 = jnp.exp(m2 - m1)
    g1 = 1.0 / (1.0 + e2)
    g2 = e2 * g1
    return jnp.where(lane == 0.0, i1, jnp.where(lane == 1.0, i2, jnp.where(lane == 2.0, g1, jnp.where(lane == 3.0, g2, 0.0))))


def _outproj_kernel(a_ref, b_ref, w_ref, x_ref, gate_ref, g2_ref, sh_ref, sc_ref, *rest, has_router):
    if has_router:
        wr_ref, xo_ref, h_ref, info_ref = rest
    else:
        xo_ref, h_ref = rest
    half = a_ref.shape[1]
    acc = _dot(a_ref[...], w_ref[:half, :]) + _dot(b_ref[...], w_ref[half:, :])
    xn = x_ref[...] + gate_ref[0] * acc
    xo_ref[...] = xn
    h = _norm_mod(xn, g2_ref[...], sh_ref[0], sc_ref[0])
    h_ref[...] = h.astype(h_ref.dtype)
    if has_router:
        info_ref[...] = _route(h, wr_ref[...])


def _outproj(a, a_blk, b, b_blk, w, x, g2, mod3, layer, rows_per_batch, is_sample, w_router=None, tm=512):
    m = x.shape[0]
    tm = min(tm, m)
    half = w.shape[0] // 2
    has_router = w_router is not None
    kern = functools.partial(_outproj_kernel, has_router=has_router)
    row = pl.BlockSpec((tm, D), lambda i: (i, 0))
    in_specs = [pl.BlockSpec((tm, half), lambda i: (i, a_blk)),
                pl.BlockSpec((tm, half), lambda i: (i, b_blk)),
                pl.BlockSpec(w.shape, lambda i: (0, 0)),
                row,
                _mod_spec(layer, 2, tm, rows_per_batch, is_sample),
                pl.BlockSpec((1, D), lambda i: (0, 0)),
                _mod_spec(layer, 3, tm, rows_per_batch, is_sample),
                _mod_spec(layer, 4, tm, rows_per_batch, is_sample)]
    args = [a, b, w, x, mod3, g2, mod3, mod3]
    out_specs = [row, row]
    out_shape = [jax.ShapeDtypeStruct((m, D), f32), jax.ShapeDtypeStruct((m, D), f32 if has_router else bf16)]
    if has_router:
        in_specs.append(pl.BlockSpec((D, LANES), lambda i: (0, 0)))
        args.append(w_router)
        out_specs.append(pl.BlockSpec((tm, LANES), lambda i: (i, 0)))
        out_shape.append(jax.ShapeDtypeStruct((m, LANES), f32))
    return pl.pallas_call(
        kern, grid=(m // tm,), in_specs=in_specs, out_specs=out_specs, out_shape=out_shape,
        compiler_params=_cparams(("arbitrary",)), name="out_proj",
    )(*args)


def _rms(x, g):
    return x * lax.rsqrt(jnp.mean(x * x, axis=-1, keepdims=True) + EPS) * g


def _swiglu_step(h, wg, wu, wd):
    g = _dot(h, wg.astype(bf16))
    u = _dot(h, wu.astype(bf16))
    act = g * jax.nn.sigmoid(g) * u
    return _dot(act.astype(bf16), wd.astype(bf16))


def _ffn_kernel(*refs, final_norm):
    h_ref, wg_ref, wu_ref, wd_ref, x_ref, gate_ref = refs[:6]
    rest = list(refs[6:])
    fg_ref = rest.pop(0) if final_norm else None
    o_ref, acc = rest
    j = pl.program_id(1)

    @pl.when(j == 0)
    def _():
        acc[...] = jnp.zeros_like(acc)

    acc[...] += _swiglu_step(h_ref[...], wg_ref[...], wu_ref[...], wd_ref[...])

    @pl.when(j == pl.num_programs(1) - 1)
    def _():
        out = x_ref[...] + gate_ref[0] * acc[...]
        o_ref[...] = _rms(out, fg_ref[...]) if final_norm else out


def _ffn(h, w_gu, w_down, x, mod3, layer, rows_per_batch, is_sample, final_g=None, tm=1024, tf=256):
    m = x.shape[0]
    tm = min(tm, m)
    f = w_down.shape[0]
    nf = f // tf
    final_norm = final_g is not None
    kern = functools.partial(_ffn_kernel, final_norm=final_norm)
    row = pl.BlockSpec((tm, D), lambda i, j: (i, 0))
    in_specs = [row,
                pl.BlockSpec((D, tf), lambda i, j: (0, j)),
                pl.BlockSpec((D, tf), lambda i, j: (0, j + nf)),
                pl.BlockSpec((tf, D), lambda i, j: (j, 0)),
                row,
                _mod_spec(layer, 5, tm, rows_per_batch, is_sample)]
    args = [h, w_gu, w_gu, w_down, x, mod3]
    if final_norm:
        in_specs.append(pl.BlockSpec((1, D), lambda i, j: (0, 0)))
        args.append(final_g)
    return pl.pallas_call(
        kern, grid=(m // tm, nf), in_specs=in_specs, out_specs=row,
        out_shape=jax.ShapeDtypeStruct((m, D), f32),
        scratch_shapes=[pltpu.VMEM((tm, D), f32)],
        compiler_params=_cparams(("arbitrary", "arbitrary")), name="ffn",
    )(*args)


MOE_TM = 1024
MOE_TF = 256
DMA_UNROLL = 8


def _moe_pos_kernel(info_ref, pos_ref, tot_ref, carry, *, tb):
    ph = pl.program_id(0)
    i = pl.program_id(1)
    lane = lax.broadcasted_iota(jnp.int32, (1, LANES), 1).astype(f32)
    info = info_ref[...]
    oh1 = (lane == info[:, 0:1]).astype(f32)
    oh2 = (lane == info[:, 1:2]).astype(f32)
    cnt = oh1 + oh2

    @pl.when((ph == 0) & (i == 0))
    def _():
        carry[...] = jnp.zeros_like(carry)

    @pl.when(ph == 0)
    def _():
        carry[0:1, :] += cnt.sum(axis=0, keepdims=True)

    @pl.when((ph == 1) & (i == 0))
    def _():
        tot = carry[0:1, :]
        tot_ref[...] = jnp.broadcast_to(tot, tot_ref.shape)
        padded = jnp.floor((tot + (MOE_TM - 1)) * (1.0 / MOE_TM)) * MOE_TM
        start = jnp.zeros_like(tot)
        for e in range(N_EXPERTS - 1):
            pe = jnp.where(lane == e, padded, 0.0).sum(axis=-1, keepdims=True)
            start = start + jnp.where(lane > e, pe, 0.0)
        carry[1:2, :] = start

    @pl.when(ph == 1)
    def _():
        r = lax.broadcasted_iota(jnp.int32, (tb, tb), 0)
        c = lax.broadcasted_iota(jnp.int32, (tb, tb), 1)
        tri = jnp.where(r > c, 1.0, 0.0).astype(bf16)
        base = carry[1:2, :] + _dot(tri, cnt.astype(bf16))
        p1 = (oh1 * base).sum(axis=-1, keepdims=True)
        p2 = (oh2 * base).sum(axis=-1, keepdims=True)
        pos_ref[...] = jnp.where(lane == 0.0, p1, jnp.where(lane == 1.0, p2, 0.0)).astype(jnp.int32)
        carry[1:2, :] += cnt.sum(axis=0, keepdims=True)


def _moe_positions(info, tb=512):
    m = info.shape[0]
    tb = min(tb, m)
    nblk = m // tb
    pos, tot = pl.pallas_call(
        functools.partial(_moe_pos_kernel, tb=tb),
        grid=(2, nblk),
        in_specs=[pl.BlockSpec((tb, LANES), lambda p, i: (i, 0))],
        out_specs=[pl.BlockSpec((tb, LANES), lambda p, i: (i * p, 0)),
                   pl.BlockSpec((8, LANES), lambda p, i: (0, 0))],
        out_shape=[jax.ShapeDtypeStruct((m, LANES), jnp.int32), jax.ShapeDtypeStruct((8, LANES), f32)],
        scratch_shapes=[pltpu.VMEM((8, LANES), f32)],
        compiler_params=_cparams(("arbitrary", "arbitrary")), name="moe_pos",
    )(info)
    return pos[:, :2].reshape(-1), tot[0, :N_EXPERTS].astype(jnp.int32)


def _row_copy(src, s_row, dst, d_row, sem):
    return pltpu.make_async_copy(src.at[pl.ds(s_row, 1)], dst.at[pl.ds(d_row, 1)], sem)


def _moe_dispatch_kernel(pos_ref, zrow_ref, zflag_ref, hp_ref, hs_ref, o_ref, zeros, sem, zsem, *, nbp, tb):
    i = pl.program_id(0)

    @pl.when(i == 0)
    def _():
        zeros[...] = jnp.zeros_like(zeros)

        def zero_tile(z):
            dst = o_ref.at[pl.ds(pl.multiple_of(zrow_ref[z], MOE_TM), MOE_TM)]
            return pltpu.make_async_copy(zeros, dst, zsem)

        for z in range(zrow_ref.shape[0]):
            @pl.when(zflag_ref[z] != 0)
            def _():
                zero_tile(z).start()
        for z in range(zrow_ref.shape[0]):
            @pl.when(zflag_ref[z] != 0)
            def _():
                zero_tile(z).wait()

    def scatter(src):
        def issue(r, _):
            g = 2 * (i * tb + r)
            _row_copy(src, r, o_ref, pos_ref[g], sem).start()
            _row_copy(src, r, o_ref, pos_ref[g + 1], sem).start()
            return 0
        lax.fori_loop(0, tb, issue, 0, unroll=DMA_UNROLL)

        def drain(r, _):
            _row_copy(src, 0, o_ref, 0, sem).wait()
            _row_copy(src, 0, o_ref, 0, sem).wait()
            return 0
        lax.fori_loop(0, tb, drain, 0, unroll=DMA_UNROLL)

    @pl.when(i < nbp)
    def _():
        scatter(hp_ref)

    @pl.when(i >= nbp)
    def _():
        scatter(hs_ref)


def _moe_dispatch(pos, zrow, zflag, hp, hs, n_rows, tb=512):
    mp, ms = hp.shape[0], hs.shape[0]
    tb = min(tb, mp, ms)
    nbp, nbs = mp // tb, ms // tb
    kern = functools.partial(_moe_dispatch_kernel, nbp=nbp, tb=tb)
    return pl.pallas_call(
        kern,
        grid_spec=pltpu.PrefetchScalarGridSpec(
            num_scalar_prefetch=3, grid=(nbp + nbs,),
            in_specs=[pl.BlockSpec((tb, D), lambda i, *_: (jnp.minimum(i, nbp - 1), 0)),
                      pl.BlockSpec((tb, D), lambda i, *_: (jnp.maximum(i - nbp, 0), 0))],
            out_specs=pl.BlockSpec(memory_space=pl.ANY),
            scratch_shapes=[pltpu.VMEM((MOE_TM, D), f32), pltpu.SemaphoreType.DMA(()),
                            pltpu.SemaphoreType.DMA(())]),
        out_shape=jax.ShapeDtypeStruct((n_rows, D), f32),
        compiler_params=_cparams(("arbitrary",)), name="moe_dispatch",
    )(pos, zrow, zflag, hp, hs)


def _moe_ffn_kernel(te_ref, na_ref, h_ref, wg_ref, wu_ref, wd_ref, o_ref, hb, acc):
    i = pl.program_id(0)
    j = pl.program_id(1)
    last = j == pl.num_programs(1) - 1

    @pl.when(i < na_ref[0])
    def _():
        @pl.when(j == 0)
        def _():
            hb[...] = h_ref[...].astype(bf16)
            acc[...] = jnp.zeros_like(acc)

        acc[...] += _swiglu_step(hb[...], wg_ref[0], wu_ref[0], wd_ref[0])

        @pl.when(last)
        def _():
            o_ref[...] = acc[...]

    @pl.when((i >= na_ref[0]) & last)
    def _():
        o_ref[...] = jnp.zeros_like(o_ref)


def _moe_ffn(h_sorted, w_gu, w_down, tile_expert, n_active):
    n_rows = h_sorted.shape[0]
    tm, tf = MOE_TM, MOE_TF
    f = w_down.shape[1]
    nf = f // tf
    n_tiles = n_rows // tm

    def tile(i, na):
        return jnp.minimum(i, na[0] - 1)

    def col(i, j, na):
        return jnp.where(i < na[0], j, nf - 1)

    return pl.pallas_call(
        _moe_ffn_kernel,
        grid_spec=pltpu.PrefetchScalarGridSpec(
            num_scalar_prefetch=2, grid=(n_tiles, nf),
            in_specs=[pl.BlockSpec((tm, D), lambda i, j, te, na: (tile(i, na), 0)),
                      pl.BlockSpec((1, D, tf), lambda i, j, te, na: (te[i], 0, col(i, j, na))),
                      pl.BlockSpec((1, D, tf), lambda i, j, te, na: (te[i], 0, col(i, j, na) + nf)),
                      pl.BlockSpec((1, tf, D), lambda i, j, te, na: (te[i], col(i, j, na), 0))],
            out_specs=pl.BlockSpec((tm, D), lambda i, j, te, na: (i, 0)),
            scratch_shapes=[pltpu.VMEM((tm, D), bf16), pltpu.VMEM((tm, D), f32)]),
        out_shape=jax.ShapeDtypeStruct((n_rows, D), f32),
        compiler_params=_cparams(("arbitrary", "arbitrary")), name="moe_ffn",
    )(tile_expert, n_active, h_sorted, w_gu, w_gu, w_down)


def _moe_combine_kernel(pos_ref, info_ref, x_ref, gate_ref, fg_ref, y_ref, o_ref, ybuf, sem, *, tb, base,
                        final_norm):
    i = pl.program_id(0)
    n = pl.num_programs(0)

    def issue(blk, slot):
        def body(r, _):
            g = 2 * (base + blk * tb + r)
            _row_copy(y_ref, pos_ref[g], ybuf.at[slot, 0], r, sem.at[slot]).start()
            _row_copy(y_ref, pos_ref[g + 1], ybuf.at[slot, 1], r, sem.at[slot]).start()
            return 0
        lax.fori_loop(0, tb, body, 0, unroll=DMA_UNROLL)

    @pl.when(i == 0)
    def _():
        issue(0, 0)

    @pl.when(i + 1 < n)
    def _():
        issue(i + 1, (i + 1) % 2)

    slot = i % 2

    def drain(r, _):
        _row_copy(y_ref, 0, ybuf.at[slot, 0], 0, sem.at[slot]).wait()
        _row_copy(y_ref, 0, ybuf.at[slot, 1], 0, sem.at[slot]).wait()
        return 0
    lax.fori_loop(0, tb, drain, 0, unroll=DMA_UNROLL)

    info = info_ref[...]
    y = info[:, 2:3] * ybuf[slot, 0] + info[:, 3:4] * ybuf[slot, 1]
    out = x_ref[...] + gate_ref[0] * y
    o_ref[...] = _rms(out, fg_ref[...]) if final_norm else out


def _moe_combine(pos, info, x, mod3, layer, y_sorted, base, rows_per_batch, is_sample, final_g, tb=256):
    m = x.shape[0]
    tb = min(tb, m)
    final_norm = final_g is not None
    if final_g is None:
        final_g = jnp.ones((1, D), f32)
    kern = functools.partial(_moe_combine_kernel, tb=tb, base=base, final_norm=final_norm)
    return pl.pallas_call(
        kern,
        grid_spec=pltpu.PrefetchScalarGridSpec(
            num_scalar_prefetch=1, grid=(m // tb,),
            in_specs=[pl.BlockSpec((tb, LANES), lambda i, p: (i, 0)),
                      pl.BlockSpec((tb, D), lambda i, p: (i, 0)),
                      _mod_spec(layer, 5, tb, rows_per_batch, is_sample),
                      pl.BlockSpec((1, D), lambda i, p: (0, 0)),
                      pl.BlockSpec(memory_space=pl.ANY)],
            out_specs=pl.BlockSpec((tb, D), lambda i, p: (i, 0)),
            scratch_shapes=[pltpu.VMEM((2, 2, tb, D), f32), pltpu.SemaphoreType.DMA((2,))]),
        out_shape=jax.ShapeDtypeStruct((m, D), f32),
        compiler_params=_cparams(("arbitrary",)), name="moe_combine",
    )(pos, info, x, mod3, final_g, y_sorted)


def _moe(hp, hs, info_p, info_s, xp, xs, w_gu, w_down, mod3, layer, tp, ts, final_g):
    mp, ms = hp.shape[0], hs.shape[0]
    tm = MOE_TM
    n_tiles = (2 * (mp + ms)) // tm + N_EXPERTS
    pos, counts = _moe_positions(jnp.concatenate([info_p, info_s], axis=0))
    tiles_per = (counts + tm - 1) // tm
    cum = jnp.cumsum(tiles_per)
    n_active = cum[-1:].astype(jnp.int32)
    ids = jnp.arange(n_tiles, dtype=jnp.int32)
    te = jnp.minimum((ids[:, None] >= cum[None, :]).astype(jnp.int32).sum(axis=1), N_EXPERTS - 1)
    last_e = te[jnp.maximum(n_active[0] - 1, 0)]
    te = jnp.where(ids < n_active[0], te, last_e)
    tail = jnp.arange(n_tiles - N_EXPERTS, n_tiles, dtype=jnp.int32)
    zrow = jnp.concatenate([jnp.maximum(cum - 1, 0), tail]).astype(jnp.int32) * tm
    zflag = jnp.concatenate([counts % tm != 0, tail >= n_active[0]]).astype(jnp.int32)
    h_sorted = _moe_dispatch(pos, zrow, zflag, hp, hs, n_tiles * tm)
    y_sorted = _moe_ffn(h_sorted, w_gu, w_down, te, n_active)
    yp = _moe_combine(pos, info_p, xp, mod3, layer, y_sorted, 0, tp, False, final_g)
    ys = _moe_combine(pos, info_s, xs, mod3, layer, y_sorted, mp, ts, True, final_g)
    return yp, ys


def kernel(x_prompt, x_sample, cache_a_k, cache_a_v, state_ret_fwd, state_ret_bwd, cache_c_k, cache_c_v, c, c_ctx,
           mod_w, mod_b, norm1_g, norm2_g, final_g, ev_w_in, ev_w_out, ev_sink, ev_ret_logit_fwd, ev_ret_logit_bwd,
           ev_ffn_w_gu, ev_ffn_w_down, od_w_qkv, od_w_out, od_rpb, od_w_router, od_moe_w_gu, od_moe_w_down):
    bp, tp, _ = x_prompt.shape
    bs, ts, _ = x_sample.shape
    depth = mod_w.shape[0]
    assert 1 + bs <= MOD_ROWS
    xp = x_prompt.reshape(bp * tp, D)
    xs = x_sample.reshape(bs * ts, D)

    cvecs = jnp.zeros((MOD_ROWS, D), f32).at[0].set(c_ctx).at[1:1 + bs].set(c)
    mod3 = _adaln(cvecs, mod_w, mod_b)

    a_q = A_HEADS * HEAD_DIM
    a_kv = A_KV_HEADS * HEAD_DIM
    b_w = B_HEADS * HEAD_DIM
    c_w = C_HEADS * HEAD_DIM
    streams = ((False, tp), (True, ts))
    outs = {}
    for i in range(depth):
        j = i // 2
        g1 = norm1_g[i].reshape(1, D)
        g2 = norm2_g[i].reshape(1, D)
        last = i == depth - 1
        fg = final_g.reshape(1, D) if last else None
        if i % 2 == 0:
            w_in = ev_w_in[j].astype(bf16)
            w_out = ev_w_out[j].astype(bf16)
            splits = (a_q, a_kv, a_kv, b_w, b_w, b_w, b_w)
            dts = (f32,) * 7
            tables = _ret_tables(ev_ret_logit_fwd[j], ev_ret_logit_bwd[j])
            qa_p, ka_p, va_p, qr_p, kr_p, vr_p, gr_p = _proj(xp, g1, mod3, i, w_in, splits, dts, tp, False)
            qa_s, ka_s, va_s, qr_s, kr_s, vr_s, gr_s = _proj(xs, g1, mod3, i, w_in, splits, dts, ts, True)
            oa_p = _ctx_attn(qa_p, ka_p, va_p, ev_sink[j], bp, A_HEADS, A_KV_HEADS)
            zero = jnp.zeros((bp, B_HEADS, HEAD_DIM, HEAD_DIM), f32)
            or_p, sf, sb = _retention(qr_p, kr_p, vr_p, gr_p, zero, zero, tables, bp)
            lc = cache_a_k.shape[2]
            oa_s = _win_attn(qa_s, ka_s, va_s, cache_a_k[:, j].reshape(bs, lc, a_kv),
                             cache_a_v[:, j].reshape(bs, lc, a_kv), ev_sink[j], bs)
            or_s, _, _ = _retention(qr_s, kr_s, vr_s, gr_s, state_ret_fwd[:, j], state_ret_bwd[:, j], tables, bs)
            outs.setdefault('a_k', []).append(ka_p.reshape(bp, tp, A_KV_HEADS, HEAD_DIM))
            outs.setdefault('a_v', []).append(va_p.reshape(bp, tp, A_KV_HEADS, HEAD_DIM))
            outs.setdefault('r_f', []).append(sf)
            outs.setdefault('r_b', []).append(sb)
            xp, hp = _outproj(oa_p, 0, or_p, 0, w_out, xp, g2, mod3, i, tp, False)
            xs, hs = _outproj(oa_s, 0, or_s, 0, w_out, xs, g2, mod3, i, ts, True)
            xp = _ffn(hp, ev_ffn_w_gu[j], ev_ffn_w_down[j], xp, mod3, i, tp, False, final_g=fg)
            xs = _ffn(hs, ev_ffn_w_gu[j], ev_ffn_w_down[j], xs, mod3, i, ts, True, final_g=fg)
        else:
            w_qkv = od_w_qkv[j].astype(bf16)
            w_out = od_w_out[j].astype(bf16)
            w_router = jnp.zeros((D, LANES), f32).at[:, :N_EXPERTS].set(od_w_router[j])
            splits = (c_w, c_w, c_w)
            dts = (f32,) * 3
            q_p, k_p, v_p = _proj(xp, g1, mod3, i, w_qkv, splits, dts, tp, False)
            q_s, k_s, v_s = _proj(xs, g1, mod3, i, w_qkv, splits, dts, ts, True)
            o_p = _ctx_attn(q_p, k_p, v_p, None, bp, C_HEADS, C_HEADS)
            lc = cache_c_k.shape[2]
            o_s = _nb_attn(q_s, k_s, v_s, cache_c_k[:, j].reshape(bs, lc, c_w),
                           cache_c_v[:, j].reshape(bs, lc, c_w), od_rpb[j], bs)
            outs.setdefault('c_k', []).append(k_p.reshape(bp, tp, C_HEADS, HEAD_DIM))
            outs.setdefault('c_v', []).append(v_p.reshape(bp, tp, C_HEADS, HEAD_DIM))
            xp, hp, info_p = _outproj(o_p, 0, o_p, 1, w_out, xp, g2, mod3, i, tp, False, w_router=w_router)
            xs, hs, info_s = _outproj(o_s, 0, o_s, 1, w_out, xs, g2, mod3, i, ts, True, w_router=w_router)
            xp, xs = _moe(hp, hs, info_p, info_s, xp, xs, od_moe_w_gu[j], od_moe_w_down[j], mod3, i, tp, ts, fg)
    y_prompt = xp.reshape(bp, tp, D)
    y_sample = xs.reshape(bs, ts, D)
    return (y_prompt, y_sample,
            jnp.stack(outs['a_k'], axis=1), jnp.stack(outs['a_v'], axis=1),
            jnp.stack(outs['r_f'], axis=1), jnp.stack(outs['r_b'], axis=1),
            jnp.stack(outs['c_k'], axis=1), jnp.stack(outs['c_v'], axis=1))
```

```python
import functools

import numpy as np
import jax
import jax.numpy as jnp
from jax import lax
from jax.experimental import pallas as pl
from jax.experimental.pallas import tpu as pltpu

f32 = jnp.float32
bf16 = jnp.bfloat16

D = 1024
HEAD_DIM = 64
GRID_W = 64
ROPE_BASE = 10000.0
A_HEADS = 8
A_KV_HEADS = 2
WINDOW = 128
B_HEADS = 8
RET_CHUNK = 128
C_HEADS = 16
NB_ROWS = 8
NB_COLS = 16
N_EXPERTS = 8
EPS = 1e-6
NEG = -1e30

LANES = 128
MOD_ROWS = 16
VMEM_LIMIT_MB = 56


def _cparams(sem, vmem_mb=VMEM_LIMIT_MB):
    return pltpu.CompilerParams(dimension_semantics=sem, vmem_limit_bytes=vmem_mb * 1024 * 1024)


def _dot(a, b):
    return jnp.dot(a, b, preferred_element_type=f32)


def _dot_nt(a, b):
    return lax.dot_general(a, b, (((1,), (1,)), ((), ())), preferred_element_type=f32)


def _dot_tn(a, b):
    return lax.dot_general(a, b, (((0,), (0,)), ((), ())), preferred_element_type=f32)


def _lane_lo():
    return lax.broadcasted_iota(jnp.int32, (1, LANES), 1) < HEAD_DIM


def _swap_halves(x):
    return pltpu.roll(x, HEAD_DIM, 1)


def _mod_imap(layer, which, tm, rows_per_batch, is_sample):
    def imap(i, *_):
        r = (1 + (i * tm) // rows_per_batch) if is_sample else 0
        return ((layer * MOD_ROWS + r) * 6 + which, 0, 0)

    return imap


def _mod_spec(layer, which, tm, rows_per_batch, is_sample):
    return pl.BlockSpec((1, 1, D), _mod_imap(layer, which, tm, rows_per_batch, is_sample))


def _adaln_kernel(cv_ref, w_ref, b_ref, o_ref):
    cv = cv_ref[...]
    s = (cv * jax.nn.sigmoid(cv)).astype(bf16)
    o_ref[0] = _dot(s, w_ref[0].astype(bf16)) + b_ref[0]


def _adaln(cvecs, mod_w, mod_b):
    depth = mod_w.shape[0]
    tn = 1024
    out = pl.pallas_call(
        _adaln_kernel,
        grid=(depth, 6 * D // tn),
        in_specs=[pl.BlockSpec((MOD_ROWS, D), lambda l, j: (0, 0)),
                  pl.BlockSpec((1, D, tn), lambda l, j: (l, 0, j)),
                  pl.BlockSpec((1, 1, tn), lambda l, j: (l, 0, j))],
        out_specs=pl.BlockSpec((1, MOD_ROWS, tn), lambda l, j: (l, 0, j)),
        out_shape=jax.ShapeDtypeStruct((depth, MOD_ROWS, 6 * D), f32),
        compiler_params=_cparams(("arbitrary", "arbitrary")),
        name="adaln",
    )(cvecs, mod_w, mod_b.reshape(depth, 1, 6 * D))
    return out.reshape(depth * MOD_ROWS * 6, 1, D)


def _norm_mod(x, g, shift, scale):
    y = x * lax.rsqrt(jnp.mean(x * x, axis=-1, keepdims=True) + EPS) * g
    return y * (1.0 + scale) + shift


def _proj_kernel(x_ref, g_ref, sh_ref, sc_ref, w_ref, *out_refs, splits, chunk):
    h = _norm_mod(x_ref[...], g_ref[...], sh_ref[0], sc_ref[0]).astype(bf16)
    off = 0
    for o_ref, width in zip(out_refs, splits):
        for c0 in range(0, width, chunk):
            cw = min(chunk, width - c0)
            o_ref[:, c0:c0 + cw] = _dot(h, w_ref[:, off + c0:off + c0 + cw]).astype(o_ref.dtype)
        off += width


def _proj(x, g, mod3, layer, w, splits, dtypes, rows_per_batch, is_sample, tm=512):
    m = x.shape[0]
    tm = min(tm, m)
    n = w.shape[1]
    kern = functools.partial(_proj_kernel, splits=tuple(splits), chunk=512)
    return pl.pallas_call(
        kern,
        grid=(m // tm,),
        in_specs=[pl.BlockSpec((tm, D), lambda i: (i, 0)),
                  pl.BlockSpec((1, D), lambda i: (0, 0)),
                  _mod_spec(layer, 0, tm, rows_per_batch, is_sample),
                  _mod_spec(layer, 1, tm, rows_per_batch, is_sample),
                  pl.BlockSpec((D, n), lambda i: (0, 0))],
        out_specs=[pl.BlockSpec((tm, s), lambda i: (i, 0)) for s in splits],
        out_shape=[jax.ShapeDtypeStruct((m, s), dt) for s, dt in zip(splits, dtypes)],
        compiler_params=_cparams(("arbitrary",)),
        name="norm_proj",
    )(x, g, mod3, mod3, w)


def _softmax_parts(parts, sink):
    m = parts[0].max(axis=-1, keepdims=True)
    for s in parts[1:]:
        m = jnp.maximum(m, s.max(axis=-1, keepdims=True))
    if sink is not None:
        m = jnp.maximum(m, sink)
    es = [jnp.exp(s - m) for s in parts]
    den = es[0].sum(axis=-1, keepdims=True)
    for e in es[1:]:
        den = den + e.sum(axis=-1, keepdims=True)
    if sink is not None:
        den = den + jnp.exp(sink - m)
    return [e.astype(bf16) for e in es], 1.0 / den


def _ctx_attn_kernel(sink_ref, q_ref, k_ref, v_ref, o_ref, *, heads, kv_heads, has_sink):
    rep = heads // kv_heads
    lo = _lane_lo()
    scale = HEAD_DIM ** -0.5
    for pi in range(heads // 2):
        qp = q_ref[:, pi * LANES:(pi + 1) * LANES] * scale
        outs = []
        for hh in range(2):
            h = 2 * pi + hh
            g = h // rep
            kp, kh = g // 2, g % 2
            qv = (qp if kh == hh else _swap_halves(qp)).astype(bf16)
            kv = k_ref[:, kp * LANES:(kp + 1) * LANES]
            km = (jnp.where(lo, kv, 0.0) if kh == 0 else jnp.where(lo, 0.0, kv)).astype(bf16)
            s = _dot_nt(qv, km)
            (p,), inv = _softmax_parts([s], sink_ref[h] if has_sink else None)
            o = _dot(p, v_ref[:, kp * LANES:(kp + 1) * LANES].astype(bf16)) * inv
            outs.append(o if kh == hh else _swap_halves(o))
        o_ref[:, pi * LANES:(pi + 1) * LANES] = jnp.where(lo, outs[0], outs[1]).astype(o_ref.dtype)


def _ctx_attn(q, k, v, sink, batch, heads, kv_heads):
    m = q.shape[0]
    t = m // batch
    has_sink = sink is not None
    if sink is None:
        sink = jnp.zeros((heads,), f32)
    kern = functools.partial(_ctx_attn_kernel, heads=heads, kv_heads=kv_heads, has_sink=has_sink)
    return pl.pallas_call(
        kern,
        grid=(batch,),
        in_specs=[pl.BlockSpec(memory_space=pltpu.SMEM),
                  pl.BlockSpec((t, heads * HEAD_DIM), lambda b: (b, 0)),
                  pl.BlockSpec((t, kv_heads * HEAD_DIM), lambda b: (b, 0)),
                  pl.BlockSpec((t, kv_heads * HEAD_DIM), lambda b: (b, 0))],
        out_specs=pl.BlockSpec((t, heads * HEAD_DIM), lambda b: (b, 0)),
        out_shape=jax.ShapeDtypeStruct((m, heads * HEAD_DIM), bf16),
        compiler_params=_cparams(("arbitrary",)),
        name="ctx_attn",
    )(sink.astype(f32), q, k, v)


def _rope(x, cos, sin_signed):
    lane = lax.broadcasted_iota(jnp.int32, (1, LANES), 1)
    first = (lane % 32) < 16
    rot = jnp.where(first, pltpu.roll(x, LANES - 16, 1), pltpu.roll(x, 16, 1))
    return x * cos + rot * sin_signed


def _win_attn_kernel(sink_ref, q_ref, k_ref, v_ref, ck_ref, cv_ref, cosq_ref, sinq_ref, cosk_ref, sink_k_ref,
                     o_ref, klm, vl, ckm, cvb, *, qblk, t_lat):
    n = pl.program_id(1)
    lo = _lane_lo()
    span = 3 * qblk

    @pl.when(n == 0)
    def _():
        kr = _rope(k_ref[...], cosk_ref[...], sink_k_ref[...])
        klm[0] = jnp.where(lo, kr, 0.0).astype(bf16)
        klm[1] = jnp.where(lo, 0.0, kr).astype(bf16)
        vl[...] = v_ref[...].astype(bf16)
        c = ck_ref[0]
        ckm[0] = jnp.where(lo, c, 0.0).astype(bf16)
        ckm[1] = jnp.where(lo, 0.0, c).astype(bf16)
        cvb[...] = cv_ref[0].astype(bf16)

    start = pl.multiple_of(jnp.clip(n * qblk - qblk, 0, t_lat - span), qblk)
    qpos = n * qblk + lax.broadcasted_iota(jnp.int32, (qblk, 1), 0)
    kpos = start + lax.broadcasted_iota(jnp.int32, (1, span), 1)
    valid = jnp.abs(kpos - qpos) <= WINDOW
    rep = A_HEADS // A_KV_HEADS
    scale = HEAD_DIM ** -0.5
    cq = cosq_ref[...]
    sq = sinq_ref[...]
    for pi in range(A_HEADS // 2):
        qp = _rope(q_ref[:, pi * LANES:(pi + 1) * LANES], cq, sq) * scale
        outs = []
        for hh in range(2):
            h = 2 * pi + hh
            g = h // rep
            qv = (qp if g == hh else _swap_halves(qp)).astype(bf16)
            s_c = _dot_nt(qv, ckm[g])
            s_l = _dot_nt(qv, klm[g, pl.ds(start, span), :])
            s_l = jnp.where(valid, s_l, NEG)
            (p_c, p_l), inv = _softmax_parts([s_c, s_l], sink_ref[h])
            o = (_dot(p_c, cvb[...]) + _dot(p_l, vl[pl.ds(start, span), :])) * inv
            outs.append(o if g == hh else _swap_halves(o))
        o_ref[:, pi * LANES:(pi + 1) * LANES] = jnp.where(lo, outs[0], outs[1]).astype(o_ref.dtype)


def _rope_tables(t):
    half = HEAD_DIM // 2
    nf = half // 2
    pos = jnp.arange(t)
    row = (pos // GRID_W).astype(f32)
    col = (pos % GRID_W).astype(f32)
    inv = ROPE_BASE ** (-jnp.arange(nf, dtype=f32) / nf)
    ang_r = row[:, None] * inv
    ang_c = col[:, None] * inv
    ang = jnp.concatenate([ang_r, ang_r, ang_c, ang_c], axis=-1)
    sign = jnp.concatenate([-jnp.ones((nf,), f32), jnp.ones((nf,), f32)] * 2)
    cos = jnp.cos(ang)
    sin_signed = jnp.sin(ang) * sign
    return jnp.tile(cos, (1, 2)), jnp.tile(sin_signed, (1, 2))


def _win_attn(q, k, v, ck, cv, sink, batch):
    m = q.shape[0]
    t = m // batch
    qblk = 128
    nb = t // qblk
    lc = ck.shape[1]
    cos, sin_s = _rope_tables(t)
    kern = functools.partial(_win_attn_kernel, qblk=qblk, t_lat=t)
    return pl.pallas_call(
        kern,
        grid=(batch, nb),
        in_specs=[pl.BlockSpec(memory_space=pltpu.SMEM),
                  pl.BlockSpec((qblk, A_HEADS * HEAD_DIM), lambda b, n: (b * nb + n, 0)),
                  pl.BlockSpec((t, LANES), lambda b, n: (b, 0)),
                  pl.BlockSpec((t, LANES), lambda b, n: (b, 0)),
                  pl.BlockSpec((1, lc, LANES), lambda b, n: (b, 0, 0)),
                  pl.BlockSpec((1, lc, LANES), lambda b, n: (b, 0, 0)),
                  pl.BlockSpec((qblk, LANES), lambda b, n: (n, 0)),
                  pl.BlockSpec((qblk, LANES), lambda b, n: (n, 0)),
                  pl.BlockSpec((t, LANES), lambda b, n: (0, 0)),
                  pl.BlockSpec((t, LANES), lambda b, n: (0, 0))],
        out_specs=pl.BlockSpec((qblk, A_HEADS * HEAD_DIM), lambda b, n: (b * nb + n, 0)),
        out_shape=jax.ShapeDtypeStruct((m, A_HEADS * HEAD_DIM), bf16),
        scratch_shapes=[pltpu.VMEM((2, t, LANES), bf16), pltpu.VMEM((t, LANES), bf16),
                        pltpu.VMEM((2, lc, LANES), bf16), pltpu.VMEM((lc, LANES), bf16)],
        compiler_params=_cparams(("arbitrary", "arbitrary")),
        name="win_attn",
    )(sink.astype(f32), q, k, v, ck, cv, cos, sin_s, cos, sin_s)


def _nb_attn_kernel(q_ref, k_ref, v_ref, ck_ref, cv_ref, tl_ref, tr_ref, o_ref, bias, *, t_lat, qblk):
    b = pl.program_id(1)
    lo = _lane_lo()
    rows = t_lat // GRID_W
    wr = min(NB_ROWS, rows)
    neg_slot = 2 * NB_ROWS - 1

    @pl.when(b == 0)
    def _():
        for hh in range(2):
            for rq in range(rows):
                k0 = min(max(rq - wr // 2, 0), rows - wr)
                for p in range(rows // 2):
                    idx = []
                    for rk in (2 * p, 2 * p + 1):
                        idx.append(rk - rq + NB_ROWS - 1 if k0 <= rk < k0 + wr else neg_slot)
                    bias[hh, rq * GRID_W:(rq + 1) * GRID_W, p * LANES:(p + 1) * LANES] = (
                        tl_ref[hh, idx[0]] + tr_ref[hh, idx[1]])

    kb = k_ref[...]
    km = [jnp.where(lo, kb, 0.0).astype(bf16), jnp.where(lo, 0.0, kb).astype(bf16)]
    vb = v_ref[...].astype(bf16)
    cb = ck_ref[0]
    ckm = [jnp.where(lo, cb, 0.0).astype(bf16), jnp.where(lo, 0.0, cb).astype(bf16)]
    cvb = cv_ref[0].astype(bf16)
    scale = HEAD_DIM ** -0.5
    for qb in range(t_lat // qblk):
        rs = slice(qb * qblk, (qb + 1) * qblk)
        r0, r1 = qb * qblk // GRID_W, ((qb + 1) * qblk - 1) // GRID_W
        k_lo = min(max(r0 - wr // 2, 0), rows - wr) * GRID_W
        k_hi = (min(max(r1 - wr // 2, 0), rows - wr) + wr) * GRID_W
        ks = slice(k_lo // LANES * LANES, -(-k_hi // LANES) * LANES)
        qv = (q_ref[rs, :] * scale).astype(bf16)
        outs = []
        for hh in range(2):
            s_c = _dot_nt(qv, ckm[hh])
            s_n = _dot_nt(qv, km[hh][ks]) + bias[hh, rs, ks]
            (p_c, p_n), inv = _softmax_parts([s_c, s_n], None)
            outs.append((_dot(p_c, cvb) + _dot(p_n, vb[ks])) * inv)
        o_ref[rs, :] = jnp.where(lo, outs[0], outs[1]).astype(o_ref.dtype)


def _nb_bias_tiles(rpb):
    heads = rpb.shape[0]
    cq = jnp.arange(GRID_W)
    cstart = jnp.clip(cq - NB_COLS // 2, 0, GRID_W - NB_COLS)
    col_ok = (cq[None, :] >= cstart[:, None]) & (cq[None, :] < cstart[:, None] + NB_COLS)
    dc = jnp.clip(cq[None, :] - cq[:, None], 1 - NB_COLS, NB_COLS - 1) + NB_COLS - 1
    onehot = jax.nn.one_hot(dc.reshape(-1), 2 * NB_COLS - 1, dtype=f32)
    tm = jnp.einsum('hab,kb->hak', rpb.astype(f32), onehot, precision=lax.Precision.HIGHEST)
    tm = tm.reshape(heads, 2 * NB_ROWS - 1, GRID_W, GRID_W)
    tm = jnp.where(col_ok[None, None], tm, NEG)
    tm = jnp.concatenate([tm, jnp.full((heads, 1, GRID_W, GRID_W), NEG, f32)], axis=1)
    z = jnp.zeros_like(tm)
    return jnp.concatenate([tm, z], axis=-1), jnp.concatenate([z, tm], axis=-1)


def _nb_attn(q, k, v, ck, cv, rpb, batch):
    m = q.shape[0]
    t = m // batch
    lc = ck.shape[1]
    npairs = C_HEADS // 2
    tl, tr = _nb_bias_tiles(rpb)
    kern = functools.partial(_nb_attn_kernel, t_lat=t, qblk=256)
    return pl.pallas_call(
        kern,
        grid=(npairs, batch),
        in_specs=[pl.BlockSpec((t, LANES), lambda p, b: (b, p)),
                  pl.BlockSpec((t, LANES), lambda p, b: (b, p)),
                  pl.BlockSpec((t, LANES), lambda p, b: (b, p)),
                  pl.BlockSpec((1, lc, LANES), lambda p, b: (b, 0, p)),
                  pl.BlockSpec((1, lc, LANES), lambda p, b: (b, 0, p)),
                  pl.BlockSpec((2, 2 * NB_ROWS, GRID_W, LANES), lambda p, b: (p, 0, 0, 0)),
                  pl.BlockSpec((2, 2 * NB_ROWS, GRID_W, LANES), lambda p, b: (p, 0, 0, 0))],
        out_specs=pl.BlockSpec((t, LANES), lambda p, b: (b, p)),
        out_shape=jax.ShapeDtypeStruct((m, C_HEADS * HEAD_DIM), bf16),
        scratch_shapes=[pltpu.VMEM((2, t, t), f32)],
        compiler_params=_cparams(("arbitrary", "arbitrary")),
        name="nb_attn",
    )(q, k, v, ck, cv, tl, tr)


def _ret_kernel(q_ref, k_ref, v_ref, g_ref, s0f_ref, s0b_ref, qdf_ref, qdb_ref, kdf_ref, kdb_ref, dm_ref,
                cdf_ref, cdb_ref, bd_ref, o_ref, sf_ref, sb_ref, kvf, kvb, *, t):
    c = RET_CHUNK
    n = t // c
    lo = _lane_lo()
    bd = bd_ref[...]
    for pi in range(B_HEADS // 2):
        cols = slice(pi * LANES, (pi + 1) * LANES)
        qdf, qdb, kdf, kdb = qdf_ref[pi], qdb_ref[pi], kdf_ref[pi], kdb_ref[pi]
        cdf, cdb = cdf_ref[pi], cdb_ref[pi]
        for ci in range(n):
            rs = slice(ci * c, (ci + 1) * c)
            kc = k_ref[rs, cols] * (HEAD_DIM ** -0.5)
            vc = v_ref[rs, cols].astype(bf16)
            kvf[ci] = _dot_tn((kc * kdf).astype(bf16), vc) * bd
            kvb[ci] = _dot_tn((kc * kdb).astype(bf16), vc) * bd
        s = s0f_ref[0, pi]
        for ci in range(n):
            upd = kvf[ci]
            kvf[ci] = s
            s = s * cdf + upd
        sf_ref[0, pi] = s
        s = s0b_ref[0, pi]
        for ci in range(n - 1, -1, -1):
            upd = kvb[ci]
            kvb[ci] = s
            s = s * cdb + upd
        sb_ref[0, pi] = s
        for ci in range(n):
            rs = slice(ci * c, (ci + 1) * c)
            qc = q_ref[rs, cols]
            kc = k_ref[rs, cols] * (HEAD_DIM ** -0.5)
            vc = v_ref[rs, cols].astype(bf16)
            qb = qc.astype(bf16)
            a0 = _dot_nt(qb, jnp.where(lo, kc, 0.0).astype(bf16)) * dm_ref[2 * pi]
            a1 = _dot_nt(qb, jnp.where(lo, 0.0, kc).astype(bf16)) * dm_ref[2 * pi + 1]
            o = jnp.where(lo, _dot(a0.astype(bf16), vc), _dot(a1.astype(bf16), vc))
            o = o + _dot((qc * qdf).astype(bf16), kvf[ci].astype(bf16))
            o = o + _dot((qc * qdb).astype(bf16), kvb[ci].astype(bf16))
            inv_n = 1.0 / HEAD_DIM
            m0 = jnp.where(lo, o, 0.0).sum(axis=-1, keepdims=True) * inv_n
            m1 = jnp.where(lo, 0.0, o).sum(axis=-1, keepdims=True) * inv_n
            d = o - jnp.where(lo, m0, m1)
            d2 = d * d
            v0 = jnp.where(lo, d2, 0.0).sum(axis=-1, keepdims=True) * inv_n
            v1 = jnp.where(lo, 0.0, d2).sum(axis=-1, keepdims=True) * inv_n
            y = d * lax.rsqrt(jnp.where(lo, v0, v1) + EPS)
            gt = g_ref[rs, cols]
            o_ref[rs, cols] = (gt * jax.nn.sigmoid(gt) * y).astype(o_ref.dtype)


def _ret_tables(lg_f, lg_b):
    c = RET_CHUNK
    lf = jax.nn.log_sigmoid(lg_f.astype(f32))
    lb = jax.nn.log_sigmoid(lg_b.astype(f32))
    idx = jnp.arange(c, dtype=f32)

    def lanes(per_head):
        r = per_head.shape[1]
        x = jnp.repeat(per_head[:, :, None], HEAD_DIM, axis=2)
        x = x.reshape(B_HEADS // 2, 2, r, HEAD_DIM).transpose(0, 2, 1, 3)
        return x.reshape(B_HEADS // 2, r, LANES)

    qdf = lanes(jnp.exp(lf[:, None] * (idx + 1.0)))
    kdf = lanes(jnp.exp(lf[:, None] * (c - 1.0 - idx)))
    qdb = lanes(jnp.exp(lb[:, None] * (c - idx)))
    kdb = lanes(jnp.exp(lb[:, None] * idx))
    diff = idx[:, None] - idx[None, :]
    low = jnp.where(diff >= 0, jnp.exp(lf[:, None, None] * jnp.maximum(diff, 0.0)), 0.0)
    upp = jnp.where(diff <= 0, jnp.exp(lb[:, None, None] * jnp.maximum(-diff, 0.0)), 0.0)
    dm = low + upp
    cdf = lanes(jnp.exp(lf * c)[:, None])
    cdb = lanes(jnp.exp(lb * c)[:, None])
    r = jnp.arange(LANES)
    bd = ((r[:, None] < HEAD_DIM) == (r[None, :] < HEAD_DIM)).astype(f32)
    return qdf, qdb, kdf, kdb, dm, cdf, cdb, bd


def _blockdiag_states(s):
    b = s.shape[0]
    s = s.astype(f32).reshape(b, B_HEADS // 2, 2, HEAD_DIM, HEAD_DIM)
    z = jnp.zeros_like(s[:, :, 0])
    top = jnp.concatenate([s[:, :, 0], z], axis=-1)
    bot = jnp.concatenate([z, s[:, :, 1]], axis=-1)
    return jnp.concatenate([top, bot], axis=-2)


def _diag_states(s):
    b = s.shape[0]
    h0 = s[:, :, :HEAD_DIM, :HEAD_DIM]
    h1 = s[:, :, HEAD_DIM:, HEAD_DIM:]
    return jnp.stack([h0, h1], axis=2).reshape(b, B_HEADS, HEAD_DIM, HEAD_DIM)


def _retention(q, k, v, g, s0f, s0b, tables, batch):
    m = q.shape[0]
    t = m // batch
    w = B_HEADS * HEAD_DIM
    np_ = B_HEADS // 2
    c = RET_CHUNK
    kern = functools.partial(_ret_kernel, t=t)
    tok = pl.BlockSpec((t, w), lambda b: (b, 0))
    st = pl.BlockSpec((1, np_, LANES, LANES), lambda b: (b, 0, 0, 0))

    def full(a):
        nd = a.ndim
        return pl.BlockSpec(a.shape, lambda b: (0,) * nd)

    o, sf, sb = pl.pallas_call(
        kern,
        grid=(batch,),
        in_specs=[tok, tok, tok, tok, st, st] + [full(a) for a in tables],
        out_specs=[tok, st, st],
        out_shape=[jax.ShapeDtypeStruct((m, w), bf16),
                   jax.ShapeDtypeStruct((batch, np_, LANES, LANES), f32),
                   jax.ShapeDtypeStruct((batch, np_, LANES, LANES), f32)],
        scratch_shapes=[pltpu.VMEM((t // c, LANES, LANES), f32), pltpu.VMEM((t // c, LANES, LANES), f32)],
        compiler_params=_cparams(("arbitrary",)),
        name="retention",
    )(q, k, v, g, _blockdiag_states(s0f), _blockdiag_states(s0b), *tables)
    return o, _diag_states(sf), _diag_states(sb)


def _route(h, wr):
    lane = lax.broadcasted_iota(jnp.int32, (h.shape[0], LANES), 1).astype(f32)
    logits = jnp.full((h.shape[0], LANES), -jnp.inf, f32)
    for e in range(N_EXPERTS):
        logits = jnp.where(lane == e, (h * wr[e:e + 1, :]).sum(axis=-1, keepdims=True), logits)
    m1 = logits.max(axis=-1, keepdims=True)
    i1 = jnp.where(logits == m1, lane, float(LANES)).min(axis=-1, keepdims=True)
    rest = jnp.where(lane == i1, -jnp.inf, logits)
    m2 = rest.max(axis=-1, keepdims=True)
    i2 = jnp.where(rest == m2, lane, float(LANES)).min(axis=-1, keepdims=True)
    e2 = jnp.exp(m2 - m1)
    g1 = 1.0 / (1.0 + e2)
    g2 = e2 * g1
    return jnp.where(lane == 0.0, i1, jnp.where(lane == 1.0, i2, jnp.where(lane == 2.0, g1, jnp.where(lane == 3.0, g2, 0.0))))


def _outproj_kernel(a_ref, b_ref, w_ref, x_ref, gate_ref, g2_ref, sh_ref, sc_ref, *rest, has_router):
    if has_router:
        wr_ref, xo_ref, h_ref, info_ref = rest
    else:
        xo_ref, h_ref = rest
    half = a_ref.shape[1]
    acc = _dot(a_ref[...], w_ref[:half, :]) + _dot(b_ref[...], w_ref[half:, :])
    xn = x_ref[...] + gate_ref[0] * acc
    xo_ref[...] = xn
    h = _norm_mod(xn, g2_ref[...], sh_ref[0], sc_ref[0])
    h_ref[...] = h.astype(h_ref.dtype)
    if has_router:
        info_ref[...] = _route(h, wr_ref[...])


def _outproj(a, a_blk, b, b_blk, w, x, g2, mod3, layer, rows_per_batch, is_sample, w_router=None, tm=512):
    m = x.shape[0]
    tm = min(tm, m)
    half = w.shape[0] // 2
    has_router = w_router is not None
    kern = functools.partial(_outproj_kernel, has_router=has_router)
    row = pl.BlockSpec((tm, D), lambda i: (i, 0))
    in_specs = [pl.BlockSpec((tm, half), lambda i: (i, a_blk)),
                pl.BlockSpec((tm, half), lambda i: (i, b_blk)),
                pl.BlockSpec(w.shape, lambda i: (0, 0)),
                row,
                _mod_spec(layer, 2, tm, rows_per_batch, is_sample),
                pl.BlockSpec((1, D), lambda i: (0, 0)),
                _mod_spec(layer, 3, tm, rows_per_batch, is_sample),
                _mod_spec(layer, 4, tm, rows_per_batch, is_sample)]
    args = [a, b, w, x, mod3, g2, mod3, mod3]
    out_specs = [row, row]
    out_shape = [jax.ShapeDtypeStruct((m, D), f32), jax.ShapeDtypeStruct((m, D), f32 if has_router else bf16)]
    if has_router:
        in_specs.append(pl.BlockSpec((N_EXPERTS, D), lambda i: (0, 0)))
        args.append(w_router)
        out_specs.append(pl.BlockSpec((tm, LANES), lambda i: (i, 0)))
        out_shape.append(jax.ShapeDtypeStruct((m, LANES), f32))
    return pl.pallas_call(
        kern, grid=(m // tm,), in_specs=in_specs, out_specs=out_specs, out_shape=out_shape,
        compiler_params=_cparams(("arbitrary",)), name="out_proj",
    )(*args)


def _rms(x, g):
    return x * lax.rsqrt(jnp.mean(x * x, axis=-1, keepdims=True) + EPS) * g


FFN_TM = 512
FFN_TF = 256


def _swiglu(h, wgu_ref, wd_ref, act):
    f = wd_ref.shape[0]
    for c0 in range(0, f, FFN_TF):
        g = _dot(h, wgu_ref[:, c0:c0 + FFN_TF])
        u = _dot(h, wgu_ref[:, f + c0:f + c0 + FFN_TF])
        act[:, c0:c0 + FFN_TF] = (g * jax.nn.sigmoid(g) * u).astype(bf16)
    return _dot(act[...], wd_ref[...])


def _ffn_kernel(*refs, final_norm):
    h_ref, wgu_ref, wd_ref, x_ref, gate_ref = refs[:5]
    rest = list(refs[5:])
    fg_ref = rest.pop(0) if final_norm else None
    o_ref, act = rest
    out = x_ref[...] + gate_ref[0] * _swiglu(h_ref[...], wgu_ref, wd_ref, act)
    o_ref[...] = _rms(out, fg_ref[...]) if final_norm else out


def _ffn(h, w_gu, w_down, x, mod3, layer, rows_per_batch, is_sample, final_g=None, tm=FFN_TM):
    m = x.shape[0]
    tm = min(tm, m)
    f = w_down.shape[0]
    final_norm = final_g is not None
    kern = functools.partial(_ffn_kernel, final_norm=final_norm)
    row = pl.BlockSpec((tm, D), lambda i: (i, 0))
    in_specs = [row,
                pl.BlockSpec((D, 2 * f), lambda i: (0, 0)),
                pl.BlockSpec((f, D), lambda i: (0, 0)),
                row,
                _mod_spec(layer, 5, tm, rows_per_batch, is_sample)]
    args = [h, w_gu, w_down, x, mod3]
    if final_norm:
        in_specs.append(pl.BlockSpec((1, D), lambda i: (0, 0)))
        args.append(final_g)
    return pl.pallas_call(
        kern, grid=(m // tm,), in_specs=in_specs, out_specs=row,
        out_shape=jax.ShapeDtypeStruct((m, D), f32),
        scratch_shapes=[pltpu.VMEM((tm, f), bf16)],
        compiler_params=_cparams(("arbitrary",)), name="ffn",
    )(*args)


MOE_TM = FFN_TM
DMA_UNROLL = 8


def _moe_pos_kernel(info_ref, pos_ref, tot_ref, carry, *, tb):
    ph = pl.program_id(0)
    i = pl.program_id(1)
    lane = lax.broadcasted_iota(jnp.int32, (1, LANES), 1).astype(f32)
    info = info_ref[...]
    oh1 = (lane == info[:, 0:1]).astype(f32)
    oh2 = (lane == info[:, 1:2]).astype(f32)
    cnt = oh1 + oh2

    @pl.when((ph == 0) & (i == 0))
    def _():
        carry[...] = jnp.zeros_like(carry)

    @pl.when(ph == 0)
    def _():
        carry[0:1, :] += cnt.sum(axis=0, keepdims=True)

    @pl.when((ph == 1) & (i == 0))
    def _():
        tot = carry[0:1, :]
        tot_ref[...] = jnp.broadcast_to(tot, tot_ref.shape)
        padded = jnp.floor((tot + (MOE_TM - 1)) * (1.0 / MOE_TM)) * MOE_TM
        start = jnp.zeros_like(tot)
        for e in range(N_EXPERTS - 1):
            pe = jnp.where(lane == e, padded, 0.0).sum(axis=-1, keepdims=True)
            start = start + jnp.where(lane > e, pe, 0.0)
        carry[1:2, :] = start

    @pl.when(ph == 1)
    def _():
        r = lax.broadcasted_iota(jnp.int32, (tb, tb), 0)
        c = lax.broadcasted_iota(jnp.int32, (tb, tb), 1)
        tri = jnp.where(r > c, 1.0, 0.0).astype(bf16)
        base = carry[1:2, :] + _dot(tri, cnt.astype(bf16))
        p1 = (oh1 * base).sum(axis=-1, keepdims=True)
        p2 = (oh2 * base).sum(axis=-1, keepdims=True)
        pos_ref[...] = jnp.where(lane == 0.0, p1, jnp.where(lane == 1.0, p2, 0.0)).astype(jnp.int32)
        carry[1:2, :] += cnt.sum(axis=0, keepdims=True)


def _moe_positions(info, tb=512):
    m = info.shape[0]
    tb = min(tb, m)
    nblk = m // tb
    pos, tot = pl.pallas_call(
        functools.partial(_moe_pos_kernel, tb=tb),
        grid=(2, nblk),
        in_specs=[pl.BlockSpec((tb, LANES), lambda p, i: (i, 0))],
        out_specs=[pl.BlockSpec((tb, LANES), lambda p, i: (i * p, 0)),
                   pl.BlockSpec((8, LANES), lambda p, i: (0, 0))],
        out_shape=[jax.ShapeDtypeStruct((m, LANES), jnp.int32), jax.ShapeDtypeStruct((8, LANES), f32)],
        scratch_shapes=[pltpu.VMEM((8, LANES), f32)],
        compiler_params=_cparams(("arbitrary", "arbitrary")), name="moe_pos",
    )(info)
    return pos[:, :2].reshape(-1), tot[0, :N_EXPERTS].astype(jnp.int32)


def _row_copy(src, s_row, dst, d_row, sem):
    return pltpu.make_async_copy(src.at[pl.ds(s_row, 1)], dst.at[pl.ds(d_row, 1)], sem)


def _moe_dispatch_kernel(pos_ref, zrow_ref, zflag_ref, hp_ref, hs_ref, o_ref, zeros, sem, zsem, *, nbp, tb):
    i = pl.program_id(0)

    @pl.when(i == 0)
    def _():
        zeros[...] = jnp.zeros_like(zeros)

        def zero_tile(z):
            dst = o_ref.at[pl.ds(pl.multiple_of(zrow_ref[z], MOE_TM), MOE_TM)]
            return pltpu.make_async_copy(zeros, dst, zsem)

        for z in range(zrow_ref.shape[0]):
            @pl.when(zflag_ref[z] != 0)
            def _():
                zero_tile(z).start()
        for z in range(zrow_ref.shape[0]):
            @pl.when(zflag_ref[z] != 0)
            def _():
                zero_tile(z).wait()

    def scatter(src):
        def issue(r, _):
            g = 2 * (i * tb + r)
            _row_copy(src, r, o_ref, pos_ref[g], sem).start()
            _row_copy(src, r, o_ref, pos_ref[g + 1], sem).start(priority=1)
            return 0
        lax.fori_loop(0, tb, issue, 0, unroll=DMA_UNROLL)

        def drain(r, _):
            _row_copy(src, 0, o_ref, 0, sem).wait()
            _row_copy(src, 0, o_ref, 0, sem).wait()
            return 0
        lax.fori_loop(0, tb, drain, 0, unroll=DMA_UNROLL)

    @pl.when(i < nbp)
    def _():
        scatter(hp_ref)

    @pl.when(i >= nbp)
    def _():
        scatter(hs_ref)


def _moe_dispatch(pos, zrow, zflag, hp, hs, n_rows, tb=512):
    mp, ms = hp.shape[0], hs.shape[0]
    tb = min(tb, mp, ms)
    nbp, nbs = mp // tb, ms // tb
    kern = functools.partial(_moe_dispatch_kernel, nbp=nbp, tb=tb)
    return pl.pallas_call(
        kern,
        grid_spec=pltpu.PrefetchScalarGridSpec(
            num_scalar_prefetch=3, grid=(nbp + nbs,),
            in_specs=[pl.BlockSpec((tb, D), lambda i, *_: (jnp.minimum(i, nbp - 1), 0)),
                      pl.BlockSpec((tb, D), lambda i, *_: (jnp.maximum(i - nbp, 0), 0))],
            out_specs=pl.BlockSpec(memory_space=pl.ANY),
            scratch_shapes=[pltpu.VMEM((MOE_TM, D), f32), pltpu.SemaphoreType.DMA(()),
                            pltpu.SemaphoreType.DMA(())]),
        out_shape=jax.ShapeDtypeStruct((n_rows, D), f32),
        compiler_params=_cparams(("arbitrary",)), name="moe_dispatch",
    )(pos, zrow, zflag, hp, hs)


def _moe_ffn_kernel(te_ref, na_ref, h_ref, wgu_ref, wd_ref, o_ref, act):
    i = pl.program_id(0)

    @pl.when(i < na_ref[0])
    def _():
        o_ref[...] = _swiglu(h_ref[...].astype(bf16), wgu_ref.at[0], wd_ref.at[0], act)

    @pl.when(i >= na_ref[0])
    def _():
        o_ref[...] = jnp.zeros_like(o_ref)


def _moe_ffn(h_sorted, w_gu, w_down, tile_expert, n_active):
    n_rows = h_sorted.shape[0]
    tm = MOE_TM
    f = w_down.shape[1]
    n_tiles = n_rows // tm
    return pl.pallas_call(
        _moe_ffn_kernel,
        grid_spec=pltpu.PrefetchScalarGridSpec(
            num_scalar_prefetch=2, grid=(n_tiles,),
            in_specs=[pl.BlockSpec((tm, D), lambda i, te, na: (jnp.minimum(i, na[0] - 1), 0)),
                      pl.BlockSpec((1, D, 2 * f), lambda i, te, na: (te[i], 0, 0)),
                      pl.BlockSpec((1, f, D), lambda i, te, na: (te[i], 0, 0))],
            out_specs=pl.BlockSpec((tm, D), lambda i, te, na: (i, 0)),
            scratch_shapes=[pltpu.VMEM((tm, f), bf16)]),
        out_shape=jax.ShapeDtypeStruct((n_rows, D), f32),
        compiler_params=_cparams(("arbitrary",)), name="moe_ffn",
    )(tile_expert, n_active, h_sorted, w_gu, w_down)


def _moe_combine_kernel(pos_ref, info_ref, x_ref, gate_ref, fg_ref, y_ref, o_ref, ybuf, sem, *, tb, base,
                        final_norm):
    i = pl.program_id(0)
    n = pl.num_programs(0)

    def issue(blk, slot):
        def body(r, _):
            g = 2 * (base + blk * tb + r)
            _row_copy(y_ref, pos_ref[g], ybuf.at[slot, 0], r, sem.at[slot]).start()
            _row_copy(y_ref, pos_ref[g + 1], ybuf.at[slot, 1], r, sem.at[slot]).start(priority=1)
            return 0
        lax.fori_loop(0, tb, body, 0, unroll=DMA_UNROLL)

    @pl.when(i == 0)
    def _():
        issue(0, 0)

    @pl.when(i + 1 < n)
    def _():
        issue(i + 1, (i + 1) % 2)

    slot = i % 2

    def drain(r, _):
        _row_copy(y_ref, 0, ybuf.at[slot, 0], 0, sem.at[slot]).wait()
        _row_copy(y_ref, 0, ybuf.at[slot, 1], 0, sem.at[slot]).wait()
        return 0
    lax.fori_loop(0, tb, drain, 0, unroll=DMA_UNROLL)

    info = info_ref[...]
    y = info[:, 2:3] * ybuf[slot, 0] + info[:, 3:4] * ybuf[slot, 1]
    out = x_ref[...] + gate_ref[0] * y
    o_ref[...] = _rms(out, fg_ref[...]) if final_norm else out


def _moe_combine(pos, info, x, mod3, layer, y_sorted, base, rows_per_batch, is_sample, final_g, tb=256):
    m = x.shape[0]
    tb = min(tb, m)
    final_norm = final_g is not None
    if final_g is None:
        final_g = jnp.ones((1, D), f32)
    kern = functools.partial(_moe_combine_kernel, tb=tb, base=base, final_norm=final_norm)
    return pl.pallas_call(
        kern,
        grid_spec=pltpu.PrefetchScalarGridSpec(
            num_scalar_prefetch=1, grid=(m // tb,),
            in_specs=[pl.BlockSpec((tb, LANES), lambda i, p: (i, 0)),
                      pl.BlockSpec((tb, D), lambda i, p: (i, 0)),
                      _mod_spec(layer, 5, tb, rows_per_batch, is_sample),
                      pl.BlockSpec((1, D), lambda i, p: (0, 0)),
                      pl.BlockSpec(memory_space=pl.ANY)],
            out_specs=pl.BlockSpec((tb, D), lambda i, p: (i, 0)),
            scratch_shapes=[pltpu.VMEM((2, 2, tb, D), f32), pltpu.SemaphoreType.DMA((2,))]),
        out_shape=jax.ShapeDtypeStruct((m, D), f32),
        compiler_params=_cparams(("arbitrary",)), name="moe_combine",
    )(pos, info, x, mod3, final_g, y_sorted)


def _moe(hp, hs, info_p, info_s, xp, xs, w_gu, w_down, mod3, layer, tp, ts, final_g):
    mp, ms = hp.shape[0], hs.shape[0]
    tm = MOE_TM
    n_tiles = (2 * (mp + ms)) // tm + N_EXPERTS
    pos, counts = _moe_positions(jnp.concatenate([info_p, info_s], axis=0))
    tiles_per = (counts + tm - 1) // tm
    cum = jnp.cumsum(tiles_per)
    n_active = cum[-1:].astype(jnp.int32)
    ids = jnp.arange(n_tiles, dtype=jnp.int32)
    te = jnp.minimum((ids[:, None] >= cum[None, :]).astype(jnp.int32).sum(axis=1), N_EXPERTS - 1)
    last_e = te[jnp.maximum(n_active[0] - 1, 0)]
    te = jnp.where(ids < n_active[0], te, last_e)
    tail = jnp.arange(n_tiles - N_EXPERTS, n_tiles, dtype=jnp.int32)
    zrow = jnp.concatenate([jnp.maximum(cum - 1, 0), tail]).astype(jnp.int32) * tm
    zflag = jnp.concatenate([counts % tm != 0, tail >= n_active[0]]).astype(jnp.int32)
    h_sorted = _moe_dispatch(pos, zrow, zflag, hp, hs, n_tiles * tm)
    y_sorted = _moe_ffn(h_sorted, w_gu, w_down, te, n_active)
    yp = _moe_combine(pos, info_p, xp, mod3, layer, y_sorted, 0, tp, False, final_g)
    ys = _moe_combine(pos, info_s, xs, mod3, layer, y_sorted, mp, ts, True, final_g)
    return yp, ys


def kernel(x_prompt, x_sample, cache_a_k, cache_a_v, state_ret_fwd, state_ret_bwd, cache_c_k, cache_c_v, c, c_ctx,
           mod_w, mod_b, norm1_g, norm2_g, final_g, ev_w_in, ev_w_out, ev_sink, ev_ret_logit_fwd, ev_ret_logit_bwd,
           ev_ffn_w_gu, ev_ffn_w_down, od_w_qkv, od_w_out, od_rpb, od_w_router, od_moe_w_gu, od_moe_w_down):
    bp, tp, _ = x_prompt.shape
    bs, ts, _ = x_sample.shape
    depth = mod_w.shape[0]
    assert 1 + bs <= MOD_ROWS
    xp = x_prompt.reshape(bp * tp, D)
    xs = x_sample.reshape(bs * ts, D)

    cvecs = jnp.zeros((MOD_ROWS, D), f32).at[0].set(c_ctx).at[1:1 + bs].set(c)
    mod3 = _adaln(cvecs, mod_w, mod_b)

    a_q = A_HEADS * HEAD_DIM
    a_kv = A_KV_HEADS * HEAD_DIM
    b_w = B_HEADS * HEAD_DIM
    c_w = C_HEADS * HEAD_DIM
    streams = ((False, tp), (True, ts))
    outs = {}
    for i in range(depth):
        j = i // 2
        g1 = norm1_g[i].reshape(1, D)
        g2 = norm2_g[i].reshape(1, D)
        last = i == depth - 1
        fg = final_g.reshape(1, D) if last else None
        if i % 2 == 0:
            w_in = ev_w_in[j].astype(bf16)
            w_out = ev_w_out[j].astype(bf16)
            splits = (a_q, a_kv, a_kv, b_w, b_w, b_w, b_w)
            dts = (f32,) * 7
            tables = _ret_tables(ev_ret_logit_fwd[j], ev_ret_logit_bwd[j])
            qa_p, ka_p, va_p, qr_p, kr_p, vr_p, gr_p = _proj(xp, g1, mod3, i, w_in, splits, dts, tp, False)
            qa_s, ka_s, va_s, qr_s, kr_s, vr_s, gr_s = _proj(xs, g1, mod3, i, w_in, splits, dts, ts, True)
            oa_p = _ctx_attn(qa_p, ka_p, va_p, ev_sink[j], bp, A_HEADS, A_KV_HEADS)
            zero = jnp.zeros((bp, B_HEADS, HEAD_DIM, HEAD_DIM), f32)
            or_p, sf, sb = _retention(qr_p, kr_p, vr_p, gr_p, zero, zero, tables, bp)
            lc = cache_a_k.shape[2]
            oa_s = _win_attn(qa_s, ka_s, va_s, cache_a_k[:, j].reshape(bs, lc, a_kv),
                             cache_a_v[:, j].reshape(bs, lc, a_kv), ev_sink[j], bs)
            or_s, _, _ = _retention(qr_s, kr_s, vr_s, gr_s, state_ret_fwd[:, j], state_ret_bwd[:, j], tables, bs)
            outs.setdefault('a_k', []).append(ka_p.reshape(bp, tp, A_KV_HEADS, HEAD_DIM))
            outs.setdefault('a_v', []).append(va_p.reshape(bp, tp, A_KV_HEADS, HEAD_DIM))
            outs.setdefault('r_f', []).append(sf)
            outs.setdefault('r_b', []).append(sb)
            xp, hp = _outproj(oa_p, 0, or_p, 0, w_out, xp, g2, mod3, i, tp, False)
            xs, hs = _outproj(oa_s, 0, or_s, 0, w_out, xs, g2, mod3, i, ts, True)
            w_gu = ev_ffn_w_gu[j].astype(bf16)
            w_dn = ev_ffn_w_down[j].astype(bf16)
            xp = _ffn(hp, w_gu, w_dn, xp, mod3, i, tp, False, final_g=fg)
            xs = _ffn(hs, w_gu, w_dn, xs, mod3, i, ts, True, final_g=fg)
        else:
            w_qkv = od_w_qkv[j].astype(bf16)
            w_out = od_w_out[j].astype(bf16)
            w_router = od_w_router[j].astype(f32).T
            splits = (c_w, c_w, c_w)
            dts = (f32,) * 3
            q_p, k_p, v_p = _proj(xp, g1, mod3, i, w_qkv, splits, dts, tp, False)
            q_s, k_s, v_s = _proj(xs, g1, mod3, i, w_qkv, splits, dts, ts, True)
            o_p = _ctx_attn(q_p, k_p, v_p, None, bp, C_HEADS, C_HEADS)
            lc = cache_c_k.shape[2]
            o_s = _nb_attn(q_s, k_s, v_s, cache_c_k[:, j].reshape(bs, lc, c_w),
                           cache_c_v[:, j].reshape(bs, lc, c_w), od_rpb[j], bs)
            outs.setdefault('c_k', []).append(k_p.reshape(bp, tp, C_HEADS, HEAD_DIM))
            outs.setdefault('c_v', []).append(v_p.reshape(bp, tp, C_HEADS, HEAD_DIM))
            xp, hp, info_p = _outproj(o_p, 0, o_p, 1, w_out, xp, g2, mod3, i, tp, False, w_router=w_router)
            xs, hs, info_s = _outproj(o_s, 0, o_s, 1, w_out, xs, g2, mod3, i, ts, True, w_router=w_router)
            xp, xs = _moe(hp, hs, info_p, info_s, xp, xs, od_moe_w_gu[j].astype(bf16),
                          od_moe_w_down[j].astype(bf16), mod3, i, tp, ts, fg)
    y_prompt = xp.reshape(bp, tp, D)
    y_sample = xs.reshape(bs, ts, D)
    return (y_prompt, y_sample,
            jnp.stack(outs['a_k'], axis=1), jnp.stack(outs['a_v'], axis=1),
            jnp.stack(outs['r_f'], axis=1), jnp.stack(outs['r_b'], axis=1),
            jnp.stack(outs['c_k'], axis=1), jnp.stack(outs['c_v'], axis=1))
```

```python
import functools

import numpy as np
import jax
import jax.numpy as jnp
from jax import lax
from jax.experimental import pallas as pl
from jax.experimental.pallas import tpu as pltpu

f32 = jnp.float32
bf16 = jnp.bfloat16

D = 1024
HEAD_DIM = 64
GRID_W = 64
ROPE_BASE = 10000.0
A_HEADS = 8
A_KV_HEADS = 2
WINDOW = 128
B_HEADS = 8
RET_CHUNK = 128
C_HEADS = 16
NB_ROWS = 8
NB_COLS = 16
N_EXPERTS = 8
EPS = 1e-6
NEG = -1e30

LANES = 128
MOD_ROWS = 16
VMEM_LIMIT_MB = 56


def _cparams(sem, vmem_mb=VMEM_LIMIT_MB):
    return pltpu.CompilerParams(dimension_semantics=sem, vmem_limit_bytes=vmem_mb * 1024 * 1024)


def _dot(a, b):
    return jnp.dot(a, b, preferred_element_type=f32)


def _dot_nt(a, b):
    return lax.dot_general(a, b, (((1,), (1,)), ((), ())), preferred_element_type=f32)


def _dot_tn(a, b):
    return lax.dot_general(a, b, (((0,), (0,)), ((), ())), preferred_element_type=f32)


def _lane_lo():
    return lax.broadcasted_iota(jnp.int32, (1, LANES), 1) < HEAD_DIM


def _swap_halves(x):
    return pltpu.roll(x, HEAD_DIM, 1)


def _mod_imap(layer, which, tm, rows_per_batch, is_sample):
    def imap(i, *_):
        r = (1 + (i * tm) // rows_per_batch) if is_sample else 0
        return ((layer * MOD_ROWS + r) * 6 + which, 0, 0)

    return imap


def _mod_spec(layer, which, tm, rows_per_batch, is_sample):
    return pl.BlockSpec((1, 1, D), _mod_imap(layer, which, tm, rows_per_batch, is_sample))


def _adaln_kernel(cv_ref, w_ref, b_ref, o_ref):
    cv = cv_ref[...]
    s = (cv * jax.nn.sigmoid(cv)).astype(bf16)
    o_ref[0] = _dot(s, w_ref[0].astype(bf16)) + b_ref[0]


def _adaln(cvecs, mod_w, mod_b):
    depth = mod_w.shape[0]
    tn = 1024
    out = pl.pallas_call(
        _adaln_kernel,
        grid=(depth, 6 * D // tn),
        in_specs=[pl.BlockSpec((MOD_ROWS, D), lambda l, j: (0, 0)),
                  pl.BlockSpec((1, D, tn), lambda l, j: (l, 0, j)),
                  pl.BlockSpec((1, 1, tn), lambda l, j: (l, 0, j))],
        out_specs=pl.BlockSpec((1, MOD_ROWS, tn), lambda l, j: (l, 0, j)),
        out_shape=jax.ShapeDtypeStruct((depth, MOD_ROWS, 6 * D), f32),
        compiler_params=_cparams(("arbitrary", "arbitrary")),
        name="adaln",
    )(cvecs, mod_w, mod_b.reshape(depth, 1, 6 * D))
    return out.reshape(depth * MOD_ROWS * 6, 1, D)


def _norm_mod(x, g, shift, scale):
    y = x * lax.rsqrt(jnp.mean(x * x, axis=-1, keepdims=True) + EPS) * g
    return y * (1.0 + scale) + shift


def _proj_kernel(x_ref, g_ref, sh_ref, sc_ref, w_ref, *out_refs, splits, chunk):
    h = _norm_mod(x_ref[...], g_ref[...], sh_ref[0], sc_ref[0]).astype(bf16)
    off = 0
    for o_ref, width in zip(out_refs, splits):
        for c0 in range(0, width, chunk):
            cw = min(chunk, width - c0)
            o_ref[:, c0:c0 + cw] = _dot(h, w_ref[:, off + c0:off + c0 + cw]).astype(o_ref.dtype)
        off += width


def _proj(x, g, mod3, layer, w, splits, dtypes, rows_per_batch, is_sample, tm=1024):
    m = x.shape[0]
    tm = min(tm, m)
    n = w.shape[1]
    kern = functools.partial(_proj_kernel, splits=tuple(splits), chunk=512)
    return pl.pallas_call(
        kern,
        grid=(m // tm,),
        in_specs=[pl.BlockSpec((tm, D), lambda i: (i, 0)),
                  pl.BlockSpec((1, D), lambda i: (0, 0)),
                  _mod_spec(layer, 0, tm, rows_per_batch, is_sample),
                  _mod_spec(layer, 1, tm, rows_per_batch, is_sample),
                  pl.BlockSpec((D, n), lambda i: (0, 0))],
        out_specs=[pl.BlockSpec((tm, s), lambda i: (i, 0)) for s in splits],
        out_shape=[jax.ShapeDtypeStruct((m, s), dt) for s, dt in zip(splits, dtypes)],
        compiler_params=_cparams(("arbitrary",)),
        name="norm_proj",
    )(x, g, mod3, mod3, w)


def _softmax_parts(parts, sink):
    m = parts[0].max(axis=-1, keepdims=True)
    for s in parts[1:]:
        m = jnp.maximum(m, s.max(axis=-1, keepdims=True))
    if sink is not None:
        m = jnp.maximum(m, sink)
    es = [jnp.exp(s - m) for s in parts]
    den = es[0].sum(axis=-1, keepdims=True)
    for e in es[1:]:
        den = den + e.sum(axis=-1, keepdims=True)
    if sink is not None:
        den = den + jnp.exp(sink - m)
    return [e.astype(bf16) for e in es], 1.0 / den


def _ctx_attn_kernel(sink_ref, q_ref, k_ref, v_ref, o_ref, *, heads, kv_heads, has_sink):
    rep = heads // kv_heads
    lo = _lane_lo()
    scale = HEAD_DIM ** -0.5
    for pi in range(heads // 2):
        qp = q_ref[:, pi * LANES:(pi + 1) * LANES].astype(f32) * scale
        outs = []
        for hh in range(2):
            h = 2 * pi + hh
            g = h // rep
            kp, kh = g // 2, g % 2
            qv = (qp if kh == hh else _swap_halves(qp)).astype(bf16)
            kv = k_ref[:, kp * LANES:(kp + 1) * LANES]
            km = (jnp.where(lo, kv, 0.0) if kh == 0 else jnp.where(lo, 0.0, kv)).astype(bf16)
            (p,), inv = _softmax_parts([_dot_nt(qv, km)], sink_ref[h] if has_sink else None)
            o = _dot(p, v_ref[:, kp * LANES:(kp + 1) * LANES].astype(bf16)) * inv
            outs.append(o if kh == hh else _swap_halves(o))
        o_ref[:, pi * LANES:(pi + 1) * LANES] = jnp.where(lo, outs[0], outs[1]).astype(o_ref.dtype)


def _ctx_attn(q, k, v, sink, batch, heads, kv_heads):
    m = q.shape[0]
    t = m // batch
    has_sink = sink is not None
    if sink is None:
        sink = jnp.zeros((heads,), f32)
    kern = functools.partial(_ctx_attn_kernel, heads=heads, kv_heads=kv_heads, has_sink=has_sink)
    return pl.pallas_call(
        kern,
        grid=(batch,),
        in_specs=[pl.BlockSpec(memory_space=pltpu.SMEM),
                  pl.BlockSpec((t, heads * HEAD_DIM), lambda b: (b, 0)),
                  pl.BlockSpec((t, kv_heads * HEAD_DIM), lambda b: (b, 0)),
                  pl.BlockSpec((t, kv_heads * HEAD_DIM), lambda b: (b, 0))],
        out_specs=pl.BlockSpec((t, heads * HEAD_DIM), lambda b: (b, 0)),
        out_shape=jax.ShapeDtypeStruct((m, heads * HEAD_DIM), bf16),
        compiler_params=_cparams(("arbitrary",)),
        name="ctx_attn",
    )(sink.astype(f32), q, k, v)


def _rope(x, cos, sin_signed):
    lane = lax.broadcasted_iota(jnp.int32, (1, LANES), 1)
    first = (lane % 32) < 16
    rot = jnp.where(first, pltpu.roll(x, LANES - 16, 1), pltpu.roll(x, 16, 1))
    return x * cos + rot * sin_signed


def _win_attn_kernel(sink_ref, q_ref, k_ref, v_ref, ck_ref, cv_ref, cosq_ref, sinq_ref, cosk_ref, sink_k_ref,
                     o_ref, klm, vl, ckm, cvb, *, qblk, t_lat):
    n = pl.program_id(1)
    lo = _lane_lo()
    span = 3 * qblk

    @pl.when(n == 0)
    def _():
        kr = _rope(k_ref[...], cosk_ref[...], sink_k_ref[...])
        klm[0] = jnp.where(lo, kr, 0.0).astype(bf16)
        klm[1] = jnp.where(lo, 0.0, kr).astype(bf16)
        vl[...] = v_ref[...].astype(bf16)
        c = ck_ref[0]
        ckm[0] = jnp.where(lo, c, 0.0).astype(bf16)
        ckm[1] = jnp.where(lo, 0.0, c).astype(bf16)
        cvb[...] = cv_ref[0].astype(bf16)

    start = pl.multiple_of(jnp.clip(n * qblk - qblk, 0, t_lat - span), qblk)
    rep = A_HEADS // A_KV_HEADS
    qpos = n * qblk + lax.broadcasted_iota(jnp.int32, (rep * qblk, 1), 0) % qblk
    kpos = start + lax.broadcasted_iota(jnp.int32, (1, span), 1)
    valid = jnp.abs(kpos - qpos) <= WINDOW
    scale = HEAD_DIM ** -0.5
    cq = cosq_ref[...]
    sq = sinq_ref[...]
    roped = [_rope(q_ref[:, pi * LANES:(pi + 1) * LANES], cq, sq) * scale for pi in range(A_HEADS // 2)]
    outs = [None] * A_HEADS
    for g in range(A_KV_HEADS):
        group = range(g * rep, (g + 1) * rep)
        qst = jnp.concatenate([roped[h // 2] if h % 2 == g else _swap_halves(roped[h // 2]) for h in group],
                              axis=0).astype(bf16)
        sink = jnp.concatenate([jnp.full((qblk, 1), sink_ref[h], f32) for h in group], axis=0)
        s_c = _dot_nt(qst, ckm[g])
        s_l = jnp.where(valid, _dot_nt(qst, klm[g, pl.ds(start, span), :]), NEG)
        (p_c, p_l), inv = _softmax_parts([s_c, s_l], sink)
        o = (_dot(p_c, cvb[...]) + _dot(p_l, vl[pl.ds(start, span), :])) * inv
        for idx, h in enumerate(group):
            oh = o[idx * qblk:(idx + 1) * qblk]
            outs[h] = oh if h % 2 == g else _swap_halves(oh)
    for pi in range(A_HEADS // 2):
        o_ref[:, pi * LANES:(pi + 1) * LANES] = jnp.where(lo, outs[2 * pi], outs[2 * pi + 1]).astype(o_ref.dtype)


def _rope_tables(t):
    half = HEAD_DIM // 2
    nf = half // 2
    pos = jnp.arange(t)
    row = (pos // GRID_W).astype(f32)
    col = (pos % GRID_W).astype(f32)
    inv = ROPE_BASE ** (-jnp.arange(nf, dtype=f32) / nf)
    ang_r = row[:, None] * inv
    ang_c = col[:, None] * inv
    ang = jnp.concatenate([ang_r, ang_r, ang_c, ang_c], axis=-1)
    sign = jnp.concatenate([-jnp.ones((nf,), f32), jnp.ones((nf,), f32)] * 2)
    cos = jnp.cos(ang)
    sin_signed = jnp.sin(ang) * sign
    return jnp.tile(cos, (1, 2)), jnp.tile(sin_signed, (1, 2))


def _win_attn(q, k, v, ck, cv, sink, batch):
    m = q.shape[0]
    t = m // batch
    qblk = 128
    nb = t // qblk
    lc = ck.shape[1]
    cos, sin_s = _rope_tables(t)
    kern = functools.partial(_win_attn_kernel, qblk=qblk, t_lat=t)
    return pl.pallas_call(
        kern,
        grid=(batch, nb),
        in_specs=[pl.BlockSpec(memory_space=pltpu.SMEM),
                  pl.BlockSpec((qblk, A_HEADS * HEAD_DIM), lambda b, n: (b * nb + n, 0)),
                  pl.BlockSpec((t, LANES), lambda b, n: (b, 0)),
                  pl.BlockSpec((t, LANES), lambda b, n: (b, 0)),
                  pl.BlockSpec((1, lc, LANES), lambda b, n: (b, 0, 0)),
                  pl.BlockSpec((1, lc, LANES), lambda b, n: (b, 0, 0)),
                  pl.BlockSpec((qblk, LANES), lambda b, n: (n, 0)),
                  pl.BlockSpec((qblk, LANES), lambda b, n: (n, 0)),
                  pl.BlockSpec((t, LANES), lambda b, n: (0, 0)),
                  pl.BlockSpec((t, LANES), lambda b, n: (0, 0))],
        out_specs=pl.BlockSpec((qblk, A_HEADS * HEAD_DIM), lambda b, n: (b * nb + n, 0)),
        out_shape=jax.ShapeDtypeStruct((m, A_HEADS * HEAD_DIM), bf16),
        scratch_shapes=[pltpu.VMEM((2, t, LANES), bf16), pltpu.VMEM((t, LANES), bf16),
                        pltpu.VMEM((2, lc, LANES), bf16), pltpu.VMEM((lc, LANES), bf16)],
        compiler_params=_cparams(("arbitrary", "arbitrary")),
        name="win_attn",
    )(sink.astype(f32), q, k, v, ck, cv, cos, sin_s, cos, sin_s)


def _nb_attn_kernel(q_ref, k_ref, v_ref, ck_ref, cv_ref, tl_ref, tr_ref, o_ref, bias, *, t_lat, qblk):
    b = pl.program_id(1)
    lo = _lane_lo()
    rows = t_lat // GRID_W
    wr = min(NB_ROWS, rows)
    neg_slot = 2 * NB_ROWS - 1

    @pl.when(b == 0)
    def _():
        for hh in range(2):
            for rq in range(rows):
                k0 = min(max(rq - wr // 2, 0), rows - wr)
                for p in range(rows // 2):
                    idx = []
                    for rk in (2 * p, 2 * p + 1):
                        idx.append(rk - rq + NB_ROWS - 1 if k0 <= rk < k0 + wr else neg_slot)
                    bias[hh, rq * GRID_W:(rq + 1) * GRID_W, p * LANES:(p + 1) * LANES] = (
                        tl_ref[hh, idx[0]] + tr_ref[hh, idx[1]])

    kb = k_ref[...]
    km = [jnp.where(lo, kb, 0.0).astype(bf16), jnp.where(lo, 0.0, kb).astype(bf16)]
    vb = v_ref[...].astype(bf16)
    cb = ck_ref[0]
    ckm = [jnp.where(lo, cb, 0.0).astype(bf16), jnp.where(lo, 0.0, cb).astype(bf16)]
    cvb = cv_ref[0].astype(bf16)
    scale = HEAD_DIM ** -0.5
    for qb in range(t_lat // qblk):
        rs = slice(qb * qblk, (qb + 1) * qblk)
        r0, r1 = qb * qblk // GRID_W, ((qb + 1) * qblk - 1) // GRID_W
        k_lo = min(max(r0 - wr // 2, 0), rows - wr) * GRID_W
        k_hi = (min(max(r1 - wr // 2, 0), rows - wr) + wr) * GRID_W
        ks = slice(k_lo // LANES * LANES, -(-k_hi // LANES) * LANES)
        qv = (q_ref[rs, :] * scale).astype(bf16)
        outs = []
        for hh in range(2):
            s_c = _dot_nt(qv, ckm[hh])
            s_n = _dot_nt(qv, km[hh][ks]) + bias[hh, rs, ks]
            (p_c, p_n), inv = _softmax_parts([s_c, s_n], None)
            outs.append((_dot(p_c, cvb) + _dot(p_n, vb[ks])) * inv)
        o_ref[rs, :] = jnp.where(lo, outs[0], outs[1]).astype(o_ref.dtype)


def _nb_bias_tiles(rpb):
    heads = rpb.shape[0]
    cq = jnp.arange(GRID_W)
    cstart = jnp.clip(cq - NB_COLS // 2, 0, GRID_W - NB_COLS)
    col_ok = (cq[None, :] >= cstart[:, None]) & (cq[None, :] < cstart[:, None] + NB_COLS)
    dc = jnp.clip(cq[None, :] - cq[:, None], 1 - NB_COLS, NB_COLS - 1) + NB_COLS - 1
    onehot = jax.nn.one_hot(dc.reshape(-1), 2 * NB_COLS - 1, dtype=f32)
    tm = jnp.einsum('hab,kb->hak', rpb.astype(f32), onehot, precision=lax.Precision.HIGHEST)
    tm = tm.reshape(heads, 2 * NB_ROWS - 1, GRID_W, GRID_W)
    tm = jnp.where(col_ok[None, None], tm, NEG)
    tm = jnp.concatenate([tm, jnp.full((heads, 1, GRID_W, GRID_W), NEG, f32)], axis=1)
    z = jnp.zeros_like(tm)
    return jnp.concatenate([tm, z], axis=-1), jnp.concatenate([z, tm], axis=-1)


def _nb_attn(q, k, v, ck, cv, rpb, batch):
    m = q.shape[0]
    t = m // batch
    lc = ck.shape[1]
    npairs = C_HEADS // 2
    tl, tr = _nb_bias_tiles(rpb)
    kern = functools.partial(_nb_attn_kernel, t_lat=t, qblk=256)
    return pl.pallas_call(
        kern,
        grid=(npairs, batch),
        in_specs=[pl.BlockSpec((t, LANES), lambda p, b: (b, p)),
                  pl.BlockSpec((t, LANES), lambda p, b: (b, p)),
                  pl.BlockSpec((t, LANES), lambda p, b: (b, p)),
                  pl.BlockSpec((1, lc, LANES), lambda p, b: (b, 0, p)),
                  pl.BlockSpec((1, lc, LANES), lambda p, b: (b, 0, p)),
                  pl.BlockSpec((2, 2 * NB_ROWS, GRID_W, LANES), lambda p, b: (p, 0, 0, 0)),
                  pl.BlockSpec((2, 2 * NB_ROWS, GRID_W, LANES), lambda p, b: (p, 0, 0, 0))],
        out_specs=pl.BlockSpec((t, LANES), lambda p, b: (b, p)),
        out_shape=jax.ShapeDtypeStruct((m, C_HEADS * HEAD_DIM), bf16),
        scratch_shapes=[pltpu.VMEM((2, t, t), f32)],
        compiler_params=_cparams(("arbitrary", "arbitrary")),
        name="nb_attn",
    )(q, k, v, ck, cv, tl, tr)


def _ret_kernel(q_ref, k_ref, v_ref, g_ref, s0f_ref, s0b_ref, qdf_ref, qdb_ref, kdf_ref, kdb_ref, dm_ref,
                cdf_ref, cdb_ref, bd_ref, o_ref, sf_ref, sb_ref, kvf, kvb, *, t):
    c = RET_CHUNK
    n = t // c
    lo = _lane_lo()
    bd = bd_ref[...]
    for pi in range(B_HEADS // 2):
        cols = slice(pi * LANES, (pi + 1) * LANES)
        qdf, qdb, kdf, kdb = qdf_ref[pi], qdb_ref[pi], kdf_ref[pi], kdb_ref[pi]
        cdf, cdb = cdf_ref[pi], cdb_ref[pi]
        for ci in range(n):
            rs = slice(ci * c, (ci + 1) * c)
            kc = k_ref[rs, cols] * (HEAD_DIM ** -0.5)
            vc = v_ref[rs, cols].astype(bf16)
            kvf[ci] = _dot_tn((kc * kdf).astype(bf16), vc) * bd
            kvb[ci] = _dot_tn((kc * kdb).astype(bf16), vc) * bd
        s = s0f_ref[0, pi]
        for ci in range(n):
            upd = kvf[ci]
            kvf[ci] = s
            s = s * cdf + upd
        sf_ref[0, pi] = s
        s = s0b_ref[0, pi]
        for ci in range(n - 1, -1, -1):
            upd = kvb[ci]
            kvb[ci] = s
            s = s * cdb + upd
        sb_ref[0, pi] = s
        for ci in range(n):
            rs = slice(ci * c, (ci + 1) * c)
            qc = q_ref[rs, cols]
            kc = k_ref[rs, cols] * (HEAD_DIM ** -0.5)
            vc = v_ref[rs, cols].astype(bf16)
            qb = qc.astype(bf16)
            a0 = _dot_nt(qb, jnp.where(lo, kc, 0.0).astype(bf16)) * dm_ref[2 * pi]
            a1 = _dot_nt(qb, jnp.where(lo, 0.0, kc).astype(bf16)) * dm_ref[2 * pi + 1]
            o = jnp.where(lo, _dot(a0.astype(bf16), vc), _dot(a1.astype(bf16), vc))
            o = o + _dot((qc * qdf).astype(bf16), kvf[ci].astype(bf16))
            o = o + _dot((qc * qdb).astype(bf16), kvb[ci].astype(bf16))
            inv_n = 1.0 / HEAD_DIM
            m0 = jnp.where(lo, o, 0.0).sum(axis=-1, keepdims=True) * inv_n
            m1 = jnp.where(lo, 0.0, o).sum(axis=-1, keepdims=True) * inv_n
            d = o - jnp.where(lo, m0, m1)
            d2 = d * d
            v0 = jnp.where(lo, d2, 0.0).sum(axis=-1, keepdims=True) * inv_n
            v1 = jnp.where(lo, 0.0, d2).sum(axis=-1, keepdims=True) * inv_n
            y = d * lax.rsqrt(jnp.where(lo, v0, v1) + EPS)
            gt = g_ref[rs, cols]
            o_ref[rs, cols] = (gt * jax.nn.sigmoid(gt) * y).astype(o_ref.dtype)


def _ret_tables(lg_f, lg_b):
    c = RET_CHUNK
    lf = jax.nn.log_sigmoid(lg_f.astype(f32))
    lb = jax.nn.log_sigmoid(lg_b.astype(f32))
    idx = jnp.arange(c, dtype=f32)

    def lanes(per_head):
        r = per_head.shape[1]
        x = jnp.repeat(per_head[:, :, None], HEAD_DIM, axis=2)
        x = x.reshape(B_HEADS // 2, 2, r, HEAD_DIM).transpose(0, 2, 1, 3)
        return x.reshape(B_HEADS // 2, r, LANES)

    qdf = lanes(jnp.exp(lf[:, None] * (idx + 1.0)))
    kdf = lanes(jnp.exp(lf[:, None] * (c - 1.0 - idx)))
    qdb = lanes(jnp.exp(lb[:, None] * (c - idx)))
    kdb = lanes(jnp.exp(lb[:, None] * idx))
    diff = idx[:, None] - idx[None, :]
    low = jnp.where(diff >= 0, jnp.exp(lf[:, None, None] * jnp.maximum(diff, 0.0)), 0.0)
    upp = jnp.where(diff <= 0, jnp.exp(lb[:, None, None] * jnp.maximum(-diff, 0.0)), 0.0)
    dm = low + upp
    cdf = lanes(jnp.exp(lf * c)[:, None])
    cdb = lanes(jnp.exp(lb * c)[:, None])
    r = jnp.arange(LANES)
    bd = ((r[:, None] < HEAD_DIM) == (r[None, :] < HEAD_DIM)).astype(f32)
    return qdf, qdb, kdf, kdb, dm, cdf, cdb, bd


def _blockdiag_states(s):
    b = s.shape[0]
    s = s.astype(f32).reshape(b, B_HEADS // 2, 2, HEAD_DIM, HEAD_DIM)
    z = jnp.zeros_like(s[:, :, 0])
    top = jnp.concatenate([s[:, :, 0], z], axis=-1)
    bot = jnp.concatenate([z, s[:, :, 1]], axis=-1)
    return jnp.concatenate([top, bot], axis=-2)


def _diag_states(s):
    b = s.shape[0]
    h0 = s[:, :, :HEAD_DIM, :HEAD_DIM]
    h1 = s[:, :, HEAD_DIM:, HEAD_DIM:]
    return jnp.stack([h0, h1], axis=2).reshape(b, B_HEADS, HEAD_DIM, HEAD_DIM)


def _retention(q, k, v, g, s0f, s0b, tables, batch):
    m = q.shape[0]
    t = m // batch
    w = B_HEADS * HEAD_DIM
    np_ = B_HEADS // 2
    c = RET_CHUNK
    kern = functools.partial(_ret_kernel, t=t)
    tok = pl.BlockSpec((t, w), lambda b: (b, 0))
    st = pl.BlockSpec((1, np_, LANES, LANES), lambda b: (b, 0, 0, 0))

    def full(a):
        nd = a.ndim
        return pl.BlockSpec(a.shape, lambda b: (0,) * nd)

    o, sf, sb = pl.pallas_call(
        kern,
        grid=(batch,),
        in_specs=[tok, tok, tok, tok, st, st] + [full(a) for a in tables],
        out_specs=[tok, st, st],
        out_shape=[jax.ShapeDtypeStruct((m, w), bf16),
                   jax.ShapeDtypeStruct((batch, np_, LANES, LANES), f32),
                   jax.ShapeDtypeStruct((batch, np_, LANES, LANES), f32)],
        scratch_shapes=[pltpu.VMEM((t // c, LANES, LANES), f32), pltpu.VMEM((t // c, LANES, LANES), f32)],
        compiler_params=_cparams(("arbitrary",)),
        name="retention",
    )(q, k, v, g, _blockdiag_states(s0f), _blockdiag_states(s0b), *tables)
    return o, _diag_states(sf), _diag_states(sb)


def _route(h, wr):
    lane = lax.broadcasted_iota(jnp.int32, (h.shape[0], LANES), 1).astype(f32)
    logits = jnp.full((h.shape[0], LANES), -jnp.inf, f32)
    for e in range(N_EXPERTS):
        logits = jnp.where(lane == e, (h * wr[e:e + 1, :]).sum(axis=-1, keepdims=True), logits)
    m1 = logits.max(axis=-1, keepdims=True)
    i1 = jnp.where(logits == m1, lane, float(LANES)).min(axis=-1, keepdims=True)
    rest = jnp.where(lane == i1, -jnp.inf, logits)
    m2 = rest.max(axis=-1, keepdims=True)
    i2 = jnp.where(rest == m2, lane, float(LANES)).min(axis=-1, keepdims=True)
    e2 = jnp.exp(m2 - m1)
    g1 = 1.0 / (1.0 + e2)
    g2 = e2 * g1
    return jnp.where(lane == 0.0, i1, jnp.where(lane == 1.0, i2, jnp.where(lane == 2.0, g1, jnp.where(lane == 3.0, g2, 0.0))))


def _outproj_kernel(a_ref, b_ref, w_ref, x_ref, gate_ref, g2_ref, sh_ref, sc_ref, *rest, has_router):
    if has_router:
        wr_ref, xo_ref, h_ref, info_ref = rest
    else:
        xo_ref, h_ref = rest
    half = a_ref.shape[1]
    acc = _dot(a_ref[...], w_ref[:half, :]) + _dot(b_ref[...], w_ref[half:, :])
    xn = x_ref[...] + gate_ref[0] * acc
    xo_ref[...] = xn
    h = _norm_mod(xn, g2_ref[...], sh_ref[0], sc_ref[0])
    h_ref[...] = h.astype(h_ref.dtype)
    if has_router:
        info_ref[...] = _route(h, wr_ref[...])


def _outproj(a, a_blk, b, b_blk, w, x, g2, mod3, layer, rows_per_batch, is_sample, w_router=None, tm=1024):
    m = x.shape[0]
    tm = min(tm, m)
    half = w.shape[0] // 2
    has_router = w_router is not None
    kern = functools.partial(_outproj_kernel, has_router=has_router)
    row = pl.BlockSpec((tm, D), lambda i: (i, 0))
    in_specs = [pl.BlockSpec((tm, half), lambda i: (i, a_blk)),
                pl.BlockSpec((tm, half), lambda i: (i, b_blk)),
                pl.BlockSpec(w.shape, lambda i: (0, 0)),
                row,
                _mod_spec(layer, 2, tm, rows_per_batch, is_sample),
                pl.BlockSpec((1, D), lambda i: (0, 0)),
                _mod_spec(layer, 3, tm, rows_per_batch, is_sample),
                _mod_spec(layer, 4, tm, rows_per_batch, is_sample)]
    args = [a, b, w, x, mod3, g2, mod3, mod3]
    out_specs = [row, row]
    out_shape = [jax.ShapeDtypeStruct((m, D), f32), jax.ShapeDtypeStruct((m, D), f32 if has_router else bf16)]
    if has_router:
        in_specs.append(pl.BlockSpec((N_EXPERTS, D), lambda i: (0, 0)))
        args.append(w_router)
        out_specs.append(pl.BlockSpec((tm, LANES), lambda i: (i, 0)))
        out_shape.append(jax.ShapeDtypeStruct((m, LANES), f32))
    return pl.pallas_call(
        kern, grid=(m // tm,), in_specs=in_specs, out_specs=out_specs, out_shape=out_shape,
        compiler_params=_cparams(("arbitrary",)), name="out_proj",
    )(*args)


def _rms(x, g):
    return x * lax.rsqrt(jnp.mean(x * x, axis=-1, keepdims=True) + EPS) * g


FFN_TM = 512
FFN_TF = 256


def _swiglu(h, wgu_ref, wd_ref, act):
    f = wd_ref.shape[0]
    for c0 in range(0, f, FFN_TF):
        g = _dot(h, wgu_ref[:, c0:c0 + FFN_TF])
        u = _dot(h, wgu_ref[:, f + c0:f + c0 + FFN_TF])
        act[:, c0:c0 + FFN_TF] = (g * jax.nn.sigmoid(g) * u).astype(bf16)
    return _dot(act[...], wd_ref[...])


def _ffn_kernel(*refs, final_norm):
    h_ref, wgu_ref, wd_ref, x_ref, gate_ref = refs[:5]
    rest = list(refs[5:])
    fg_ref = rest.pop(0) if final_norm else None
    o_ref, act = rest
    out = x_ref[...] + gate_ref[0] * _swiglu(h_ref[...], wgu_ref, wd_ref, act)
    o_ref[...] = _rms(out, fg_ref[...]) if final_norm else out


def _ffn(h, w_gu, w_down, x, mod3, layer, rows_per_batch, is_sample, final_g=None, tm=FFN_TM):
    m = x.shape[0]
    tm = min(tm, m)
    f = w_down.shape[0]
    final_norm = final_g is not None
    kern = functools.partial(_ffn_kernel, final_norm=final_norm)
    row = pl.BlockSpec((tm, D), lambda i: (i, 0))
    in_specs = [row,
                pl.BlockSpec((D, 2 * f), lambda i: (0, 0)),
                pl.BlockSpec((f, D), lambda i: (0, 0)),
                row,
                _mod_spec(layer, 5, tm, rows_per_batch, is_sample)]
    args = [h, w_gu, w_down, x, mod3]
    if final_norm:
        in_specs.append(pl.BlockSpec((1, D), lambda i: (0, 0)))
        args.append(final_g)
    return pl.pallas_call(
        kern, grid=(m // tm,), in_specs=in_specs, out_specs=row,
        out_shape=jax.ShapeDtypeStruct((m, D), f32),
        scratch_shapes=[pltpu.VMEM((tm, f), bf16)],
        compiler_params=_cparams(("arbitrary",)), name="ffn",
    )(*args)


MOE_TM = FFN_TM
DMA_UNROLL = 8


def _moe_pos_kernel(info_ref, pos_ref, tot_ref, carry, *, tb):
    ph = pl.program_id(0)
    i = pl.program_id(1)
    lane = lax.broadcasted_iota(jnp.int32, (1, LANES), 1).astype(f32)
    info = info_ref[...]
    oh1 = (lane == info[:, 0:1]).astype(f32)
    oh2 = (lane == info[:, 1:2]).astype(f32)
    cnt = oh1 + oh2

    @pl.when((ph == 0) & (i == 0))
    def _():
        carry[...] = jnp.zeros_like(carry)

    @pl.when(ph == 0)
    def _():
        carry[0:1, :] += cnt.sum(axis=0, keepdims=True)

    @pl.when((ph == 1) & (i == 0))
    def _():
        tot = carry[0:1, :]
        tot_ref[...] = jnp.broadcast_to(tot, tot_ref.shape)
        padded = jnp.floor((tot + (MOE_TM - 1)) * (1.0 / MOE_TM)) * MOE_TM
        start = jnp.zeros_like(tot)
        for e in range(N_EXPERTS - 1):
            pe = jnp.where(lane == e, padded, 0.0).sum(axis=-1, keepdims=True)
            start = start + jnp.where(lane > e, pe, 0.0)
        carry[1:2, :] = start

    @pl.when(ph == 1)
    def _():
        r = lax.broadcasted_iota(jnp.int32, (tb, tb), 0)
        c = lax.broadcasted_iota(jnp.int32, (tb, tb), 1)
        tri = jnp.where(r > c, 1.0, 0.0).astype(bf16)
        base = carry[1:2, :] + _dot(tri, cnt.astype(bf16))
        p1 = (oh1 * base).sum(axis=-1, keepdims=True)
        p2 = (oh2 * base).sum(axis=-1, keepdims=True)
        pos_ref[...] = jnp.where(lane == 0.0, p1, jnp.where(lane == 1.0, p2, 0.0)).astype(jnp.int32)
        carry[1:2, :] += cnt.sum(axis=0, keepdims=True)


def _moe_positions(info, tb=1024):
    m = info.shape[0]
    tb = min(tb, m)
    nblk = m // tb
    pos, tot = pl.pallas_call(
        functools.partial(_moe_pos_kernel, tb=tb),
        grid=(2, nblk),
        in_specs=[pl.BlockSpec((tb, LANES), lambda p, i: (i, 0))],
        out_specs=[pl.BlockSpec((tb, LANES), lambda p, i: (i * p, 0)),
                   pl.BlockSpec((8, LANES), lambda p, i: (0, 0))],
        out_shape=[jax.ShapeDtypeStruct((m, LANES), jnp.int32), jax.ShapeDtypeStruct((8, LANES), f32)],
        scratch_shapes=[pltpu.VMEM((8, LANES), f32)],
        compiler_params=_cparams(("arbitrary", "arbitrary")), name="moe_pos",
    )(info)
    return pos[:, :2].reshape(-1), tot[0, :N_EXPERTS].astype(jnp.int32)


def _row_copy(src, s_row, dst, d_row, sem):
    return pltpu.make_async_copy(src.at[pl.ds(s_row, 1)], dst.at[pl.ds(d_row, 1)], sem)


def _moe_dispatch_kernel(pos_ref, zrow_ref, zflag_ref, hp_ref, hs_ref, o_ref, zeros, sem, zsem, *, nbp, tb):
    i = pl.program_id(0)

    @pl.when(i == 0)
    def _():
        zeros[...] = jnp.zeros_like(zeros)

        def zero_tile(z):
            dst = o_ref.at[pl.ds(pl.multiple_of(zrow_ref[z], MOE_TM), MOE_TM)]
            return pltpu.make_async_copy(zeros, dst, zsem)

        for z in range(zrow_ref.shape[0]):
            @pl.when(zflag_ref[z] != 0)
            def _():
                zero_tile(z).start()
        for z in range(zrow_ref.shape[0]):
            @pl.when(zflag_ref[z] != 0)
            def _():
                zero_tile(z).wait()

    def scatter(src):
        def issue(r, _):
            g = 2 * (i * tb + r)
            _row_copy(src, r, o_ref, pos_ref[g], sem).start()
            _row_copy(src, r, o_ref, pos_ref[g + 1], sem).start(priority=1)
            return 0
        lax.fori_loop(0, tb, issue, 0, unroll=DMA_UNROLL)

        def drain(r, _):
            _row_copy(src, 0, o_ref, 0, sem).wait()
            _row_copy(src, 0, o_ref, 0, sem).wait()
            return 0
        lax.fori_loop(0, tb, drain, 0, unroll=DMA_UNROLL)

    @pl.when(i < nbp)
    def _():
        scatter(hp_ref)

    @pl.when(i >= nbp)
    def _():
        scatter(hs_ref)


def _moe_dispatch(pos, zrow, zflag, hp, hs, n_rows, tb=512):
    mp, ms = hp.shape[0], hs.shape[0]
    tb = min(tb, mp, ms)
    nbp, nbs = mp // tb, ms // tb
    kern = functools.partial(_moe_dispatch_kernel, nbp=nbp, tb=tb)
    return pl.pallas_call(
        kern,
        grid_spec=pltpu.PrefetchScalarGridSpec(
            num_scalar_prefetch=3, grid=(nbp + nbs,),
            in_specs=[pl.BlockSpec((tb, D), lambda i, *_: (jnp.minimum(i, nbp - 1), 0)),
                      pl.BlockSpec((tb, D), lambda i, *_: (jnp.maximum(i - nbp, 0), 0))],
            out_specs=pl.BlockSpec(memory_space=pl.ANY),
            scratch_shapes=[pltpu.VMEM((MOE_TM, D), f32), pltpu.SemaphoreType.DMA(()),
                            pltpu.SemaphoreType.DMA(())]),
        out_shape=jax.ShapeDtypeStruct((n_rows, D), f32),
        compiler_params=_cparams(("arbitrary",)), name="moe_dispatch",
    )(pos, zrow, zflag, hp, hs)


def _moe_ffn_kernel(te_ref, na_ref, h_ref, wgu_ref, wd_ref, o_ref, act):
    i = pl.program_id(0)

    @pl.when(i < na_ref[0])
    def _():
        o_ref[...] = _swiglu(h_ref[...].astype(bf16), wgu_ref.at[0], wd_ref.at[0], act)

    @pl.when(i >= na_ref[0])
    def _():
        o_ref[...] = jnp.zeros_like(o_ref)


def _moe_ffn(h_sorted, w_gu, w_down, tile_expert, n_active):
    n_rows = h_sorted.shape[0]
    tm = MOE_TM
    f = w_down.shape[1]
    n_tiles = n_rows // tm
    return pl.pallas_call(
        _moe_ffn_kernel,
        grid_spec=pltpu.PrefetchScalarGridSpec(
            num_scalar_prefetch=2, grid=(n_tiles,),
            in_specs=[pl.BlockSpec((tm, D), lambda i, te, na: (jnp.minimum(i, na[0] - 1), 0)),
                      pl.BlockSpec((1, D, 2 * f), lambda i, te, na: (te[i], 0, 0)),
                      pl.BlockSpec((1, f, D), lambda i, te, na: (te[i], 0, 0))],
            out_specs=pl.BlockSpec((tm, D), lambda i, te, na: (i, 0)),
            scratch_shapes=[pltpu.VMEM((tm, f), bf16)]),
        out_shape=jax.ShapeDtypeStruct((n_rows, D), f32),
        compiler_params=_cparams(("arbitrary",)), name="moe_ffn",
    )(tile_expert, n_active, h_sorted, w_gu, w_down)


def _moe_combine_kernel(pos_ref, info_ref, x_ref, gate_ref, fg_ref, y_ref, o_ref, ybuf, sem, *, tb, base,
                        final_norm):
    i = pl.program_id(0)
    n = pl.num_programs(0)

    def issue(blk, slot):
        def body(r, _):
            g = 2 * (base + blk * tb + r)
            _row_copy(y_ref, pos_ref[g], ybuf.at[slot, 0], r, sem.at[slot]).start()
            _row_copy(y_ref, pos_ref[g + 1], ybuf.at[slot, 1], r, sem.at[slot]).start(priority=1)
            return 0
        lax.fori_loop(0, tb, body, 0, unroll=DMA_UNROLL)

    @pl.when(i == 0)
    def _():
        issue(0, 0)

    @pl.when(i + 1 < n)
    def _():
        issue(i + 1, (i + 1) % 2)

    slot = i % 2

    def drain(r, _):
        _row_copy(y_ref, 0, ybuf.at[slot, 0], 0, sem.at[slot]).wait()
        _row_copy(y_ref, 0, ybuf.at[slot, 1], 0, sem.at[slot]).wait()
        return 0
    lax.fori_loop(0, tb, drain, 0, unroll=DMA_UNROLL)

    info = info_ref[...]
    y = info[:, 2:3] * ybuf[slot, 0] + info[:, 3:4] * ybuf[slot, 1]
    out = x_ref[...] + gate_ref[0] * y
    o_ref[...] = _rms(out, fg_ref[...]) if final_norm else out


def _moe_combine(pos, info, x, mod3, layer, y_sorted, base, rows_per_batch, is_sample, final_g, tb=256):
    m = x.shape[0]
    tb = min(tb, m)
    final_norm = final_g is not None
    if final_g is None:
        final_g = jnp.ones((1, D), f32)
    kern = functools.partial(_moe_combine_kernel, tb=tb, base=base, final_norm=final_norm)
    return pl.pallas_call(
        kern,
        grid_spec=pltpu.PrefetchScalarGridSpec(
            num_scalar_prefetch=1, grid=(m // tb,),
            in_specs=[pl.BlockSpec((tb, LANES), lambda i, p: (i, 0)),
                      pl.BlockSpec((tb, D), lambda i, p: (i, 0)),
                      _mod_spec(layer, 5, tb, rows_per_batch, is_sample),
                      pl.BlockSpec((1, D), lambda i, p: (0, 0)),
                      pl.BlockSpec(memory_space=pl.ANY)],
            out_specs=pl.BlockSpec((tb, D), lambda i, p: (i, 0)),
            scratch_shapes=[pltpu.VMEM((2, 2, tb, D), f32), pltpu.SemaphoreType.DMA((2,))]),
        out_shape=jax.ShapeDtypeStruct((m, D), f32),
        compiler_params=_cparams(("arbitrary",)), name="moe_combine",
    )(pos, info, x, mod3, final_g, y_sorted)


def _moe(hp, hs, info_p, info_s, xp, xs, w_gu, w_down, mod3, layer, tp, ts, final_g):
    mp, ms = hp.shape[0], hs.shape[0]
    tm = MOE_TM
    n_tiles = (2 * (mp + ms)) // tm + N_EXPERTS
    pos, counts = _moe_positions(jnp.concatenate([info_p, info_s], axis=0))
    tiles_per = (counts + tm - 1) // tm
    cum = jnp.cumsum(tiles_per)
    n_active = cum[-1:].astype(jnp.int32)
    ids = jnp.arange(n_tiles, dtype=jnp.int32)
    te = jnp.minimum((ids[:, None] >= cum[None, :]).astype(jnp.int32).sum(axis=1), N_EXPERTS - 1)
    last_e = te[jnp.maximum(n_active[0] - 1, 0)]
    te = jnp.where(ids < n_active[0], te, last_e)
    tail = jnp.arange(n_tiles - N_EXPERTS, n_tiles, dtype=jnp.int32)
    zrow = jnp.concatenate([jnp.maximum(cum - 1, 0), tail]).astype(jnp.int32) * tm
    zflag = jnp.concatenate([counts % tm != 0, tail >= n_active[0]]).astype(jnp.int32)
    h_sorted = _moe_dispatch(pos, zrow, zflag, hp, hs, n_tiles * tm)
    y_sorted = _moe_ffn(h_sorted, w_gu, w_down, te, n_active)
    yp = _moe_combine(pos, info_p, xp, mod3, layer, y_sorted, 0, tp, False, final_g)
    ys = _moe_combine(pos, info_s, xs, mod3, layer, y_sorted, mp, ts, True, final_g)
    return yp, ys


def kernel(x_prompt, x_sample, cache_a_k, cache_a_v, state_ret_fwd, state_ret_bwd, cache_c_k, cache_c_v, c, c_ctx,
           mod_w, mod_b, norm1_g, norm2_g, final_g, ev_w_in, ev_w_out, ev_sink, ev_ret_logit_fwd, ev_ret_logit_bwd,
           ev_ffn_w_gu, ev_ffn_w_down, od_w_qkv, od_w_out, od_rpb, od_w_router, od_moe_w_gu, od_moe_w_down):
    bp, tp, _ = x_prompt.shape
    bs, ts, _ = x_sample.shape
    depth = mod_w.shape[0]
    assert 1 + bs <= MOD_ROWS
    xp = x_prompt.reshape(bp * tp, D)
    xs = x_sample.reshape(bs * ts, D)

    cvecs = jnp.zeros((MOD_ROWS, D), f32).at[0].set(c_ctx).at[1:1 + bs].set(c)
    mod3 = _adaln(cvecs, mod_w, mod_b)

    a_q = A_HEADS * HEAD_DIM
    a_kv = A_KV_HEADS * HEAD_DIM
    b_w = B_HEADS * HEAD_DIM
    c_w = C_HEADS * HEAD_DIM
    streams = ((False, tp), (True, ts))
    outs = {}
    for i in range(depth):
        j = i // 2
        g1 = norm1_g[i].reshape(1, D)
        g2 = norm2_g[i].reshape(1, D)
        last = i == depth - 1
        fg = final_g.reshape(1, D) if last else None
        if i % 2 == 0:
            w_in = ev_w_in[j].astype(bf16)
            w_out = ev_w_out[j].astype(bf16)
            splits = (a_q, a_kv, a_kv, b_w, b_w, b_w, b_w)
            dts_p = (bf16, f32, f32, f32, f32, bf16, f32)
            dts_s = (f32, f32, f32, f32, f32, bf16, f32)
            tables = _ret_tables(ev_ret_logit_fwd[j], ev_ret_logit_bwd[j])
            qa_p, ka_p, va_p, qr_p, kr_p, vr_p, gr_p = _proj(xp, g1, mod3, i, w_in, splits, dts_p, tp, False)
            qa_s, ka_s, va_s, qr_s, kr_s, vr_s, gr_s = _proj(xs, g1, mod3, i, w_in, splits, dts_s, ts, True)
            oa_p = _ctx_attn(qa_p, ka_p, va_p, ev_sink[j], bp, A_HEADS, A_KV_HEADS)
            zero = jnp.zeros((bp, B_HEADS, HEAD_DIM, HEAD_DIM), f32)
            or_p, sf, sb = _retention(qr_p, kr_p, vr_p, gr_p, zero, zero, tables, bp)
            lc = cache_a_k.shape[2]
            oa_s = _win_attn(qa_s, ka_s, va_s, cache_a_k[:, j].reshape(bs, lc, a_kv),
                             cache_a_v[:, j].reshape(bs, lc, a_kv), ev_sink[j], bs)
            or_s, _, _ = _retention(qr_s, kr_s, vr_s, gr_s, state_ret_fwd[:, j], state_ret_bwd[:, j], tables, bs)
            outs.setdefault('a_k', []).append(ka_p.reshape(bp, tp, A_KV_HEADS, HEAD_DIM))
            outs.setdefault('a_v', []).append(va_p.reshape(bp, tp, A_KV_HEADS, HEAD_DIM))
            outs.setdefault('r_f', []).append(sf)
            outs.setdefault('r_b', []).append(sb)
            xp, hp = _outproj(oa_p, 0, or_p, 0, w_out, xp, g2, mod3, i, tp, False)
            xs, hs = _outproj(oa_s, 0, or_s, 0, w_out, xs, g2, mod3, i, ts, True)
            w_gu = ev_ffn_w_gu[j].astype(bf16)
            w_dn = ev_ffn_w_down[j].astype(bf16)
            xp = _ffn(hp, w_gu, w_dn, xp, mod3, i, tp, False, final_g=fg)
            xs = _ffn(hs, w_gu, w_dn, xs, mod3, i, ts, True, final_g=fg)
        else:
            w_qkv = od_w_qkv[j].astype(bf16)
            w_out = od_w_out[j].astype(bf16)
            w_router = od_w_router[j].astype(f32).T
            splits = (c_w, c_w, c_w)
            q_p, k_p, v_p = _proj(xp, g1, mod3, i, w_qkv, splits, (bf16, f32, f32), tp, False)
            q_s, k_s, v_s = _proj(xs, g1, mod3, i, w_qkv, splits, (bf16, bf16, bf16), ts, True)
            o_p = _ctx_attn(q_p, k_p, v_p, None, bp, C_HEADS, C_HEADS)
            lc = cache_c_k.shape[2]
            o_s = _nb_attn(q_s, k_s, v_s, cache_c_k[:, j].reshape(bs, lc, c_w),
                           cache_c_v[:, j].reshape(bs, lc, c_w), od_rpb[j], bs)
            outs.setdefault('c_k', []).append(k_p.reshape(bp, tp, C_HEADS, HEAD_DIM))
            outs.setdefault('c_v', []).append(v_p.reshape(bp, tp, C_HEADS, HEAD_DIM))
            xp, hp, info_p = _outproj(o_p, 0, o_p, 1, w_out, xp, g2, mod3, i, tp, False, w_router=w_router)
            xs, hs, info_s = _outproj(o_s, 0, o_s, 1, w_out, xs, g2, mod3, i, ts, True, w_router=w_router)
            xp, xs = _moe(hp, hs, info_p, info_s, xp, xs, od_moe_w_gu[j].astype(bf16),
                          od_moe_w_down[j].astype(bf16), mod3, i, tp, ts, fg)
    y_prompt = xp.reshape(bp, tp, D)
    y_sample = xs.reshape(bs, ts, D)
    return (y_prompt, y_sample,
            jnp.stack(outs['a_k'], axis=1), jnp.stack(outs['a_v'], axis=1),
            jnp.stack(outs['r_f'], axis=1), jnp.stack(outs['r_b'], axis=1),
            jnp.stack(outs['c_k'], axis=1), jnp.stack(outs['c_v'], axis=1))
```

```python
import functools

import numpy as np
import jax
import jax.numpy as jnp
from jax import lax
from jax.experimental import pallas as pl
from jax.experimental.pallas import tpu as pltpu

f32 = jnp.float32
bf16 = jnp.bfloat16

D = 1024
HEAD_DIM = 64
GRID_W = 64
ROPE_BASE = 10000.0
A_HEADS = 8
A_KV_HEADS = 2
WINDOW = 128
B_HEADS = 8
RET_CHUNK = 128
C_HEADS = 16
NB_ROWS = 8
NB_COLS = 16
N_EXPERTS = 8
EPS = 1e-6
NEG = -1e30

LANES = 128
MOD_ROWS = 16
VMEM_LIMIT_MB = 56


def _cparams(sem, vmem_mb=VMEM_LIMIT_MB):
    return pltpu.CompilerParams(dimension_semantics=sem, vmem_limit_bytes=vmem_mb * 1024 * 1024)


def _dot(a, b):
    return jnp.dot(a, b, preferred_element_type=f32)


def _dot_nt(a, b):
    return lax.dot_general(a, b, (((1,), (1,)), ((), ())), preferred_element_type=f32)


def _dot_tn(a, b):
    return lax.dot_general(a, b, (((0,), (0,)), ((), ())), preferred_element_type=f32)


def _lane_lo():
    return lax.broadcasted_iota(jnp.int32, (1, LANES), 1) < HEAD_DIM


def _swap_halves(x):
    return pltpu.roll(x, HEAD_DIM, 1)


def _mod_imap(layer, which, tm, rows_per_batch, is_sample):
    def imap(i, *_):
        r = (1 + (i * tm) // rows_per_batch) if is_sample else 0
        return ((layer * MOD_ROWS + r) * 6 + which, 0, 0)

    return imap


def _mod_spec(layer, which, tm, rows_per_batch, is_sample):
    return pl.BlockSpec((1, 1, D), _mod_imap(layer, which, tm, rows_per_batch, is_sample))


def _adaln_kernel(cv_ref, w_ref, b_ref, o_ref):
    cv = cv_ref[...]
    s = (cv * jax.nn.sigmoid(cv)).astype(bf16)
    o_ref[0] = _dot(s, w_ref[0].astype(bf16)) + b_ref[0]


def _adaln(cvecs, mod_w, mod_b):
    depth = mod_w.shape[0]
    tn = 1024
    out = pl.pallas_call(
        _adaln_kernel,
        grid=(depth, 6 * D // tn),
        in_specs=[pl.BlockSpec((MOD_ROWS, D), lambda l, j: (0, 0)),
                  pl.BlockSpec((1, D, tn), lambda l, j: (l, 0, j)),
                  pl.BlockSpec((1, 1, tn), lambda l, j: (l, 0, j))],
        out_specs=pl.BlockSpec((1, MOD_ROWS, tn), lambda l, j: (l, 0, j)),
        out_shape=jax.ShapeDtypeStruct((depth, MOD_ROWS, 6 * D), f32),
        compiler_params=_cparams(("arbitrary", "arbitrary")),
        name="adaln",
    )(cvecs, mod_w, mod_b.reshape(depth, 1, 6 * D))
    return out.reshape(depth * MOD_ROWS * 6, 1, D)


def _norm_mod(x, g, shift, scale):
    y = x * lax.rsqrt(jnp.mean(x * x, axis=-1, keepdims=True) + EPS) * g
    return y * (1.0 + scale) + shift


def _cast_weights_once(w_ref, wb):
    @pl.when(pl.program_id(0) == 0)
    def _():
        wb[...] = w_ref[...].astype(bf16)


def _proj_kernel(x_ref, g_ref, sh_ref, sc_ref, w_ref, *rest, splits, chunk):
    out_refs, wb = rest[:-1], rest[-1]
    _cast_weights_once(w_ref, wb)
    h = _norm_mod(x_ref[...], g_ref[...], sh_ref[0], sc_ref[0]).astype(bf16)
    off = 0
    for o_ref, width in zip(out_refs, splits):
        for c0 in range(0, width, chunk):
            cw = min(chunk, width - c0)
            o_ref[:, c0:c0 + cw] = _dot(h, wb[:, off + c0:off + c0 + cw]).astype(o_ref.dtype)
        off += width


def _proj(x, g, mod3, layer, w, splits, dtypes, rows_per_batch, is_sample, tm=512):
    m = x.shape[0]
    tm = min(tm, m)
    n = w.shape[1]
    kern = functools.partial(_proj_kernel, splits=tuple(splits), chunk=512)
    return pl.pallas_call(
        kern,
        grid=(m // tm,),
        in_specs=[pl.BlockSpec((tm, D), lambda i: (i, 0)),
                  pl.BlockSpec((1, D), lambda i: (0, 0)),
                  _mod_spec(layer, 0, tm, rows_per_batch, is_sample),
                  _mod_spec(layer, 1, tm, rows_per_batch, is_sample),
                  pl.BlockSpec((D, n), lambda i: (0, 0))],
        out_specs=[pl.BlockSpec((tm, s), lambda i: (i, 0)) for s in splits],
        out_shape=[jax.ShapeDtypeStruct((m, s), dt) for s, dt in zip(splits, dtypes)],
        scratch_shapes=[pltpu.VMEM((D, n), bf16)],
        compiler_params=_cparams(("arbitrary",)),
        name="norm_proj",
    )(x, g, mod3, mod3, w)


def _softmax_parts(parts, sink):
    m = parts[0].max(axis=-1, keepdims=True)
    for s in parts[1:]:
        m = jnp.maximum(m, s.max(axis=-1, keepdims=True))
    if sink is not None:
        m = jnp.maximum(m, sink)
    es = [jnp.exp(s - m) for s in parts]
    den = es[0].sum(axis=-1, keepdims=True)
    for e in es[1:]:
        den = den + e.sum(axis=-1, keepdims=True)
    if sink is not None:
        den = den + jnp.exp(sink - m)
    return [e.astype(bf16) for e in es], 1.0 / den


def _ctx_attn_kernel(sink_ref, q_ref, k_ref, v_ref, o_ref, *, heads, kv_heads, has_sink):
    rep = heads // kv_heads
    lo = _lane_lo()
    scale = HEAD_DIM ** -0.5
    for pi in range(heads // 2):
        qp = q_ref[:, pi * LANES:(pi + 1) * LANES].astype(f32) * scale
        outs = []
        for hh in range(2):
            h = 2 * pi + hh
            g = h // rep
            kp, kh = g // 2, g % 2
            qv = (qp if kh == hh else _swap_halves(qp)).astype(bf16)
            kv = k_ref[:, kp * LANES:(kp + 1) * LANES]
            km = (jnp.where(lo, kv, 0.0) if kh == 0 else jnp.where(lo, 0.0, kv)).astype(bf16)
            (p,), inv = _softmax_parts([_dot_nt(qv, km)], sink_ref[h] if has_sink else None)
            o = _dot(p, v_ref[:, kp * LANES:(kp + 1) * LANES].astype(bf16)) * inv
            outs.append(o if kh == hh else _swap_halves(o))
        o_ref[:, pi * LANES:(pi + 1) * LANES] = jnp.where(lo, outs[0], outs[1]).astype(o_ref.dtype)


def _ctx_attn(q, k, v, sink, batch, heads, kv_heads):
    m = q.shape[0]
    t = m // batch
    has_sink = sink is not None
    if sink is None:
        sink = jnp.zeros((heads,), f32)
    kern = functools.partial(_ctx_attn_kernel, heads=heads, kv_heads=kv_heads, has_sink=has_sink)
    return pl.pallas_call(
        kern,
        grid=(batch,),
        in_specs=[pl.BlockSpec(memory_space=pltpu.SMEM),
                  pl.BlockSpec((t, heads * HEAD_DIM), lambda b: (b, 0)),
                  pl.BlockSpec((t, kv_heads * HEAD_DIM), lambda b: (b, 0)),
                  pl.BlockSpec((t, kv_heads * HEAD_DIM), lambda b: (b, 0))],
        out_specs=pl.BlockSpec((t, heads * HEAD_DIM), lambda b: (b, 0)),
        out_shape=jax.ShapeDtypeStruct((m, heads * HEAD_DIM), bf16),
        compiler_params=_cparams(("arbitrary",)),
        name="ctx_attn",
    )(sink.astype(f32), q, k, v)


def _rope(x, cos, sin_signed):
    lane = lax.broadcasted_iota(jnp.int32, (1, LANES), 1)
    first = (lane % 32) < 16
    rot = jnp.where(first, pltpu.roll(x, LANES - 16, 1), pltpu.roll(x, 16, 1))
    return x * cos + rot * sin_signed


def _win_attn_kernel(sink_ref, q_ref, k_ref, v_ref, ck_ref, cv_ref, cosq_ref, sinq_ref, cosk_ref, sink_k_ref,
                     o_ref, klm, vl, ckm, cvb, *, qblk, t_lat):
    n = pl.program_id(1)
    lo = _lane_lo()
    span = 3 * qblk

    @pl.when(n == 0)
    def _():
        kr = _rope(k_ref[...], cosk_ref[...], sink_k_ref[...])
        klm[0] = jnp.where(lo, kr, 0.0).astype(bf16)
        klm[1] = jnp.where(lo, 0.0, kr).astype(bf16)
        vl[...] = v_ref[...].astype(bf16)
        c = ck_ref[0]
        ckm[0] = jnp.where(lo, c, 0.0).astype(bf16)
        ckm[1] = jnp.where(lo, 0.0, c).astype(bf16)
        cvb[...] = cv_ref[0].astype(bf16)

    start = pl.multiple_of(jnp.clip(n * qblk - qblk, 0, t_lat - span), qblk)
    rep = A_HEADS // A_KV_HEADS
    qpos = n * qblk + lax.broadcasted_iota(jnp.int32, (rep * qblk, 1), 0) % qblk
    kpos = start + lax.broadcasted_iota(jnp.int32, (1, span), 1)
    valid = jnp.abs(kpos - qpos) <= WINDOW
    scale = HEAD_DIM ** -0.5
    cq = cosq_ref[...]
    sq = sinq_ref[...]
    roped = [_rope(q_ref[:, pi * LANES:(pi + 1) * LANES], cq, sq) * scale for pi in range(A_HEADS // 2)]
    outs = [None] * A_HEADS
    for g in range(A_KV_HEADS):
        group = range(g * rep, (g + 1) * rep)
        qst = jnp.concatenate([roped[h // 2] if h % 2 == g else _swap_halves(roped[h // 2]) for h in group],
                              axis=0).astype(bf16)
        sink = jnp.concatenate([jnp.full((qblk, 1), sink_ref[h], f32) for h in group], axis=0)
        s_c = _dot_nt(qst, ckm[g])
        s_l = jnp.where(valid, _dot_nt(qst, klm[g, pl.ds(start, span), :]), NEG)
        (p_c, p_l), inv = _softmax_parts([s_c, s_l], sink)
        o = (_dot(p_c, cvb[...]) + _dot(p_l, vl[pl.ds(start, span), :])) * inv
        for idx, h in enumerate(group):
            oh = o[idx * qblk:(idx + 1) * qblk]
            outs[h] = oh if h % 2 == g else _swap_halves(oh)
    for pi in range(A_HEADS // 2):
        o_ref[:, pi * LANES:(pi + 1) * LANES] = jnp.where(lo, outs[2 * pi], outs[2 * pi + 1]).astype(o_ref.dtype)


def _rope_tables(t):
    half = HEAD_DIM // 2
    nf = half // 2
    pos = jnp.arange(t)
    row = (pos // GRID_W).astype(f32)
    col = (pos % GRID_W).astype(f32)
    inv = ROPE_BASE ** (-jnp.arange(nf, dtype=f32) / nf)
    ang_r = row[:, None] * inv
    ang_c = col[:, None] * inv
    ang = jnp.concatenate([ang_r, ang_r, ang_c, ang_c], axis=-1)
    sign = jnp.concatenate([-jnp.ones((nf,), f32), jnp.ones((nf,), f32)] * 2)
    cos = jnp.cos(ang)
    sin_signed = jnp.sin(ang) * sign
    return jnp.tile(cos, (1, 2)), jnp.tile(sin_signed, (1, 2))


def _win_attn(q, k, v, ck, cv, sink, batch):
    m = q.shape[0]
    t = m // batch
    qblk = 128
    nb = t // qblk
    lc = ck.shape[1]
    cos, sin_s = _rope_tables(t)
    kern = functools.partial(_win_attn_kernel, qblk=qblk, t_lat=t)
    return pl.pallas_call(
        kern,
        grid=(batch, nb),
        in_specs=[pl.BlockSpec(memory_space=pltpu.SMEM),
                  pl.BlockSpec((qblk, A_HEADS * HEAD_DIM), lambda b, n: (b * nb + n, 0)),
                  pl.BlockSpec((t, LANES), lambda b, n: (b, 0)),
                  pl.BlockSpec((t, LANES), lambda b, n: (b, 0)),
                  pl.BlockSpec((1, lc, LANES), lambda b, n: (b, 0, 0)),
                  pl.BlockSpec((1, lc, LANES), lambda b, n: (b, 0, 0)),
                  pl.BlockSpec((qblk, LANES), lambda b, n: (n, 0)),
                  pl.BlockSpec((qblk, LANES), lambda b, n: (n, 0)),
                  pl.BlockSpec((t, LANES), lambda b, n: (0, 0)),
                  pl.BlockSpec((t, LANES), lambda b, n: (0, 0))],
        out_specs=pl.BlockSpec((qblk, A_HEADS * HEAD_DIM), lambda b, n: (b * nb + n, 0)),
        out_shape=jax.ShapeDtypeStruct((m, A_HEADS * HEAD_DIM), bf16),
        scratch_shapes=[pltpu.VMEM((2, t, LANES), bf16), pltpu.VMEM((t, LANES), bf16),
                        pltpu.VMEM((2, lc, LANES), bf16), pltpu.VMEM((lc, LANES), bf16)],
        compiler_params=_cparams(("arbitrary", "arbitrary")),
        name="win_attn",
    )(sink.astype(f32), q, k, v, ck, cv, cos, sin_s, cos, sin_s)


def _nb_attn_kernel(q_ref, k_ref, v_ref, ck_ref, cv_ref, tl_ref, tr_ref, o_ref, bias, *, t_lat, qblk):
    b = pl.program_id(1)
    lo = _lane_lo()
    rows = t_lat // GRID_W
    wr = min(NB_ROWS, rows)
    neg_slot = 2 * NB_ROWS - 1

    @pl.when(b == 0)
    def _():
        for hh in range(2):
            for rq in range(rows):
                k0 = min(max(rq - wr // 2, 0), rows - wr)
                for p in range(rows // 2):
                    idx = []
                    for rk in (2 * p, 2 * p + 1):
                        idx.append(rk - rq + NB_ROWS - 1 if k0 <= rk < k0 + wr else neg_slot)
                    bias[hh, rq * GRID_W:(rq + 1) * GRID_W, p * LANES:(p + 1) * LANES] = (
                        tl_ref[hh, idx[0]] + tr_ref[hh, idx[1]])

    kb = k_ref[...]
    km = [jnp.where(lo, kb, 0.0).astype(bf16), jnp.where(lo, 0.0, kb).astype(bf16)]
    vb = v_ref[...].astype(bf16)
    cb = ck_ref[0]
    ckm = [jnp.where(lo, cb, 0.0).astype(bf16), jnp.where(lo, 0.0, cb).astype(bf16)]
    cvb = cv_ref[0].astype(bf16)
    scale = HEAD_DIM ** -0.5
    for qb in range(t_lat // qblk):
        rs = slice(qb * qblk, (qb + 1) * qblk)
        r0, r1 = qb * qblk // GRID_W, ((qb + 1) * qblk - 1) // GRID_W
        k_lo = min(max(r0 - wr // 2, 0), rows - wr) * GRID_W
        k_hi = (min(max(r1 - wr // 2, 0), rows - wr) + wr) * GRID_W
        ks = slice(k_lo // LANES * LANES, -(-k_hi // LANES) * LANES)
        qv = (q_ref[rs, :] * scale).astype(bf16)
        outs = []
        for hh in range(2):
            s_c = _dot_nt(qv, ckm[hh])
            s_n = _dot_nt(qv, km[hh][ks]) + bias[hh, rs, ks]
            (p_c, p_n), inv = _softmax_parts([s_c, s_n], None)
            outs.append((_dot(p_c, cvb) + _dot(p_n, vb[ks])) * inv)
        o_ref[rs, :] = jnp.where(lo, outs[0], outs[1]).astype(o_ref.dtype)


def _nb_bias_tiles(rpb):
    heads = rpb.shape[0]
    cq = jnp.arange(GRID_W)
    cstart = jnp.clip(cq - NB_COLS // 2, 0, GRID_W - NB_COLS)
    col_ok = (cq[None, :] >= cstart[:, None]) & (cq[None, :] < cstart[:, None] + NB_COLS)
    dc = jnp.clip(cq[None, :] - cq[:, None], 1 - NB_COLS, NB_COLS - 1) + NB_COLS - 1
    onehot = jax.nn.one_hot(dc.reshape(-1), 2 * NB_COLS - 1, dtype=f32)
    tm = jnp.einsum('hab,kb->hak', rpb.astype(f32), onehot, precision=lax.Precision.HIGHEST)
    tm = tm.reshape(heads, 2 * NB_ROWS - 1, GRID_W, GRID_W)
    tm = jnp.where(col_ok[None, None], tm, NEG)
    tm = jnp.concatenate([tm, jnp.full((heads, 1, GRID_W, GRID_W), NEG, f32)], axis=1)
    z = jnp.zeros_like(tm)
    return jnp.concatenate([tm, z], axis=-1), jnp.concatenate([z, tm], axis=-1)


def _nb_attn(q, k, v, ck, cv, rpb, batch):
    m = q.shape[0]
    t = m // batch
    lc = ck.shape[1]
    npairs = C_HEADS // 2
    tl, tr = _nb_bias_tiles(rpb)
    kern = functools.partial(_nb_attn_kernel, t_lat=t, qblk=256)
    return pl.pallas_call(
        kern,
        grid=(npairs, batch),
        in_specs=[pl.BlockSpec((t, LANES), lambda p, b: (b, p)),
                  pl.BlockSpec((t, LANES), lambda p, b: (b, p)),
                  pl.BlockSpec((t, LANES), lambda p, b: (b, p)),
                  pl.BlockSpec((1, lc, LANES), lambda p, b: (b, 0, p)),
                  pl.BlockSpec((1, lc, LANES), lambda p, b: (b, 0, p)),
                  pl.BlockSpec((2, 2 * NB_ROWS, GRID_W, LANES), lambda p, b: (p, 0, 0, 0)),
                  pl.BlockSpec((2, 2 * NB_ROWS, GRID_W, LANES), lambda p, b: (p, 0, 0, 0))],
        out_specs=pl.BlockSpec((t, LANES), lambda p, b: (b, p)),
        out_shape=jax.ShapeDtypeStruct((m, C_HEADS * HEAD_DIM), bf16),
        scratch_shapes=[pltpu.VMEM((2, t, t), f32)],
        compiler_params=_cparams(("arbitrary", "arbitrary")),
        name="nb_attn",
    )(q, k, v, ck, cv, tl, tr)


def _ret_kernel(q_ref, k_ref, v_ref, g_ref, s0f_ref, s0b_ref, qdf_ref, qdb_ref, kdf_ref, kdb_ref, dm_ref,
                cdf_ref, cdb_ref, bd_ref, o_ref, sf_ref, sb_ref, kvf, kvb, *, t):
    c = RET_CHUNK
    n = t // c
    lo = _lane_lo()
    bd = bd_ref[...]
    for pi in range(B_HEADS // 2):
        cols = slice(pi * LANES, (pi + 1) * LANES)
        qdf, qdb, kdf, kdb = qdf_ref[pi], qdb_ref[pi], kdf_ref[pi], kdb_ref[pi]
        cdf, cdb = cdf_ref[pi], cdb_ref[pi]
        for ci in range(n):
            rs = slice(ci * c, (ci + 1) * c)
            kc = k_ref[rs, cols] * (HEAD_DIM ** -0.5)
            vc = v_ref[rs, cols].astype(bf16)
            kvf[ci] = _dot_tn((kc * kdf).astype(bf16), vc) * bd
            kvb[ci] = _dot_tn((kc * kdb).astype(bf16), vc) * bd
        s = s0f_ref[0, pi]
        for ci in range(n):
            upd = kvf[ci]
            kvf[ci] = s
            s = s * cdf + upd
        sf_ref[0, pi] = s
        s = s0b_ref[0, pi]
        for ci in range(n - 1, -1, -1):
            upd = kvb[ci]
            kvb[ci] = s
            s = s * cdb + upd
        sb_ref[0, pi] = s
        for ci in range(n):
            rs = slice(ci * c, (ci + 1) * c)
            qc = q_ref[rs, cols]
            kc = k_ref[rs, cols] * (HEAD_DIM ** -0.5)
            vc = v_ref[rs, cols].astype(bf16)
            qb = qc.astype(bf16)
            a0 = _dot_nt(qb, jnp.where(lo, kc, 0.0).astype(bf16)) * dm_ref[2 * pi]
            a1 = _dot_nt(qb, jnp.where(lo, 0.0, kc).astype(bf16)) * dm_ref[2 * pi + 1]
            o = jnp.where(lo, _dot(a0.astype(bf16), vc), _dot(a1.astype(bf16), vc))
            o = o + _dot((qc * qdf).astype(bf16), kvf[ci].astype(bf16))
            o = o + _dot((qc * qdb).astype(bf16), kvb[ci].astype(bf16))
            inv_n = 1.0 / HEAD_DIM
            m0 = jnp.where(lo, o, 0.0).sum(axis=-1, keepdims=True) * inv_n
            m1 = jnp.where(lo, 0.0, o).sum(axis=-1, keepdims=True) * inv_n
            d = o - jnp.where(lo, m0, m1)
            d2 = d * d
            v0 = jnp.where(lo, d2, 0.0).sum(axis=-1, keepdims=True) * inv_n
            v1 = jnp.where(lo, 0.0, d2).sum(axis=-1, keepdims=True) * inv_n
            y = d * lax.rsqrt(jnp.where(lo, v0, v1) + EPS)
            gt = g_ref[rs, cols]
            o_ref[rs, cols] = (gt * jax.nn.sigmoid(gt) * y).astype(o_ref.dtype)


def _ret_tables(lg_f, lg_b):
    c = RET_CHUNK
    lf = jax.nn.log_sigmoid(lg_f.astype(f32))
    lb = jax.nn.log_sigmoid(lg_b.astype(f32))
    idx = jnp.arange(c, dtype=f32)

    def lanes(per_head):
        r = per_head.shape[1]
        x = jnp.repeat(per_head[:, :, None], HEAD_DIM, axis=2)
        x = x.reshape(B_HEADS // 2, 2, r, HEAD_DIM).transpose(0, 2, 1, 3)
        return x.reshape(B_HEADS // 2, r, LANES)

    qdf = lanes(jnp.exp(lf[:, None] * (idx + 1.0)))
    kdf = lanes(jnp.exp(lf[:, None] * (c - 1.0 - idx)))
    qdb = lanes(jnp.exp(lb[:, None] * (c - idx)))
    kdb = lanes(jnp.exp(lb[:, None] * idx))
    diff = idx[:, None] - idx[None, :]
    low = jnp.where(diff >= 0, jnp.exp(lf[:, None, None] * jnp.maximum(diff, 0.0)), 0.0)
    upp = jnp.where(diff <= 0, jnp.exp(lb[:, None, None] * jnp.maximum(-diff, 0.0)), 0.0)
    dm = low + upp
    cdf = lanes(jnp.exp(lf * c)[:, None])
    cdb = lanes(jnp.exp(lb * c)[:, None])
    r = jnp.arange(LANES)
    bd = ((r[:, None] < HEAD_DIM) == (r[None, :] < HEAD_DIM)).astype(f32)
    return qdf, qdb, kdf, kdb, dm, cdf, cdb, bd


def _blockdiag_states(s):
    b = s.shape[0]
    s = s.astype(f32).reshape(b, B_HEADS // 2, 2, HEAD_DIM, HEAD_DIM)
    z = jnp.zeros_like(s[:, :, 0])
    top = jnp.concatenate([s[:, :, 0], z], axis=-1)
    bot = jnp.concatenate([z, s[:, :, 1]], axis=-1)
    return jnp.concatenate([top, bot], axis=-2)


def _diag_states(s):
    b = s.shape[0]
    h0 = s[:, :, :HEAD_DIM, :HEAD_DIM]
    h1 = s[:, :, HEAD_DIM:, HEAD_DIM:]
    return jnp.stack([h0, h1], axis=2).reshape(b, B_HEADS, HEAD_DIM, HEAD_DIM)


def _retention(q, k, v, g, s0f, s0b, tables, batch):
    m = q.shape[0]
    t = m // batch
    w = B_HEADS * HEAD_DIM
    np_ = B_HEADS // 2
    c = RET_CHUNK
    kern = functools.partial(_ret_kernel, t=t)
    tok = pl.BlockSpec((t, w), lambda b: (b, 0))
    st = pl.BlockSpec((1, np_, LANES, LANES), lambda b: (b, 0, 0, 0))

    def full(a):
        nd = a.ndim
        return pl.BlockSpec(a.shape, lambda b: (0,) * nd)

    o, sf, sb = pl.pallas_call(
        kern,
        grid=(batch,),
        in_specs=[tok, tok, tok, tok, st, st] + [full(a) for a in tables],
        out_specs=[tok, st, st],
        out_shape=[jax.ShapeDtypeStruct((m, w), bf16),
                   jax.ShapeDtypeStruct((batch, np_, LANES, LANES), f32),
                   jax.ShapeDtypeStruct((batch, np_, LANES, LANES), f32)],
        scratch_shapes=[pltpu.VMEM((t // c, LANES, LANES), f32), pltpu.VMEM((t // c, LANES, LANES), f32)],
        compiler_params=_cparams(("arbitrary",)),
        name="retention",
    )(q, k, v, g, _blockdiag_states(s0f), _blockdiag_states(s0b), *tables)
    return o, _diag_states(sf), _diag_states(sb)


def _route(h, wr):
    lane = lax.broadcasted_iota(jnp.int32, (h.shape[0], LANES), 1).astype(f32)
    logits = jnp.full((h.shape[0], LANES), -jnp.inf, f32)
    for e in range(N_EXPERTS):
        logits = jnp.where(lane == e, (h * wr[e:e + 1, :]).sum(axis=-1, keepdims=True), logits)
    m1 = logits.max(axis=-1, keepdims=True)
    i1 = jnp.where(logits == m1, lane, float(LANES)).min(axis=-1, keepdims=True)
    rest = jnp.where(lane == i1, -jnp.inf, logits)
    m2 = rest.max(axis=-1, keepdims=True)
    i2 = jnp.where(rest == m2, lane, float(LANES)).min(axis=-1, keepdims=True)
    e2 = jnp.exp(m2 - m1)
    g1 = 1.0 / (1.0 + e2)
    g2 = e2 * g1
    return jnp.where(lane == 0.0, i1, jnp.where(lane == 1.0, i2, jnp.where(lane == 2.0, g1, jnp.where(lane == 3.0, g2, 0.0))))


def _outproj_kernel(a_ref, b_ref, w_ref, x_ref, gate_ref, g2_ref, sh_ref, sc_ref, *rest, has_router):
    if has_router:
        wr_ref, xo_ref, h_ref, info_ref, wb = rest
    else:
        xo_ref, h_ref, wb = rest
    _cast_weights_once(w_ref, wb)
    half = a_ref.shape[1]
    acc = _dot(a_ref[...], wb[:half, :]) + _dot(b_ref[...], wb[half:, :])
    xn = x_ref[...] + gate_ref[0] * acc
    xo_ref[...] = xn
    h = _norm_mod(xn, g2_ref[...], sh_ref[0], sc_ref[0])
    h_ref[...] = h.astype(h_ref.dtype)
    if has_router:
        info_ref[...] = _route(h, wr_ref[...])


def _outproj(a, a_blk, b, b_blk, w, x, g2, mod3, layer, rows_per_batch, is_sample, w_router=None, tm=1024):
    m = x.shape[0]
    tm = min(tm, m)
    half = w.shape[0] // 2
    has_router = w_router is not None
    kern = functools.partial(_outproj_kernel, has_router=has_router)
    row = pl.BlockSpec((tm, D), lambda i: (i, 0))
    in_specs = [pl.BlockSpec((tm, half), lambda i: (i, a_blk)),
                pl.BlockSpec((tm, half), lambda i: (i, b_blk)),
                pl.BlockSpec(w.shape, lambda i: (0, 0)),
                row,
                _mod_spec(layer, 2, tm, rows_per_batch, is_sample),
                pl.BlockSpec((1, D), lambda i: (0, 0)),
                _mod_spec(layer, 3, tm, rows_per_batch, is_sample),
                _mod_spec(layer, 4, tm, rows_per_batch, is_sample)]
    args = [a, b, w, x, mod3, g2, mod3, mod3]
    out_specs = [row, row]
    out_shape = [jax.ShapeDtypeStruct((m, D), f32), jax.ShapeDtypeStruct((m, D), f32 if has_router else bf16)]
    if has_router:
        in_specs.append(pl.BlockSpec((N_EXPERTS, D), lambda i: (0, 0)))
        args.append(w_router)
        out_specs.append(pl.BlockSpec((tm, LANES), lambda i: (i, 0)))
        out_shape.append(jax.ShapeDtypeStruct((m, LANES), f32))
    return pl.pallas_call(
        kern, grid=(m // tm,), in_specs=in_specs, out_specs=out_specs, out_shape=out_shape,
        scratch_shapes=[pltpu.VMEM(w.shape, bf16)],
        compiler_params=_cparams(("arbitrary",)), name="out_proj",
    )(*args)


def _rms(x, g):
    return x * lax.rsqrt(jnp.mean(x * x, axis=-1, keepdims=True) + EPS) * g


FFN_TM = 512
FFN_TF = 256


def _swiglu(h, wgu_ref, wd_ref, act):
    f = wd_ref.shape[0]
    for c0 in range(0, f, FFN_TF):
        g = _dot(h, wgu_ref[:, c0:c0 + FFN_TF])
        u = _dot(h, wgu_ref[:, f + c0:f + c0 + FFN_TF])
        act[:, c0:c0 + FFN_TF] = (g * jax.nn.sigmoid(g) * u).astype(bf16)
    return _dot(act[...], wd_ref[...])


def _ffn_kernel(*refs, final_norm):
    h_ref, wgu_ref, wd_ref, x_ref, gate_ref = refs[:5]
    rest = list(refs[5:])
    fg_ref = rest.pop(0) if final_norm else None
    o_ref, act = rest
    out = x_ref[...] + gate_ref[0] * _swiglu(h_ref[...], wgu_ref, wd_ref, act)
    o_ref[...] = _rms(out, fg_ref[...]) if final_norm else out


def _ffn(h, w_gu, w_down, x, mod3, layer, rows_per_batch, is_sample, final_g=None, tm=FFN_TM):
    m = x.shape[0]
    tm = min(tm, m)
    f = w_down.shape[0]
    final_norm = final_g is not None
    kern = functools.partial(_ffn_kernel, final_norm=final_norm)
    row = pl.BlockSpec((tm, D), lambda i: (i, 0))
    in_specs = [row,
                pl.BlockSpec((D, 2 * f), lambda i: (0, 0)),
                pl.BlockSpec((f, D), lambda i: (0, 0)),
                row,
                _mod_spec(layer, 5, tm, rows_per_batch, is_sample)]
    args = [h, w_gu, w_down, x, mod3]
    if final_norm:
        in_specs.append(pl.BlockSpec((1, D), lambda i: (0, 0)))
        args.append(final_g)
    return pl.pallas_call(
        kern, grid=(m // tm,), in_specs=in_specs, out_specs=row,
        out_shape=jax.ShapeDtypeStruct((m, D), f32),
        scratch_shapes=[pltpu.VMEM((tm, f), bf16)],
        compiler_params=_cparams(("arbitrary",)), name="ffn",
    )(*args)


MOE_TM = FFN_TM
DMA_UNROLL = 8


def _moe_pos_kernel(info_ref, pos_ref, tot_ref, carry, *, tb):
    ph = pl.program_id(0)
    i = pl.program_id(1)
    lane = lax.broadcasted_iota(jnp.int32, (1, LANES), 1).astype(f32)
    info = info_ref[...]
    oh1 = (lane == info[:, 0:1]).astype(f32)
    oh2 = (lane == info[:, 1:2]).astype(f32)
    cnt = oh1 + oh2

    @pl.when((ph == 0) & (i == 0))
    def _():
        carry[...] = jnp.zeros_like(carry)

    @pl.when(ph == 0)
    def _():
        carry[0:1, :] += cnt.sum(axis=0, keepdims=True)

    @pl.when((ph == 1) & (i == 0))
    def _():
        tot = carry[0:1, :]
        tot_ref[...] = jnp.broadcast_to(tot, tot_ref.shape)
        padded = jnp.floor((tot + (MOE_TM - 1)) * (1.0 / MOE_TM)) * MOE_TM
        start = jnp.zeros_like(tot)
        for e in range(N_EXPERTS - 1):
            pe = jnp.where(lane == e, padded, 0.0).sum(axis=-1, keepdims=True)
            start = start + jnp.where(lane > e, pe, 0.0)
        carry[1:2, :] = start

    @pl.when(ph == 1)
    def _():
        r = lax.broadcasted_iota(jnp.int32, (tb, tb), 0)
        c = lax.broadcasted_iota(jnp.int32, (tb, tb), 1)
        tri = jnp.where(r > c, 1.0, 0.0).astype(bf16)
        base = carry[1:2, :] + _dot(tri, cnt.astype(bf16))
        p1 = (oh1 * base).sum(axis=-1, keepdims=True)
        p2 = (oh2 * base).sum(axis=-1, keepdims=True)
        pos_ref[...] = jnp.where(lane == 0.0, p1, jnp.where(lane == 1.0, p2, 0.0)).astype(jnp.int32)
        carry[1:2, :] += cnt.sum(axis=0, keepdims=True)


def _moe_positions(info, tb=1024):
    m = info.shape[0]
    tb = min(tb, m)
    nblk = m // tb
    pos, tot = pl.pallas_call(
        functools.partial(_moe_pos_kernel, tb=tb),
        grid=(2, nblk),
        in_specs=[pl.BlockSpec((tb, LANES), lambda p, i: (i, 0))],
        out_specs=[pl.BlockSpec((tb, LANES), lambda p, i: (i * p, 0)),
                   pl.BlockSpec((8, LANES), lambda p, i: (0, 0))],
        out_shape=[jax.ShapeDtypeStruct((m, LANES), jnp.int32), jax.ShapeDtypeStruct((8, LANES), f32)],
        scratch_shapes=[pltpu.VMEM((8, LANES), f32)],
        compiler_params=_cparams(("arbitrary", "arbitrary")), name="moe_pos",
    )(info)
    return pos[:, :2].reshape(-1), tot[0, :N_EXPERTS].astype(jnp.int32)


def _row_copy(src, s_row, dst, d_row, sem):
    return pltpu.make_async_copy(src.at[pl.ds(s_row, 1)], dst.at[pl.ds(d_row, 1)], sem)


def _moe_dispatch_kernel(pos_ref, zrow_ref, zflag_ref, hp_ref, hs_ref, o_ref, zeros, sem, zsem, *, nbp, tb):
    i = pl.program_id(0)

    @pl.when(i == 0)
    def _():
        zeros[...] = jnp.zeros_like(zeros)

        def zero_tile(z):
            dst = o_ref.at[pl.ds(pl.multiple_of(zrow_ref[z], MOE_TM), MOE_TM)]
            return pltpu.make_async_copy(zeros, dst, zsem)

        for z in range(zrow_ref.shape[0]):
            @pl.when(zflag_ref[z] != 0)
            def _():
                zero_tile(z).start()
        for z in range(zrow_ref.shape[0]):
            @pl.when(zflag_ref[z] != 0)
            def _():
                zero_tile(z).wait()

    def scatter(src):
        def issue(r, _):
            g = 2 * (i * tb + r)
            _row_copy(src, r, o_ref, pos_ref[g], sem).start()
            _row_copy(src, r, o_ref, pos_ref[g + 1], sem).start(priority=1)
            return 0
        lax.fori_loop(0, tb, issue, 0, unroll=DMA_UNROLL)

        def drain(r, _):
            _row_copy(src, 0, o_ref, 0, sem).wait()
            _row_copy(src, 0, o_ref, 0, sem).wait()
            return 0
        lax.fori_loop(0, tb, drain, 0, unroll=DMA_UNROLL)

    @pl.when(i < nbp)
    def _():
        scatter(hp_ref)

    @pl.when(i >= nbp)
    def _():
        scatter(hs_ref)


def _moe_dispatch(pos, zrow, zflag, hp, hs, n_rows, tb=1024):
    mp, ms = hp.shape[0], hs.shape[0]
    tb = min(tb, mp, ms)
    nbp, nbs = mp // tb, ms // tb
    kern = functools.partial(_moe_dispatch_kernel, nbp=nbp, tb=tb)
    return pl.pallas_call(
        kern,
        grid_spec=pltpu.PrefetchScalarGridSpec(
            num_scalar_prefetch=3, grid=(nbp + nbs,),
            in_specs=[pl.BlockSpec((tb, D), lambda i, *_: (jnp.minimum(i, nbp - 1), 0)),
                      pl.BlockSpec((tb, D), lambda i, *_: (jnp.maximum(i - nbp, 0), 0))],
            out_specs=pl.BlockSpec(memory_space=pl.ANY),
            scratch_shapes=[pltpu.VMEM((MOE_TM, D), f32), pltpu.SemaphoreType.DMA(()),
                            pltpu.SemaphoreType.DMA(())]),
        out_shape=jax.ShapeDtypeStruct((n_rows, D), f32),
        compiler_params=_cparams(("arbitrary",)), name="moe_dispatch",
    )(pos, zrow, zflag, hp, hs)


def _moe_ffn_kernel(te_ref, na_ref, h_ref, wgu_ref, wd_ref, o_ref, act):
    i = pl.program_id(0)

    @pl.when(i < na_ref[0])
    def _():
        o_ref[...] = _swiglu(h_ref[...].astype(bf16), wgu_ref.at[0], wd_ref.at[0], act)

    @pl.when(i >= na_ref[0])
    def _():
        o_ref[...] = jnp.zeros_like(o_ref)


def _moe_ffn(h_sorted, w_gu, w_down, tile_expert, n_active):
    n_rows = h_sorted.shape[0]
    tm = MOE_TM
    f = w_down.shape[1]
    n_tiles = n_rows // tm
    return pl.pallas_call(
        _moe_ffn_kernel,
        grid_spec=pltpu.PrefetchScalarGridSpec(
            num_scalar_prefetch=2, grid=(n_tiles,),
            in_specs=[pl.BlockSpec((tm, D), lambda i, te, na: (jnp.minimum(i, na[0] - 1), 0)),
                      pl.BlockSpec((1, D, 2 * f), lambda i, te, na: (te[i], 0, 0)),
                      pl.BlockSpec((1, f, D), lambda i, te, na: (te[i], 0, 0))],
            out_specs=pl.BlockSpec((tm, D), lambda i, te, na: (i, 0)),
            scratch_shapes=[pltpu.VMEM((tm, f), bf16)]),
        out_shape=jax.ShapeDtypeStruct((n_rows, D), f32),
        compiler_params=_cparams(("arbitrary",)), name="moe_ffn",
    )(tile_expert, n_active, h_sorted, w_gu, w_down)


def _moe_combine_kernel(pos_ref, info_ref, x_ref, gate_ref, fg_ref, y_ref, o_ref, ybuf, sem, *, tb, base,
                        final_norm):
    i = pl.program_id(0)
    n = pl.num_programs(0)

    def issue(blk, slot):
        def body(r, _):
            g = 2 * (base + blk * tb + r)
            _row_copy(y_ref, pos_ref[g], ybuf.at[slot, 0], r, sem.at[slot]).start()
            _row_copy(y_ref, pos_ref[g + 1], ybuf.at[slot, 1], r, sem.at[slot]).start(priority=1)
            return 0
        lax.fori_loop(0, tb, body, 0, unroll=DMA_UNROLL)

    @pl.when(i == 0)
    def _():
        issue(0, 0)

    @pl.when(i + 1 < n)
    def _():
        issue(i + 1, (i + 1) % 2)

    slot = i % 2

    def drain(r, _):
        _row_copy(y_ref, 0, ybuf.at[slot, 0], 0, sem.at[slot]).wait()
        _row_copy(y_ref, 0, ybuf.at[slot, 1], 0, sem.at[slot]).wait()
        return 0
    lax.fori_loop(0, tb, drain, 0, unroll=DMA_UNROLL)

    info = info_ref[...]
    y = info[:, 2:3] * ybuf[slot, 0] + info[:, 3:4] * ybuf[slot, 1]
    out = x_ref[...] + gate_ref[0] * y
    o_ref[...] = _rms(out, fg_ref[...]) if final_norm else out


def _moe_combine(pos, info, x, mod3, layer, y_sorted, base, rows_per_batch, is_sample, final_g, tb=512):
    m = x.shape[0]
    tb = min(tb, m)
    final_norm = final_g is not None
    if final_g is None:
        final_g = jnp.ones((1, D), f32)
    kern = functools.partial(_moe_combine_kernel, tb=tb, base=base, final_norm=final_norm)
    return pl.pallas_call(
        kern,
        grid_spec=pltpu.PrefetchScalarGridSpec(
            num_scalar_prefetch=1, grid=(m // tb,),
            in_specs=[pl.BlockSpec((tb, LANES), lambda i, p: (i, 0)),
                      pl.BlockSpec((tb, D), lambda i, p: (i, 0)),
                      _mod_spec(layer, 5, tb, rows_per_batch, is_sample),
                      pl.BlockSpec((1, D), lambda i, p: (0, 0)),
                      pl.BlockSpec(memory_space=pl.ANY)],
            out_specs=pl.BlockSpec((tb, D), lambda i, p: (i, 0)),
            scratch_shapes=[pltpu.VMEM((2, 2, tb, D), f32), pltpu.SemaphoreType.DMA((2,))]),
        out_shape=jax.ShapeDtypeStruct((m, D), f32),
        compiler_params=_cparams(("arbitrary",)), name="moe_combine",
    )(pos, info, x, mod3, final_g, y_sorted)


def _moe(hp, hs, info_p, info_s, xp, xs, w_gu, w_down, mod3, layer, tp, ts, final_g):
    mp, ms = hp.shape[0], hs.shape[0]
    tm = MOE_TM
    n_tiles = (2 * (mp + ms)) // tm + N_EXPERTS
    pos, counts = _moe_positions(jnp.concatenate([info_p, info_s], axis=0))
    tiles_per = (counts + tm - 1) // tm
    cum = jnp.cumsum(tiles_per)
    n_active = cum[-1:].astype(jnp.int32)
    ids = jnp.arange(n_tiles, dtype=jnp.int32)
    te = jnp.minimum((ids[:, None] >= cum[None, :]).astype(jnp.int32).sum(axis=1), N_EXPERTS - 1)
    last_e = te[jnp.maximum(n_active[0] - 1, 0)]
    te = jnp.where(ids < n_active[0], te, last_e)
    tail = jnp.arange(n_tiles - N_EXPERTS, n_tiles, dtype=jnp.int32)
    zrow = jnp.concatenate([jnp.maximum(cum - 1, 0), tail]).astype(jnp.int32) * tm
    zflag = jnp.concatenate([counts % tm != 0, tail >= n_active[0]]).astype(jnp.int32)
    h_sorted = _moe_dispatch(pos, zrow, zflag, hp, hs, n_tiles * tm)
    y_sorted = _moe_ffn(h_sorted, w_gu, w_down, te, n_active)
    yp = _moe_combine(pos, info_p, xp, mod3, layer, y_sorted, 0, tp, False, final_g)
    ys = _moe_combine(pos, info_s, xs, mod3, layer, y_sorted, mp, ts, True, final_g)
    return yp, ys


def kernel(x_prompt, x_sample, cache_a_k, cache_a_v, state_ret_fwd, state_ret_bwd, cache_c_k, cache_c_v, c, c_ctx,
           mod_w, mod_b, norm1_g, norm2_g, final_g, ev_w_in, ev_w_out, ev_sink, ev_ret_logit_fwd, ev_ret_logit_bwd,
           ev_ffn_w_gu, ev_ffn_w_down, od_w_qkv, od_w_out, od_rpb, od_w_router, od_moe_w_gu, od_moe_w_down):
    bp, tp, _ = x_prompt.shape
    bs, ts, _ = x_sample.shape
    depth = mod_w.shape[0]
    assert 1 + bs <= MOD_ROWS
    xp = x_prompt.reshape(bp * tp, D)
    xs = x_sample.reshape(bs * ts, D)

    cvecs = jnp.zeros((MOD_ROWS, D), f32).at[0].set(c_ctx).at[1:1 + bs].set(c)
    mod3 = _adaln(cvecs, mod_w, mod_b)

    a_q = A_HEADS * HEAD_DIM
    a_kv = A_KV_HEADS * HEAD_DIM
    b_w = B_HEADS * HEAD_DIM
    c_w = C_HEADS * HEAD_DIM
    streams = ((False, tp), (True, ts))
    outs = {}
    for i in range(depth):
        j = i // 2
        g1 = norm1_g[i].reshape(1, D)
        g2 = norm2_g[i].reshape(1, D)
        last = i == depth - 1
        fg = final_g.reshape(1, D) if last else None
        if i % 2 == 0:
            w_in = ev_w_in[j]
            w_out = ev_w_out[j]
            splits = (a_q, a_kv, a_kv, b_w, b_w, b_w, b_w)
            dts_p = (bf16, f32, f32, f32, f32, bf16, f32)
            dts_s = (f32, f32, f32, f32, f32, bf16, f32)
            tables = _ret_tables(ev_ret_logit_fwd[j], ev_ret_logit_bwd[j])
            qa_p, ka_p, va_p, qr_p, kr_p, vr_p, gr_p = _proj(xp, g1, mod3, i, w_in, splits, dts_p, tp, False)
            qa_s, ka_s, va_s, qr_s, kr_s, vr_s, gr_s = _proj(xs, g1, mod3, i, w_in, splits, dts_s, ts, True)
            oa_p = _ctx_attn(qa_p, ka_p, va_p, ev_sink[j], bp, A_HEADS, A_KV_HEADS)
            zero = jnp.zeros((bp, B_HEADS, HEAD_DIM, HEAD_DIM), f32)
            or_p, sf, sb = _retention(qr_p, kr_p, vr_p, gr_p, zero, zero, tables, bp)
            lc = cache_a_k.shape[2]
            oa_s = _win_attn(qa_s, ka_s, va_s, cache_a_k[:, j].reshape(bs, lc, a_kv),
                             cache_a_v[:, j].reshape(bs, lc, a_kv), ev_sink[j], bs)
            or_s, _, _ = _retention(qr_s, kr_s, vr_s, gr_s, state_ret_fwd[:, j], state_ret_bwd[:, j], tables, bs)
            outs.setdefault('a_k', []).append(ka_p.reshape(bp, tp, A_KV_HEADS, HEAD_DIM))
            outs.setdefault('a_v', []).append(va_p.reshape(bp, tp, A_KV_HEADS, HEAD_DIM))
            outs.setdefault('r_f', []).append(sf)
            outs.setdefault('r_b', []).append(sb)
            xp, hp = _outproj(oa_p, 0, or_p, 0, w_out, xp, g2, mod3, i, tp, False)
            xs, hs = _outproj(oa_s, 0, or_s, 0, w_out, xs, g2, mod3, i, ts, True)
            w_gu = ev_ffn_w_gu[j].astype(bf16)
            w_dn = ev_ffn_w_down[j].astype(bf16)
            xp = _ffn(hp, w_gu, w_dn, xp, mod3, i, tp, False, final_g=fg)
            xs = _ffn(hs, w_gu, w_dn, xs, mod3, i, ts, True, final_g=fg)
        else:
            w_qkv = od_w_qkv[j]
            w_out = od_w_out[j]
            w_router = od_w_router[j].astype(f32).T
            splits = (c_w, c_w, c_w)
            q_p, k_p, v_p = _proj(xp, g1, mod3, i, w_qkv, splits, (bf16, f32, f32), tp, False)
            q_s, k_s, v_s = _proj(xs, g1, mod3, i, w_qkv, splits, (bf16, bf16, bf16), ts, True)
            o_p = _ctx_attn(q_p, k_p, v_p, None, bp, C_HEADS, C_HEADS)
            lc = cache_c_k.shape[2]
            o_s = _nb_attn(q_s, k_s, v_s, cache_c_k[:, j].reshape(bs, lc, c_w),
                           cache_c_v[:, j].reshape(bs, lc, c_w), od_rpb[j], bs)
            outs.setdefault('c_k', []).append(k_p.reshape(bp, tp, C_HEADS, HEAD_DIM))
            outs.setdefault('c_v', []).append(v_p.reshape(bp, tp, C_HEADS, HEAD_DIM))
            xp, hp, info_p = _outproj(o_p, 0, o_p, 1, w_out, xp, g2, mod3, i, tp, False, w_router=w_router)
            xs, hs, info_s = _outproj(o_s, 0, o_s, 1, w_out, xs, g2, mod3, i, ts, True, w_router=w_router)
            xp, xs = _moe(hp, hs, info_p, info_s, xp, xs, od_moe_w_gu[j].astype(bf16),
                          od_moe_w_down[j].astype(bf16), mod3, i, tp, ts, fg)
    y_prompt = xp.reshape(bp, tp, D)
    y_sample = xs.reshape(bs, ts, D)
    return (y_prompt, y_sample,
            jnp.stack(outs['a_k'], axis=1), jnp.stack(outs['a_v'], axis=1),
            jnp.stack(outs['r_f'], axis=1), jnp.stack(outs['r_b'], axis=1),
            jnp.stack(outs['c_k'], axis=1), jnp.stack(outs['c_v'], axis=1))
```

```python
import functools

import numpy as np
import jax
import jax.numpy as jnp
from jax import lax
from jax.experimental import pallas as pl
from jax.experimental.pallas import tpu as pltpu

f32 = jnp.float32
bf16 = jnp.bfloat16

D = 1024
HEAD_DIM = 64
GRID_W = 64
ROPE_BASE = 10000.0
A_HEADS = 8
A_KV_HEADS = 2
WINDOW = 128
B_HEADS = 8
RET_CHUNK = 128
C_HEADS = 16
NB_ROWS = 8
NB_COLS = 16
N_EXPERTS = 8
EPS = 1e-6
NEG = -1e30

LANES = 128
MOD_ROWS = 16
VMEM_LIMIT_MB = 56


def _cparams(sem, vmem_mb=VMEM_LIMIT_MB):
    return pltpu.CompilerParams(dimension_semantics=sem, vmem_limit_bytes=vmem_mb * 1024 * 1024)


def _dot(a, b):
    return jnp.dot(a, b, preferred_element_type=f32)


def _dot_nt(a, b):
    return lax.dot_general(a, b, (((1,), (1,)), ((), ())), preferred_element_type=f32)


def _dot_tn(a, b):
    return lax.dot_general(a, b, (((0,), (0,)), ((), ())), preferred_element_type=f32)


def _lane_lo():
    return lax.broadcasted_iota(jnp.int32, (1, LANES), 1) < HEAD_DIM


def _swap_halves(x):
    return pltpu.roll(x, HEAD_DIM, 1)


def _mod_imap(layer, which, tm, rows_per_batch, is_sample):
    def imap(i, *_):
        r = (1 + (i * tm) // rows_per_batch) if is_sample else 0
        return ((layer * MOD_ROWS + r) * 6 + which, 0, 0)

    return imap


def _mod_spec(layer, which, tm, rows_per_batch, is_sample):
    return pl.BlockSpec((1, 1, D), _mod_imap(layer, which, tm, rows_per_batch, is_sample))


def _adaln_kernel(cv_ref, w_ref, b_ref, o_ref):
    cv = cv_ref[...]
    s = (cv * jax.nn.sigmoid(cv)).astype(bf16)
    o_ref[0] = _dot(s, w_ref[0].astype(bf16)) + b_ref[0]


def _adaln(cvecs, mod_w, mod_b):
    depth = mod_w.shape[0]
    tn = 1024
    out = pl.pallas_call(
        _adaln_kernel,
        grid=(depth, 6 * D // tn),
        in_specs=[pl.BlockSpec((MOD_ROWS, D), lambda l, j: (0, 0)),
                  pl.BlockSpec((1, D, tn), lambda l, j: (l, 0, j)),
                  pl.BlockSpec((1, 1, tn), lambda l, j: (l, 0, j))],
        out_specs=pl.BlockSpec((1, MOD_ROWS, tn), lambda l, j: (l, 0, j)),
        out_shape=jax.ShapeDtypeStruct((depth, MOD_ROWS, 6 * D), f32),
        compiler_params=_cparams(("arbitrary", "arbitrary")),
        name="adaln",
    )(cvecs, mod_w, mod_b.reshape(depth, 1, 6 * D))
    return out.reshape(depth * MOD_ROWS * 6, 1, D)


def _norm_mod(x, g, shift, scale):
    y = x * lax.rsqrt(jnp.mean(x * x, axis=-1, keepdims=True) + EPS) * g
    return y * (1.0 + scale) + shift


def _cast_weights_once(w_ref, wb):
    @pl.when(pl.program_id(0) == 0)
    def _():
        wb[...] = w_ref[...].astype(bf16)


def _proj_kernel(x_ref, g_ref, sh_ref, sc_ref, w_ref, *rest, splits, chunk):
    out_refs, wb = rest[:-1], rest[-1]
    _cast_weights_once(w_ref, wb)
    h = _norm_mod(x_ref[...], g_ref[...], sh_ref[0], sc_ref[0]).astype(bf16)
    off = 0
    for o_ref, width in zip(out_refs, splits):
        for c0 in range(0, width, chunk):
            cw = min(chunk, width - c0)
            o_ref[:, c0:c0 + cw] = _dot(h, wb[:, off + c0:off + c0 + cw]).astype(o_ref.dtype)
        off += width


def _proj(x, g, mod3, layer, w, splits, dtypes, rows_per_batch, is_sample, tm=512):
    m = x.shape[0]
    tm = min(tm, m)
    n = w.shape[1]
    kern = functools.partial(_proj_kernel, splits=tuple(splits), chunk=512)
    return pl.pallas_call(
        kern,
        grid=(m // tm,),
        in_specs=[pl.BlockSpec((tm, D), lambda i: (i, 0)),
                  pl.BlockSpec((1, D), lambda i: (0, 0)),
                  _mod_spec(layer, 0, tm, rows_per_batch, is_sample),
                  _mod_spec(layer, 1, tm, rows_per_batch, is_sample),
                  pl.BlockSpec((D, n), lambda i: (0, 0))],
        out_specs=[pl.BlockSpec((tm, s), lambda i: (i, 0)) for s in splits],
        out_shape=[jax.ShapeDtypeStruct((m, s), dt) for s, dt in zip(splits, dtypes)],
        scratch_shapes=[pltpu.VMEM((D, n), bf16)],
        compiler_params=_cparams(("arbitrary",)),
        name="norm_proj",
    )(x, g, mod3, mod3, w)


def _softmax_parts(parts, sink):
    m = parts[0].max(axis=-1, keepdims=True)
    for s in parts[1:]:
        m = jnp.maximum(m, s.max(axis=-1, keepdims=True))
    if sink is not None:
        m = jnp.maximum(m, sink)
    es = [jnp.exp(s - m) for s in parts]
    den = es[0].sum(axis=-1, keepdims=True)
    for e in es[1:]:
        den = den + e.sum(axis=-1, keepdims=True)
    if sink is not None:
        den = den + jnp.exp(sink - m)
    return [e.astype(bf16) for e in es], 1.0 / den


def _ctx_attn_kernel(sink_ref, q_ref, k_ref, v_ref, o_ref, *, heads, kv_heads, has_sink):
    rep = heads // kv_heads
    lo = _lane_lo()
    scale = HEAD_DIM ** -0.5
    for pi in range(heads // 2):
        qp = q_ref[:, pi * LANES:(pi + 1) * LANES].astype(f32) * scale
        outs = []
        for hh in range(2):
            h = 2 * pi + hh
            g = h // rep
            kp, kh = g // 2, g % 2
            qv = (qp if kh == hh else _swap_halves(qp)).astype(bf16)
            kv = k_ref[:, kp * LANES:(kp + 1) * LANES]
            km = (jnp.where(lo, kv, 0.0) if kh == 0 else jnp.where(lo, 0.0, kv)).astype(bf16)
            (p,), inv = _softmax_parts([_dot_nt(qv, km)], sink_ref[h] if has_sink else None)
            o = _dot(p, v_ref[:, kp * LANES:(kp + 1) * LANES].astype(bf16)) * inv
            outs.append(o if kh == hh else _swap_halves(o))
        o_ref[:, pi * LANES:(pi + 1) * LANES] = jnp.where(lo, outs[0], outs[1]).astype(o_ref.dtype)


def _ctx_attn(q, k, v, sink, batch, heads, kv_heads):
    m = q.shape[0]
    t = m // batch
    has_sink = sink is not None
    if sink is None:
        sink = jnp.zeros((heads,), f32)
    kern = functools.partial(_ctx_attn_kernel, heads=heads, kv_heads=kv_heads, has_sink=has_sink)
    return pl.pallas_call(
        kern,
        grid=(batch,),
        in_specs=[pl.BlockSpec(memory_space=pltpu.SMEM),
                  pl.BlockSpec((t, heads * HEAD_DIM), lambda b: (b, 0)),
                  pl.BlockSpec((t, kv_heads * HEAD_DIM), lambda b: (b, 0)),
                  pl.BlockSpec((t, kv_heads * HEAD_DIM), lambda b: (b, 0))],
        out_specs=pl.BlockSpec((t, heads * HEAD_DIM), lambda b: (b, 0)),
        out_shape=jax.ShapeDtypeStruct((m, heads * HEAD_DIM), bf16),
        compiler_params=_cparams(("arbitrary",)),
        name="ctx_attn",
    )(sink.astype(f32), q, k, v)


def _rope(x, cos, sin_signed):
    lane = lax.broadcasted_iota(jnp.int32, (1, LANES), 1)
    first = (lane % 32) < 16
    rot = jnp.where(first, pltpu.roll(x, LANES - 16, 1), pltpu.roll(x, 16, 1))
    return x * cos + rot * sin_signed


def _win_attn_kernel(sink_ref, q_ref, k_ref, v_ref, ck_ref, cv_ref, cosq_ref, sinq_ref, cosk_ref, sink_k_ref,
                     o_ref, klm, vl, ckm, cvb, *, qblk, t_lat):
    n = pl.program_id(1)
    lo = _lane_lo()
    span = 3 * qblk

    @pl.when(n == 0)
    def _():
        kr = _rope(k_ref[...], cosk_ref[...], sink_k_ref[...])
        klm[0] = jnp.where(lo, kr, 0.0).astype(bf16)
        klm[1] = jnp.where(lo, 0.0, kr).astype(bf16)
        vl[...] = v_ref[...].astype(bf16)
        c = ck_ref[0]
        ckm[0] = jnp.where(lo, c, 0.0).astype(bf16)
        ckm[1] = jnp.where(lo, 0.0, c).astype(bf16)
        cvb[...] = cv_ref[0].astype(bf16)

    start = pl.multiple_of(jnp.clip(n * qblk - qblk, 0, t_lat - span), qblk)
    rep = A_HEADS // A_KV_HEADS
    qpos = n * qblk + lax.broadcasted_iota(jnp.int32, (rep * qblk, 1), 0) % qblk
    kpos = start + lax.broadcasted_iota(jnp.int32, (1, span), 1)
    valid = jnp.abs(kpos - qpos) <= WINDOW
    scale = HEAD_DIM ** -0.5
    cq = cosq_ref[...]
    sq = sinq_ref[...]
    roped = [_rope(q_ref[:, pi * LANES:(pi + 1) * LANES], cq, sq) * scale for pi in range(A_HEADS // 2)]
    outs = [None] * A_HEADS
    for g in range(A_KV_HEADS):
        group = range(g * rep, (g + 1) * rep)
        qst = jnp.concatenate([roped[h // 2] if h % 2 == g else _swap_halves(roped[h // 2]) for h in group],
                              axis=0).astype(bf16)
        sink = jnp.concatenate([jnp.full((qblk, 1), sink_ref[h], f32) for h in group], axis=0)
        s_c = _dot_nt(qst, ckm[g])
        s_l = jnp.where(valid, _dot_nt(qst, klm[g, pl.ds(start, span), :]), NEG)
        (p_c, p_l), inv = _softmax_parts([s_c, s_l], sink)
        o = (_dot(p_c, cvb[...]) + _dot(p_l, vl[pl.ds(start, span), :])) * inv
        for idx, h in enumerate(group):
            oh = o[idx * qblk:(idx + 1) * qblk]
            outs[h] = oh if h % 2 == g else _swap_halves(oh)
    for pi in range(A_HEADS // 2):
        o_ref[:, pi * LANES:(pi + 1) * LANES] = jnp.where(lo, outs[2 * pi], outs[2 * pi + 1]).astype(o_ref.dtype)


def _rope_tables(t):
    half = HEAD_DIM // 2
    nf = half // 2
    pos = jnp.arange(t)
    row = (pos // GRID_W).astype(f32)
    col = (pos % GRID_W).astype(f32)
    inv = ROPE_BASE ** (-jnp.arange(nf, dtype=f32) / nf)
    ang_r = row[:, None] * inv
    ang_c = col[:, None] * inv
    ang = jnp.concatenate([ang_r, ang_r, ang_c, ang_c], axis=-1)
    sign = jnp.concatenate([-jnp.ones((nf,), f32), jnp.ones((nf,), f32)] * 2)
    cos = jnp.cos(ang)
    sin_signed = jnp.sin(ang) * sign
    return jnp.tile(cos, (1, 2)), jnp.tile(sin_signed, (1, 2))


def _win_attn(q, k, v, ck, cv, sink, batch):
    m = q.shape[0]
    t = m // batch
    qblk = 128
    nb = t // qblk
    lc = ck.shape[1]
    cos, sin_s = _rope_tables(t)
    kern = functools.partial(_win_attn_kernel, qblk=qblk, t_lat=t)
    return pl.pallas_call(
        kern,
        grid=(batch, nb),
        in_specs=[pl.BlockSpec(memory_space=pltpu.SMEM),
                  pl.BlockSpec((qblk, A_HEADS * HEAD_DIM), lambda b, n: (b * nb + n, 0)),
                  pl.BlockSpec((t, LANES), lambda b, n: (b, 0)),
                  pl.BlockSpec((t, LANES), lambda b, n: (b, 0)),
                  pl.BlockSpec((1, lc, LANES), lambda b, n: (b, 0, 0)),
                  pl.BlockSpec((1, lc, LANES), lambda b, n: (b, 0, 0)),
                  pl.BlockSpec((qblk, LANES), lambda b, n: (n, 0)),
                  pl.BlockSpec((qblk, LANES), lambda b, n: (n, 0)),
                  pl.BlockSpec((t, LANES), lambda b, n: (0, 0)),
                  pl.BlockSpec((t, LANES), lambda b, n: (0, 0))],
        out_specs=pl.BlockSpec((qblk, A_HEADS * HEAD_DIM), lambda b, n: (b * nb + n, 0)),
        out_shape=jax.ShapeDtypeStruct((m, A_HEADS * HEAD_DIM), bf16),
        scratch_shapes=[pltpu.VMEM((2, t, LANES), bf16), pltpu.VMEM((t, LANES), bf16),
                        pltpu.VMEM((2, lc, LANES), bf16), pltpu.VMEM((lc, LANES), bf16)],
        compiler_params=_cparams(("arbitrary", "arbitrary")),
        name="win_attn",
    )(sink.astype(f32), q, k, v, ck, cv, cos, sin_s, cos, sin_s)


def _nb_attn_kernel(q_ref, k_ref, v_ref, ck_ref, cv_ref, tl_ref, tr_ref, *rest, t_lat, qblk, n_cast):
    cast_in, cast_out, (o_ref, bias) = rest[:n_cast], rest[n_cast:2 * n_cast], rest[2 * n_cast:]
    for src, dst in zip(cast_in, cast_out):
        dst[...] = src[...].astype(bf16)
    b = pl.program_id(1)
    lo = _lane_lo()
    rows = t_lat // GRID_W
    wr = min(NB_ROWS, rows)
    neg_slot = 2 * NB_ROWS - 1

    @pl.when(b == 0)
    def _():
        for hh in range(2):
            for rq in range(rows):
                k0 = min(max(rq - wr // 2, 0), rows - wr)
                for p in range(rows // 2):
                    idx = []
                    for rk in (2 * p, 2 * p + 1):
                        idx.append(rk - rq + NB_ROWS - 1 if k0 <= rk < k0 + wr else neg_slot)
                    bias[hh, rq * GRID_W:(rq + 1) * GRID_W, p * LANES:(p + 1) * LANES] = (
                        tl_ref[hh, idx[0]] + tr_ref[hh, idx[1]])

    kb = k_ref[...]
    km = [jnp.where(lo, kb, 0.0).astype(bf16), jnp.where(lo, 0.0, kb).astype(bf16)]
    vb = v_ref[...].astype(bf16)
    cb = ck_ref[0]
    ckm = [jnp.where(lo, cb, 0.0).astype(bf16), jnp.where(lo, 0.0, cb).astype(bf16)]
    cvb = cv_ref[0].astype(bf16)
    scale = HEAD_DIM ** -0.5
    for qb in range(t_lat // qblk):
        rs = slice(qb * qblk, (qb + 1) * qblk)
        r0, r1 = qb * qblk // GRID_W, ((qb + 1) * qblk - 1) // GRID_W
        k_lo = min(max(r0 - wr // 2, 0), rows - wr) * GRID_W
        k_hi = (min(max(r1 - wr // 2, 0), rows - wr) + wr) * GRID_W
        ks = slice(k_lo // LANES * LANES, -(-k_hi // LANES) * LANES)
        qv = (q_ref[rs, :] * scale).astype(bf16)
        outs = []
        for hh in range(2):
            s_c = _dot_nt(qv, ckm[hh])
            s_n = _dot_nt(qv, km[hh][ks]) + bias[hh, rs, ks]
            (p_c, p_n), inv = _softmax_parts([s_c, s_n], None)
            outs.append((_dot(p_c, cvb) + _dot(p_n, vb[ks])) * inv)
        o_ref[rs, :] = jnp.where(lo, outs[0], outs[1]).astype(o_ref.dtype)


def _nb_bias_tiles(rpb):
    heads = rpb.shape[0]
    cq = jnp.arange(GRID_W)
    cstart = jnp.clip(cq - NB_COLS // 2, 0, GRID_W - NB_COLS)
    col_ok = (cq[None, :] >= cstart[:, None]) & (cq[None, :] < cstart[:, None] + NB_COLS)
    dc = jnp.clip(cq[None, :] - cq[:, None], 1 - NB_COLS, NB_COLS - 1) + NB_COLS - 1
    onehot = jax.nn.one_hot(dc.reshape(-1), 2 * NB_COLS - 1, dtype=f32)
    tm = jnp.einsum('hab,kb->hak', rpb.astype(f32), onehot, precision=lax.Precision.HIGHEST)
    tm = tm.reshape(heads, 2 * NB_ROWS - 1, GRID_W, GRID_W)
    tm = jnp.where(col_ok[None, None], tm, NEG)
    tm = jnp.concatenate([tm, jnp.full((heads, 1, GRID_W, GRID_W), NEG, f32)], axis=1)
    z = jnp.zeros_like(tm)
    return jnp.concatenate([tm, z], axis=-1), jnp.concatenate([z, tm], axis=-1)


def _nb_attn(q, k, v, ck, cv, rpb, batch, cast=()):
    m = q.shape[0]
    t = m // batch
    lc = ck.shape[1]
    npairs = C_HEADS // 2
    steps = npairs * batch
    tl, tr = _nb_bias_tiles(rpb)
    kern = functools.partial(_nb_attn_kernel, t_lat=t, qblk=256, n_cast=len(cast))
    cast_specs = []
    for w in cast:
        rows = w.shape[0] // steps
        assert rows * steps == w.shape[0] and rows % 16 == 0, (w.shape, steps)
        cast_specs.append(pl.BlockSpec((rows, w.shape[1]), lambda p, b: (p * batch + b, 0)))
    outs = pl.pallas_call(
        kern,
        grid=(npairs, batch),
        in_specs=[pl.BlockSpec((t, LANES), lambda p, b: (b, p)),
                  pl.BlockSpec((t, LANES), lambda p, b: (b, p)),
                  pl.BlockSpec((t, LANES), lambda p, b: (b, p)),
                  pl.BlockSpec((1, lc, LANES), lambda p, b: (b, 0, p)),
                  pl.BlockSpec((1, lc, LANES), lambda p, b: (b, 0, p)),
                  pl.BlockSpec((2, 2 * NB_ROWS, GRID_W, LANES), lambda p, b: (p, 0, 0, 0)),
                  pl.BlockSpec((2, 2 * NB_ROWS, GRID_W, LANES), lambda p, b: (p, 0, 0, 0))] + cast_specs,
        out_specs=cast_specs + [pl.BlockSpec((t, LANES), lambda p, b: (b, p))],
        out_shape=[jax.ShapeDtypeStruct(w.shape, bf16) for w in cast]
        + [jax.ShapeDtypeStruct((m, C_HEADS * HEAD_DIM), bf16)],
        scratch_shapes=[pltpu.VMEM((2, t, t), f32)],
        compiler_params=_cparams(("arbitrary", "arbitrary")),
        name="nb_attn",
    )(q, k, v, ck, cv, tl, tr, *cast)
    return outs[-1], outs[:-1]


def _ret_kernel(q_ref, k_ref, v_ref, g_ref, s0f_ref, s0b_ref, qdf_ref, qdb_ref, kdf_ref, kdb_ref, dm_ref,
                cdf_ref, cdb_ref, bd_ref, o_ref, sf_ref, sb_ref, kvf, kvb, *, t):
    c = RET_CHUNK
    n = t // c
    lo = _lane_lo()
    bd = bd_ref[...]
    for pi in range(B_HEADS // 2):
        cols = slice(pi * LANES, (pi + 1) * LANES)
        qdf, qdb, kdf, kdb = qdf_ref[pi], qdb_ref[pi], kdf_ref[pi], kdb_ref[pi]
        cdf, cdb = cdf_ref[pi], cdb_ref[pi]
        for ci in range(n):
            rs = slice(ci * c, (ci + 1) * c)
            kc = k_ref[rs, cols] * (HEAD_DIM ** -0.5)
            vc = v_ref[rs, cols].astype(bf16)
            kvf[ci] = _dot_tn((kc * kdf).astype(bf16), vc) * bd
            kvb[ci] = _dot_tn((kc * kdb).astype(bf16), vc) * bd
        s = s0f_ref[0, pi]
        for ci in range(n):
            upd = kvf[ci]
            kvf[ci] = s
            s = s * cdf + upd
        sf_ref[0, pi] = s
        s = s0b_ref[0, pi]
        for ci in range(n - 1, -1, -1):
            upd = kvb[ci]
            kvb[ci] = s
            s = s * cdb + upd
        sb_ref[0, pi] = s
        for ci in range(n):
            rs = slice(ci * c, (ci + 1) * c)
            qc = q_ref[rs, cols]
            kc = k_ref[rs, cols] * (HEAD_DIM ** -0.5)
            vc = v_ref[rs, cols].astype(bf16)
            qb = qc.astype(bf16)
            a0 = _dot_nt(qb, jnp.where(lo, kc, 0.0).astype(bf16)) * dm_ref[2 * pi]
            a1 = _dot_nt(qb, jnp.where(lo, 0.0, kc).astype(bf16)) * dm_ref[2 * pi + 1]
            o = jnp.where(lo, _dot(a0.astype(bf16), vc), _dot(a1.astype(bf16), vc))
            o = o + _dot((qc * qdf).astype(bf16), kvf[ci].astype(bf16))
            o = o + _dot((qc * qdb).astype(bf16), kvb[ci].astype(bf16))
            inv_n = 1.0 / HEAD_DIM
            m0 = jnp.where(lo, o, 0.0).sum(axis=-1, keepdims=True) * inv_n
            m1 = jnp.where(lo, 0.0, o).sum(axis=-1, keepdims=True) * inv_n
            d = o - jnp.where(lo, m0, m1)
            d2 = d * d
            v0 = jnp.where(lo, d2, 0.0).sum(axis=-1, keepdims=True) * inv_n
            v1 = jnp.where(lo, 0.0, d2).sum(axis=-1, keepdims=True) * inv_n
            y = d * lax.rsqrt(jnp.where(lo, v0, v1) + EPS)
            gt = g_ref[rs, cols]
            o_ref[rs, cols] = (gt * jax.nn.sigmoid(gt) * y).astype(o_ref.dtype)


def _ret_tables(lg_f, lg_b):
    c = RET_CHUNK
    lf = jax.nn.log_sigmoid(lg_f.astype(f32))
    lb = jax.nn.log_sigmoid(lg_b.astype(f32))
    idx = jnp.arange(c, dtype=f32)

    def lanes(per_head):
        r = per_head.shape[1]
        x = jnp.repeat(per_head[:, :, None], HEAD_DIM, axis=2)
        x = x.reshape(B_HEADS // 2, 2, r, HEAD_DIM).transpose(0, 2, 1, 3)
        return x.reshape(B_HEADS // 2, r, LANES)

    qdf = lanes(jnp.exp(lf[:, None] * (idx + 1.0)))
    kdf = lanes(jnp.exp(lf[:, None] * (c - 1.0 - idx)))
    qdb = lanes(jnp.exp(lb[:, None] * (c - idx)))
    kdb = lanes(jnp.exp(lb[:, None] * idx))
    diff = idx[:, None] - idx[None, :]
    low = jnp.where(diff >= 0, jnp.exp(lf[:, None, None] * jnp.maximum(diff, 0.0)), 0.0)
    upp = jnp.where(diff <= 0, jnp.exp(lb[:, None, None] * jnp.maximum(-diff, 0.0)), 0.0)
    dm = low + upp
    cdf = lanes(jnp.exp(lf * c)[:, None])
    cdb = lanes(jnp.exp(lb * c)[:, None])
    r = jnp.arange(LANES)
    bd = ((r[:, None] < HEAD_DIM) == (r[None, :] < HEAD_DIM)).astype(f32)
    return qdf, qdb, kdf, kdb, dm, cdf, cdb, bd


def _blockdiag_states(s):
    b = s.shape[0]
    s = s.astype(f32).reshape(b, B_HEADS // 2, 2, HEAD_DIM, HEAD_DIM)
    z = jnp.zeros_like(s[:, :, 0])
    top = jnp.concatenate([s[:, :, 0], z], axis=-1)
    bot = jnp.concatenate([z, s[:, :, 1]], axis=-1)
    return jnp.concatenate([top, bot], axis=-2)


def _diag_states(s):
    b = s.shape[0]
    h0 = s[:, :, :HEAD_DIM, :HEAD_DIM]
    h1 = s[:, :, HEAD_DIM:, HEAD_DIM:]
    return jnp.stack([h0, h1], axis=2).reshape(b, B_HEADS, HEAD_DIM, HEAD_DIM)


def _retention(q, k, v, g, s0f, s0b, tables, batch):
    m = q.shape[0]
    t = m // batch
    w = B_HEADS * HEAD_DIM
    np_ = B_HEADS // 2
    c = RET_CHUNK
    kern = functools.partial(_ret_kernel, t=t)
    tok = pl.BlockSpec((t, w), lambda b: (b, 0))
    st = pl.BlockSpec((1, np_, LANES, LANES), lambda b: (b, 0, 0, 0))

    def full(a):
        nd = a.ndim
        return pl.BlockSpec(a.shape, lambda b: (0,) * nd)

    o, sf, sb = pl.pallas_call(
        kern,
        grid=(batch,),
        in_specs=[tok, tok, tok, tok, st, st] + [full(a) for a in tables],
        out_specs=[tok, st, st],
        out_shape=[jax.ShapeDtypeStruct((m, w), bf16),
                   jax.ShapeDtypeStruct((batch, np_, LANES, LANES), f32),
                   jax.ShapeDtypeStruct((batch, np_, LANES, LANES), f32)],
        scratch_shapes=[pltpu.VMEM((t // c, LANES, LANES), f32), pltpu.VMEM((t // c, LANES, LANES), f32)],
        compiler_params=_cparams(("arbitrary",)),
        name="retention",
    )(q, k, v, g, _blockdiag_states(s0f), _blockdiag_states(s0b), *tables)
    return o, _diag_states(sf), _diag_states(sb)


def _route(h, wr):
    lane = lax.broadcasted_iota(jnp.int32, (h.shape[0], LANES), 1).astype(f32)
    logits = jnp.full((h.shape[0], LANES), -jnp.inf, f32)
    for e in range(N_EXPERTS):
        logits = jnp.where(lane == e, (h * wr[e:e + 1, :]).sum(axis=-1, keepdims=True), logits)
    m1 = logits.max(axis=-1, keepdims=True)
    i1 = jnp.where(logits == m1, lane, float(LANES)).min(axis=-1, keepdims=True)
    rest = jnp.where(lane == i1, -jnp.inf, logits)
    m2 = rest.max(axis=-1, keepdims=True)
    i2 = jnp.where(rest == m2, lane, float(LANES)).min(axis=-1, keepdims=True)
    e2 = jnp.exp(m2 - m1)
    g1 = 1.0 / (1.0 + e2)
    g2 = e2 * g1
    return jnp.where(lane == 0.0, i1, jnp.where(lane == 1.0, i2, jnp.where(lane == 2.0, g1, jnp.where(lane == 3.0, g2, 0.0))))


def _outproj_kernel(a_ref, b_ref, w_ref, x_ref, gate_ref, g2_ref, sh_ref, sc_ref, *rest, has_router):
    if has_router:
        wr_ref, xo_ref, h_ref, info_ref, wb = rest
    else:
        xo_ref, h_ref, wb = rest
    _cast_weights_once(w_ref, wb)
    half = a_ref.shape[1]
    acc = _dot(a_ref[...], wb[:half, :]) + _dot(b_ref[...], wb[half:, :])
    xn = x_ref[...] + gate_ref[0] * acc
    xo_ref[...] = xn
    h = _norm_mod(xn, g2_ref[...], sh_ref[0], sc_ref[0])
    h_ref[...] = h.astype(h_ref.dtype)
    if has_router:
        info_ref[...] = _route(h, wr_ref[...])


def _outproj(a, a_blk, b, b_blk, w, x, g2, mod3, layer, rows_per_batch, is_sample, w_router=None, tm=1024):
    m = x.shape[0]
    tm = min(tm, m)
    half = w.shape[0] // 2
    has_router = w_router is not None
    kern = functools.partial(_outproj_kernel, has_router=has_router)
    row = pl.BlockSpec((tm, D), lambda i: (i, 0))
    in_specs = [pl.BlockSpec((tm, half), lambda i: (i, a_blk)),
                pl.BlockSpec((tm, half), lambda i: (i, b_blk)),
                pl.BlockSpec(w.shape, lambda i: (0, 0)),
                row,
                _mod_spec(layer, 2, tm, rows_per_batch, is_sample),
                pl.BlockSpec((1, D), lambda i: (0, 0)),
                _mod_spec(layer, 3, tm, rows_per_batch, is_sample),
                _mod_spec(layer, 4, tm, rows_per_batch, is_sample)]
    args = [a, b, w, x, mod3, g2, mod3, mod3]
    out_specs = [row, row]
    out_shape = [jax.ShapeDtypeStruct((m, D), f32), jax.ShapeDtypeStruct((m, D), f32 if has_router else bf16)]
    if has_router:
        in_specs.append(pl.BlockSpec((N_EXPERTS, D), lambda i: (0, 0)))
        args.append(w_router)
        out_specs.append(pl.BlockSpec((tm, LANES), lambda i: (i, 0)))
        out_shape.append(jax.ShapeDtypeStruct((m, LANES), f32))
    return pl.pallas_call(
        kern, grid=(m // tm,), in_specs=in_specs, out_specs=out_specs, out_shape=out_shape,
        scratch_shapes=[pltpu.VMEM(w.shape, bf16)],
        compiler_params=_cparams(("arbitrary",)), name="out_proj",
    )(*args)


def _rms(x, g):
    return x * lax.rsqrt(jnp.mean(x * x, axis=-1, keepdims=True) + EPS) * g


FFN_TM = 512
FFN_TF = 256


def _swiglu(h, wgu_ref, wd_ref, act):
    f = wd_ref.shape[0]
    for c0 in range(0, f, FFN_TF):
        g = _dot(h, wgu_ref[:, c0:c0 + FFN_TF])
        u = _dot(h, wgu_ref[:, f + c0:f + c0 + FFN_TF])
        act[:, c0:c0 + FFN_TF] = (g * jax.nn.sigmoid(g) * u).astype(bf16)
    return _dot(act[...], wd_ref[...])


def _ffn_kernel(*refs, final_norm):
    h_ref, wgu_ref, wd_ref, x_ref, gate_ref = refs[:5]
    rest = list(refs[5:])
    fg_ref = rest.pop(0) if final_norm else None
    o_ref, act = rest
    out = x_ref[...] + gate_ref[0] * _swiglu(h_ref[...], wgu_ref, wd_ref, act)
    o_ref[...] = _rms(out, fg_ref[...]) if final_norm else out


def _ffn(h, w_gu, w_down, x, mod3, layer, rows_per_batch, is_sample, final_g=None, tm=FFN_TM):
    m = x.shape[0]
    tm = min(tm, m)
    f = w_down.shape[0]
    final_norm = final_g is not None
    kern = functools.partial(_ffn_kernel, final_norm=final_norm)
    row = pl.BlockSpec((tm, D), lambda i: (i, 0))
    in_specs = [row,
                pl.BlockSpec((D, 2 * f), lambda i: (0, 0)),
                pl.BlockSpec((f, D), lambda i: (0, 0)),
                row,
                _mod_spec(layer, 5, tm, rows_per_batch, is_sample)]
    args = [h, w_gu, w_down, x, mod3]
    if final_norm:
        in_specs.append(pl.BlockSpec((1, D), lambda i: (0, 0)))
        args.append(final_g)
    return pl.pallas_call(
        kern, grid=(m // tm,), in_specs=in_specs, out_specs=row,
        out_shape=jax.ShapeDtypeStruct((m, D), f32),
        scratch_shapes=[pltpu.VMEM((tm, f), bf16)],
        compiler_params=_cparams(("arbitrary",)), name="ffn",
    )(*args)


MOE_TM = FFN_TM
DMA_UNROLL = 8


def _moe_pos_kernel(info_ref, pos_ref, tot_ref, carry, *, tb):
    ph = pl.program_id(0)
    i = pl.program_id(1)
    lane = lax.broadcasted_iota(jnp.int32, (1, LANES), 1).astype(f32)
    info = info_ref[...]
    oh1 = (lane == info[:, 0:1]).astype(f32)
    oh2 = (lane == info[:, 1:2]).astype(f32)
    cnt = oh1 + oh2

    @pl.when((ph == 0) & (i == 0))
    def _():
        carry[...] = jnp.zeros_like(carry)

    @pl.when(ph == 0)
    def _():
        carry[0:1, :] += cnt.sum(axis=0, keepdims=True)

    @pl.when((ph == 1) & (i == 0))
    def _():
        tot = carry[0:1, :]
        tot_ref[...] = jnp.broadcast_to(tot, tot_ref.shape)
        padded = jnp.floor((tot + (MOE_TM - 1)) * (1.0 / MOE_TM)) * MOE_TM
        start = jnp.zeros_like(tot)
        for e in range(N_EXPERTS - 1):
            pe = jnp.where(lane == e, padded, 0.0).sum(axis=-1, keepdims=True)
            start = start + jnp.where(lane > e, pe, 0.0)
        carry[1:2, :] = start

    @pl.when(ph == 1)
    def _():
        r = lax.broadcasted_iota(jnp.int32, (tb, tb), 0)
        c = lax.broadcasted_iota(jnp.int32, (tb, tb), 1)
        tri = jnp.where(r > c, 1.0, 0.0).astype(bf16)
        base = carry[1:2, :] + _dot(tri, cnt.astype(bf16))
        p1 = (oh1 * base).sum(axis=-1, keepdims=True)
        p2 = (oh2 * base).sum(axis=-1, keepdims=True)
        pos_ref[...] = jnp.where(lane == 0.0, p1, jnp.where(lane == 1.0, p2, 0.0)).astype(jnp.int32)
        carry[1:2, :] += cnt.sum(axis=0, keepdims=True)


def _moe_positions(info, tb=1024):
    m = info.shape[0]
    tb = min(tb, m)
    nblk = m // tb
    pos, tot = pl.pallas_call(
        functools.partial(_moe_pos_kernel, tb=tb),
        grid=(2, nblk),
        in_specs=[pl.BlockSpec((tb, LANES), lambda p, i: (i, 0))],
        out_specs=[pl.BlockSpec((tb, LANES), lambda p, i: (i * p, 0)),
                   pl.BlockSpec((8, LANES), lambda p, i: (0, 0))],
        out_shape=[jax.ShapeDtypeStruct((m, LANES), jnp.int32), jax.ShapeDtypeStruct((8, LANES), f32)],
        scratch_shapes=[pltpu.VMEM((8, LANES), f32)],
        compiler_params=_cparams(("arbitrary", "arbitrary")), name="moe_pos",
    )(info)
    return pos[:, :2].reshape(-1), tot[0, :N_EXPERTS].astype(jnp.int32)


def _row_copy(src, s_row, dst, d_row, sem):
    return pltpu.make_async_copy(src.at[pl.ds(s_row, 1)], dst.at[pl.ds(d_row, 1)], sem)


def _moe_dispatch_kernel(pos_ref, zrow_ref, zflag_ref, hp_ref, hs_ref, o_ref, zeros, sem, zsem, *, nbp, tb):
    i = pl.program_id(0)

    @pl.when(i == 0)
    def _():
        zeros[...] = jnp.zeros_like(zeros)

        def zero_tile(z):
            dst = o_ref.at[pl.ds(pl.multiple_of(zrow_ref[z], MOE_TM), MOE_TM)]
            return pltpu.make_async_copy(zeros, dst, zsem)

        for z in range(zrow_ref.shape[0]):
            @pl.when(zflag_ref[z] != 0)
            def _():
                zero_tile(z).start()
        for z in range(zrow_ref.shape[0]):
            @pl.when(zflag_ref[z] != 0)
            def _():
                zero_tile(z).wait()

    def scatter(src):
        def issue(r, _):
            g = 2 * (i * tb + r)
            _row_copy(src, r, o_ref, pos_ref[g], sem).start()
            _row_copy(src, r, o_ref, pos_ref[g + 1], sem).start(priority=1)
            return 0
        lax.fori_loop(0, tb, issue, 0, unroll=DMA_UNROLL)

        def drain(r, _):
            _row_copy(src, 0, o_ref, 0, sem).wait()
            _row_copy(src, 0, o_ref, 0, sem).wait()
            return 0
        lax.fori_loop(0, tb, drain, 0, unroll=DMA_UNROLL)

    @pl.when(i < nbp)
    def _():
        scatter(hp_ref)

    @pl.when(i >= nbp)
    def _():
        scatter(hs_ref)


def _moe_dispatch(pos, zrow, zflag, hp, hs, n_rows, tb=1024):
    mp, ms = hp.shape[0], hs.shape[0]
    tb = min(tb, mp, ms)
    nbp, nbs = mp // tb, ms // tb
    kern = functools.partial(_moe_dispatch_kernel, nbp=nbp, tb=tb)
    return pl.pallas_call(
        kern,
        grid_spec=pltpu.PrefetchScalarGridSpec(
            num_scalar_prefetch=3, grid=(nbp + nbs,),
            in_specs=[pl.BlockSpec((tb, D), lambda i, *_: (jnp.minimum(i, nbp - 1), 0)),
                      pl.BlockSpec((tb, D), lambda i, *_: (jnp.maximum(i - nbp, 0), 0))],
            out_specs=pl.BlockSpec(memory_space=pl.ANY),
            scratch_shapes=[pltpu.VMEM((MOE_TM, D), f32), pltpu.SemaphoreType.DMA(()),
                            pltpu.SemaphoreType.DMA(())]),
        out_shape=jax.ShapeDtypeStruct((n_rows, D), f32),
        compiler_params=_cparams(("arbitrary",)), name="moe_dispatch",
    )(pos, zrow, zflag, hp, hs)


def _moe_ffn_kernel(te_ref, na_ref, h_ref, wgu_ref, wd_ref, o_ref, act):
    i = pl.program_id(0)

    @pl.when(i < na_ref[0])
    def _():
        o_ref[...] = _swiglu(h_ref[...].astype(bf16), wgu_ref.at[0], wd_ref.at[0], act)

    @pl.when(i >= na_ref[0])
    def _():
        o_ref[...] = jnp.zeros_like(o_ref)


def _moe_ffn(h_sorted, w_gu, w_down, tile_expert, n_active):
    n_rows = h_sorted.shape[0]
    tm = MOE_TM
    f = w_down.shape[1]
    n_tiles = n_rows // tm
    return pl.pallas_call(
        _moe_ffn_kernel,
        grid_spec=pltpu.PrefetchScalarGridSpec(
            num_scalar_prefetch=2, grid=(n_tiles,),
            in_specs=[pl.BlockSpec((tm, D), lambda i, te, na: (jnp.minimum(i, na[0] - 1), 0)),
                      pl.BlockSpec((1, D, 2 * f), lambda i, te, na: (te[i], 0, 0)),
                      pl.BlockSpec((1, f, D), lambda i, te, na: (te[i], 0, 0))],
            out_specs=pl.BlockSpec((tm, D), lambda i, te, na: (i, 0)),
            scratch_shapes=[pltpu.VMEM((tm, f), bf16)]),
        out_shape=jax.ShapeDtypeStruct((n_rows, D), f32),
        compiler_params=_cparams(("arbitrary",)), name="moe_ffn",
    )(tile_expert, n_active, h_sorted, w_gu, w_down)


def _moe_combine_kernel(pos_ref, info_ref, x_ref, gate_ref, fg_ref, y_ref, o_ref, ybuf, sem, *, tb, base,
                        final_norm):
    i = pl.program_id(0)
    n = pl.num_programs(0)

    def issue(blk, slot):
        def body(r, _):
            g = 2 * (base + blk * tb + r)
            _row_copy(y_ref, pos_ref[g], ybuf.at[slot, 0], r, sem.at[slot]).start()
            _row_copy(y_ref, pos_ref[g + 1], ybuf.at[slot, 1], r, sem.at[slot]).start(priority=1)
            return 0
        lax.fori_loop(0, tb, body, 0, unroll=DMA_UNROLL)

    @pl.when(i == 0)
    def _():
        issue(0, 0)

    @pl.when(i + 1 < n)
    def _():
        issue(i + 1, (i + 1) % 2)

    slot = i % 2

    def drain(r, _):
        _row_copy(y_ref, 0, ybuf.at[slot, 0], 0, sem.at[slot]).wait()
        _row_copy(y_ref, 0, ybuf.at[slot, 1], 0, sem.at[slot]).wait()
        return 0
    lax.fori_loop(0, tb, drain, 0, unroll=DMA_UNROLL)

    info = info_ref[...]
    y = info[:, 2:3] * ybuf[slot, 0] + info[:, 3:4] * ybuf[slot, 1]
    out = x_ref[...] + gate_ref[0] * y
    o_ref[...] = _rms(out, fg_ref[...]) if final_norm else out


def _moe_combine(pos, info, x, mod3, layer, y_sorted, base, rows_per_batch, is_sample, final_g, tb=512):
    m = x.shape[0]
    tb = min(tb, m)
    final_norm = final_g is not None
    if final_g is None:
        final_g = jnp.ones((1, D), f32)
    kern = functools.partial(_moe_combine_kernel, tb=tb, base=base, final_norm=final_norm)
    return pl.pallas_call(
        kern,
        grid_spec=pltpu.PrefetchScalarGridSpec(
            num_scalar_prefetch=1, grid=(m // tb,),
            in_specs=[pl.BlockSpec((tb, LANES), lambda i, p: (i, 0)),
                      pl.BlockSpec((tb, D), lambda i, p: (i, 0)),
                      _mod_spec(layer, 5, tb, rows_per_batch, is_sample),
                      pl.BlockSpec((1, D), lambda i, p: (0, 0)),
                      pl.BlockSpec(memory_space=pl.ANY)],
            out_specs=pl.BlockSpec((tb, D), lambda i, p: (i, 0)),
            scratch_shapes=[pltpu.VMEM((2, 2, tb, D), f32), pltpu.SemaphoreType.DMA((2,))]),
        out_shape=jax.ShapeDtypeStruct((m, D), f32),
        compiler_params=_cparams(("arbitrary",)), name="moe_combine",
    )(pos, info, x, mod3, final_g, y_sorted)


def _moe(hp, hs, info_p, info_s, xp, xs, w_gu, w_down, mod3, layer, tp, ts, final_g):
    mp, ms = hp.shape[0], hs.shape[0]
    tm = MOE_TM
    n_tiles = (2 * (mp + ms)) // tm + N_EXPERTS
    pos, counts = _moe_positions(jnp.concatenate([info_p, info_s], axis=0))
    tiles_per = (counts + tm - 1) // tm
    cum = jnp.cumsum(tiles_per)
    n_active = cum[-1:].astype(jnp.int32)
    ids = jnp.arange(n_tiles, dtype=jnp.int32)
    te = jnp.minimum((ids[:, None] >= cum[None, :]).astype(jnp.int32).sum(axis=1), N_EXPERTS - 1)
    last_e = te[jnp.maximum(n_active[0] - 1, 0)]
    te = jnp.where(ids < n_active[0], te, last_e)
    tail = jnp.arange(n_tiles - N_EXPERTS, n_tiles, dtype=jnp.int32)
    zrow = jnp.concatenate([jnp.maximum(cum - 1, 0), tail]).astype(jnp.int32) * tm
    zflag = jnp.concatenate([counts % tm != 0, tail >= n_active[0]]).astype(jnp.int32)
    h_sorted = _moe_dispatch(pos, zrow, zflag, hp, hs, n_tiles * tm)
    y_sorted = _moe_ffn(h_sorted, w_gu, w_down, te, n_active)
    yp = _moe_combine(pos, info_p, xp, mod3, layer, y_sorted, 0, tp, False, final_g)
    ys = _moe_combine(pos, info_s, xs, mod3, layer, y_sorted, mp, ts, True, final_g)
    return yp, ys


def kernel(x_prompt, x_sample, cache_a_k, cache_a_v, state_ret_fwd, state_ret_bwd, cache_c_k, cache_c_v, c, c_ctx,
           mod_w, mod_b, norm1_g, norm2_g, final_g, ev_w_in, ev_w_out, ev_sink, ev_ret_logit_fwd, ev_ret_logit_bwd,
           ev_ffn_w_gu, ev_ffn_w_down, od_w_qkv, od_w_out, od_rpb, od_w_router, od_moe_w_gu, od_moe_w_down):
    bp, tp, _ = x_prompt.shape
    bs, ts, _ = x_sample.shape
    depth = mod_w.shape[0]
    assert 1 + bs <= MOD_ROWS
    xp = x_prompt.reshape(bp * tp, D)
    xs = x_sample.reshape(bs * ts, D)

    cvecs = jnp.zeros((MOD_ROWS, D), f32).at[0].set(c_ctx).at[1:1 + bs].set(c)
    mod3 = _adaln(cvecs, mod_w, mod_b)

    a_q = A_HEADS * HEAD_DIM
    a_kv = A_KV_HEADS * HEAD_DIM
    b_w = B_HEADS * HEAD_DIM
    c_w = C_HEADS * HEAD_DIM
    streams = ((False, tp), (True, ts))
    outs = {}
    for i in range(depth):
        j = i // 2
        g1 = norm1_g[i].reshape(1, D)
        g2 = norm2_g[i].reshape(1, D)
        last = i == depth - 1
        fg = final_g.reshape(1, D) if last else None
        if i % 2 == 0:
            w_in = ev_w_in[j]
            w_out = ev_w_out[j]
            splits = (a_q, a_kv, a_kv, b_w, b_w, b_w, b_w)
            dts_p = (bf16, f32, f32, f32, f32, bf16, f32)
            dts_s = (f32, f32, f32, f32, f32, bf16, f32)
            tables = _ret_tables(ev_ret_logit_fwd[j], ev_ret_logit_bwd[j])
            qa_p, ka_p, va_p, qr_p, kr_p, vr_p, gr_p = _proj(xp, g1, mod3, i, w_in, splits, dts_p, tp, False)
            qa_s, ka_s, va_s, qr_s, kr_s, vr_s, gr_s = _proj(xs, g1, mod3, i, w_in, splits, dts_s, ts, True)
            oa_p = _ctx_attn(qa_p, ka_p, va_p, ev_sink[j], bp, A_HEADS, A_KV_HEADS)
            zero = jnp.zeros((bp, B_HEADS, HEAD_DIM, HEAD_DIM), f32)
            or_p, sf, sb = _retention(qr_p, kr_p, vr_p, gr_p, zero, zero, tables, bp)
            lc = cache_a_k.shape[2]
            oa_s = _win_attn(qa_s, ka_s, va_s, cache_a_k[:, j].reshape(bs, lc, a_kv),
                             cache_a_v[:, j].reshape(bs, lc, a_kv), ev_sink[j], bs)
            or_s, _, _ = _retention(qr_s, kr_s, vr_s, gr_s, state_ret_fwd[:, j], state_ret_bwd[:, j], tables, bs)
            outs.setdefault('a_k', []).append(ka_p.reshape(bp, tp, A_KV_HEADS, HEAD_DIM))
            outs.setdefault('a_v', []).append(va_p.reshape(bp, tp, A_KV_HEADS, HEAD_DIM))
            outs.setdefault('r_f', []).append(sf)
            outs.setdefault('r_b', []).append(sb)
            xp, hp = _outproj(oa_p, 0, or_p, 0, w_out, xp, g2, mod3, i, tp, False)
            xs, hs = _outproj(oa_s, 0, or_s, 0, w_out, xs, g2, mod3, i, ts, True)
            w_gu = ev_ffn_w_gu[j].astype(bf16)
            w_dn = ev_ffn_w_down[j].astype(bf16)
            xp = _ffn(hp, w_gu, w_dn, xp, mod3, i, tp, False, final_g=fg)
            xs = _ffn(hs, w_gu, w_dn, xs, mod3, i, ts, True, final_g=fg)
        else:
            w_qkv = od_w_qkv[j]
            w_out = od_w_out[j]
            w_router = od_w_router[j].astype(f32).T
            splits = (c_w, c_w, c_w)
            q_p, k_p, v_p = _proj(xp, g1, mod3, i, w_qkv, splits, (bf16, f32, f32), tp, False)
            q_s, k_s, v_s = _proj(xs, g1, mod3, i, w_qkv, splits, (bf16, bf16, bf16), ts, True)
            o_p = _ctx_attn(q_p, k_p, v_p, None, bp, C_HEADS, C_HEADS)
            lc = cache_c_k.shape[2]
            w_gu, w_dn = od_moe_w_gu[j], od_moe_w_down[j]
            o_s, (w_gu_b, w_dn_b) = _nb_attn(
                q_s, k_s, v_s, cache_c_k[:, j].reshape(bs, lc, c_w), cache_c_v[:, j].reshape(bs, lc, c_w),
                od_rpb[j], bs, cast=(w_gu.reshape(-1, w_gu.shape[-1]), w_dn.reshape(-1, w_dn.shape[-1])))
            outs.setdefault('c_k', []).append(k_p.reshape(bp, tp, C_HEADS, HEAD_DIM))
            outs.setdefault('c_v', []).append(v_p.reshape(bp, tp, C_HEADS, HEAD_DIM))
            xp, hp, info_p = _outproj(o_p, 0, o_p, 1, w_out, xp, g2, mod3, i, tp, False, w_router=w_router)
            xs, hs, info_s = _outproj(o_s, 0, o_s, 1, w_out, xs, g2, mod3, i, ts, True, w_router=w_router)
            xp, xs = _moe(hp, hs, info_p, info_s, xp, xs, w_gu_b.reshape(w_gu.shape), w_dn_b.reshape(w_dn.shape),
                          mod3, i, tp, ts, fg)
    y_prompt = xp.reshape(bp, tp, D)
    y_sample = xs.reshape(bs, ts, D)
    return (y_prompt, y_sample,
            jnp.stack(outs['a_k'], axis=1), jnp.stack(outs['a_v'], axis=1),
            jnp.stack(outs['r_f'], axis=1), jnp.stack(outs['r_b'], axis=1),
            jnp.stack(outs['c_k'], axis=1), jnp.stack(outs['c_v'], axis=1))
```

```python
import functools

import numpy as np
import jax
import jax.numpy as jnp
from jax import lax
from jax.experimental import pallas as pl
from jax.experimental.pallas import tpu as pltpu

f32 = jnp.float32
bf16 = jnp.bfloat16

D = 1024
HEAD_DIM = 64
GRID_W = 64
ROPE_BASE = 10000.0
A_HEADS = 8
A_KV_HEADS = 2
WINDOW = 128
B_HEADS = 8
RET_CHUNK = 128
C_HEADS = 16
NB_ROWS = 8
NB_COLS = 16
N_EXPERTS = 8
EPS = 1e-6
NEG = -1e30

LANES = 128
MOD_ROWS = 16
VMEM_LIMIT_MB = 56


def _cparams(sem, vmem_mb=VMEM_LIMIT_MB):
    return pltpu.CompilerParams(dimension_semantics=sem, vmem_limit_bytes=vmem_mb * 1024 * 1024)


def _dot(a, b):
    return jnp.dot(a, b, preferred_element_type=f32)


def _dot_nt(a, b):
    return lax.dot_general(a, b, (((1,), (1,)), ((), ())), preferred_element_type=f32)


def _dot_tn(a, b):
    return lax.dot_general(a, b, (((0,), (0,)), ((), ())), preferred_element_type=f32)


def _lane_lo():
    return lax.broadcasted_iota(jnp.int32, (1, LANES), 1) < HEAD_DIM


def _swap_halves(x):
    return pltpu.roll(x, HEAD_DIM, 1)


def _mod_imap(layer, which, tm, rows_per_batch, is_sample):
    def imap(i, *_):
        r = (1 + (i * tm) // rows_per_batch) if is_sample else 0
        return ((layer * MOD_ROWS + r) * 6 + which, 0, 0)

    return imap


def _mod_spec(layer, which, tm, rows_per_batch, is_sample):
    return pl.BlockSpec((1, 1, D), _mod_imap(layer, which, tm, rows_per_batch, is_sample))


def _adaln_kernel(cv_ref, w_ref, b_ref, o_ref):
    cv = cv_ref[...]
    s = (cv * jax.nn.sigmoid(cv)).astype(bf16)
    o_ref[0] = _dot(s, w_ref[0].astype(bf16)) + b_ref[0]


def _adaln(cvecs, mod_w, mod_b):
    depth = mod_w.shape[0]
    tn = 1024
    out = pl.pallas_call(
        _adaln_kernel,
        grid=(depth, 6 * D // tn),
        in_specs=[pl.BlockSpec((MOD_ROWS, D), lambda l, j: (0, 0)),
                  pl.BlockSpec((1, D, tn), lambda l, j: (l, 0, j)),
                  pl.BlockSpec((1, 1, tn), lambda l, j: (l, 0, j))],
        out_specs=pl.BlockSpec((1, MOD_ROWS, tn), lambda l, j: (l, 0, j)),
        out_shape=jax.ShapeDtypeStruct((depth, MOD_ROWS, 6 * D), f32),
        compiler_params=_cparams(("arbitrary", "arbitrary")),
        name="adaln",
    )(cvecs, mod_w, mod_b.reshape(depth, 1, 6 * D))
    return out.reshape(depth * MOD_ROWS * 6, 1, D)


def _norm_mod(x, g, shift, scale):
    y = x * lax.rsqrt(jnp.mean(x * x, axis=-1, keepdims=True) + EPS) * g
    return y * (1.0 + scale) + shift


def _cast_weights_once(w_ref, wb):
    @pl.when(pl.program_id(0) == 0)
    def _():
        wb[...] = w_ref[...].astype(bf16)


def _ride_along_cast(srcs, dsts):
    for src, dst in zip(srcs, dsts):
        dst[...] = src[...].astype(bf16)


def _cast_specs(cast, steps, imap):
    specs = []
    for w in cast:
        rows = w.shape[0] // steps
        assert rows * steps == w.shape[0] and rows % 16 == 0, (w.shape, steps)
        specs.append(pl.BlockSpec((rows, w.shape[1]), imap))
    return specs, [jax.ShapeDtypeStruct(w.shape, bf16) for w in cast]


def _proj_kernel(x_ref, g_ref, sh_ref, sc_ref, w_ref, *rest, splits, chunk, n_cast):
    cast_in, rest = rest[:n_cast], rest[n_cast:]
    out_refs, cast_out, wb = rest[:len(splits)], rest[len(splits):-1], rest[-1]
    _ride_along_cast(cast_in, cast_out)
    _cast_weights_once(w_ref, wb)
    h = _norm_mod(x_ref[...], g_ref[...], sh_ref[0], sc_ref[0]).astype(bf16)
    off = 0
    for o_ref, width in zip(out_refs, splits):
        for c0 in range(0, width, chunk):
            cw = min(chunk, width - c0)
            o_ref[:, c0:c0 + cw] = _dot(h, wb[:, off + c0:off + c0 + cw]).astype(o_ref.dtype)
        off += width


def _proj(x, g, mod3, layer, w, splits, dtypes, rows_per_batch, is_sample, cast=(), tm=512):
    m = x.shape[0]
    tm = min(tm, m)
    n = w.shape[1]
    kern = functools.partial(_proj_kernel, splits=tuple(splits), chunk=512, n_cast=len(cast))
    cast_specs, cast_shapes = _cast_specs(cast, m // tm, lambda i: (i, 0))
    return pl.pallas_call(
        kern,
        grid=(m // tm,),
        in_specs=[pl.BlockSpec((tm, D), lambda i: (i, 0)),
                  pl.BlockSpec((1, D), lambda i: (0, 0)),
                  _mod_spec(layer, 0, tm, rows_per_batch, is_sample),
                  _mod_spec(layer, 1, tm, rows_per_batch, is_sample),
                  pl.BlockSpec((D, n), lambda i: (0, 0))] + cast_specs,
        out_specs=[pl.BlockSpec((tm, s), lambda i: (i, 0)) for s in splits] + cast_specs,
        out_shape=[jax.ShapeDtypeStruct((m, s), dt) for s, dt in zip(splits, dtypes)] + cast_shapes,
        scratch_shapes=[pltpu.VMEM((D, n), bf16)],
        compiler_params=_cparams(("arbitrary",)),
        name="norm_proj",
    )(x, g, mod3, mod3, w, *cast)


def _softmax_parts(parts, sink):
    m = parts[0].max(axis=-1, keepdims=True)
    for s in parts[1:]:
        m = jnp.maximum(m, s.max(axis=-1, keepdims=True))
    if sink is not None:
        m = jnp.maximum(m, sink)
    es = [jnp.exp(s - m) for s in parts]
    den = es[0].sum(axis=-1, keepdims=True)
    for e in es[1:]:
        den = den + e.sum(axis=-1, keepdims=True)
    if sink is not None:
        den = den + jnp.exp(sink - m)
    return [e.astype(bf16) for e in es], 1.0 / den


def _ctx_attn_kernel(sink_ref, q_ref, k_ref, v_ref, o_ref, *, heads, kv_heads, has_sink):
    rep = heads // kv_heads
    lo = _lane_lo()
    scale = HEAD_DIM ** -0.5
    for pi in range(heads // 2):
        qp = q_ref[:, pi * LANES:(pi + 1) * LANES].astype(f32) * scale
        outs = []
        for hh in range(2):
            h = 2 * pi + hh
            g = h // rep
            kp, kh = g // 2, g % 2
            qv = (qp if kh == hh else _swap_halves(qp)).astype(bf16)
            kv = k_ref[:, kp * LANES:(kp + 1) * LANES]
            km = (jnp.where(lo, kv, 0.0) if kh == 0 else jnp.where(lo, 0.0, kv)).astype(bf16)
            (p,), inv = _softmax_parts([_dot_nt(qv, km)], sink_ref[h] if has_sink else None)
            o = _dot(p, v_ref[:, kp * LANES:(kp + 1) * LANES].astype(bf16)) * inv
            outs.append(o if kh == hh else _swap_halves(o))
        o_ref[:, pi * LANES:(pi + 1) * LANES] = jnp.where(lo, outs[0], outs[1]).astype(o_ref.dtype)


def _ctx_attn(q, k, v, sink, batch, heads, kv_heads):
    m = q.shape[0]
    t = m // batch
    has_sink = sink is not None
    if sink is None:
        sink = jnp.zeros((heads,), f32)
    kern = functools.partial(_ctx_attn_kernel, heads=heads, kv_heads=kv_heads, has_sink=has_sink)
    return pl.pallas_call(
        kern,
        grid=(batch,),
        in_specs=[pl.BlockSpec(memory_space=pltpu.SMEM),
                  pl.BlockSpec((t, heads * HEAD_DIM), lambda b: (b, 0)),
                  pl.BlockSpec((t, kv_heads * HEAD_DIM), lambda b: (b, 0)),
                  pl.BlockSpec((t, kv_heads * HEAD_DIM), lambda b: (b, 0))],
        out_specs=pl.BlockSpec((t, heads * HEAD_DIM), lambda b: (b, 0)),
        out_shape=jax.ShapeDtypeStruct((m, heads * HEAD_DIM), bf16),
        compiler_params=_cparams(("arbitrary",)),
        name="ctx_attn",
    )(sink.astype(f32), q, k, v)


def _rope(x, cos, sin_signed):
    lane = lax.broadcasted_iota(jnp.int32, (1, LANES), 1)
    first = (lane % 32) < 16
    rot = jnp.where(first, pltpu.roll(x, LANES - 16, 1), pltpu.roll(x, 16, 1))
    return x * cos + rot * sin_signed


def _win_attn_kernel(sink_ref, q_ref, k_ref, v_ref, ck_ref, cv_ref, cosq_ref, sinq_ref, cosk_ref, sink_k_ref,
                     o_ref, klm, vl, ckm, cvb, *, qblk, t_lat):
    n = pl.program_id(1)
    lo = _lane_lo()
    span = 3 * qblk

    @pl.when(n == 0)
    def _():
        kr = _rope(k_ref[...], cosk_ref[...], sink_k_ref[...])
        klm[0] = jnp.where(lo, kr, 0.0).astype(bf16)
        klm[1] = jnp.where(lo, 0.0, kr).astype(bf16)
        vl[...] = v_ref[...].astype(bf16)
        c = ck_ref[0]
        ckm[0] = jnp.where(lo, c, 0.0).astype(bf16)
        ckm[1] = jnp.where(lo, 0.0, c).astype(bf16)
        cvb[...] = cv_ref[0].astype(bf16)

    start = pl.multiple_of(jnp.clip(n * qblk - qblk, 0, t_lat - span), qblk)
    rep = A_HEADS // A_KV_HEADS
    qpos = n * qblk + lax.broadcasted_iota(jnp.int32, (rep * qblk, 1), 0) % qblk
    kpos = start + lax.broadcasted_iota(jnp.int32, (1, span), 1)
    valid = jnp.abs(kpos - qpos) <= WINDOW
    scale = HEAD_DIM ** -0.5
    cq = cosq_ref[...]
    sq = sinq_ref[...]
    roped = [_rope(q_ref[:, pi * LANES:(pi + 1) * LANES], cq, sq) * scale for pi in range(A_HEADS // 2)]
    outs = [None] * A_HEADS
    for g in range(A_KV_HEADS):
        group = range(g * rep, (g + 1) * rep)
        qst = jnp.concatenate([roped[h // 2] if h % 2 == g else _swap_halves(roped[h // 2]) for h in group],
                              axis=0).astype(bf16)
        sink = jnp.concatenate([jnp.full((qblk, 1), sink_ref[h], f32) for h in group], axis=0)
        s_c = _dot_nt(qst, ckm[g])
        s_l = jnp.where(valid, _dot_nt(qst, klm[g, pl.ds(start, span), :]), NEG)
        (p_c, p_l), inv = _softmax_parts([s_c, s_l], sink)
        o = (_dot(p_c, cvb[...]) + _dot(p_l, vl[pl.ds(start, span), :])) * inv
        for idx, h in enumerate(group):
            oh = o[idx * qblk:(idx + 1) * qblk]
            outs[h] = oh if h % 2 == g else _swap_halves(oh)
    for pi in range(A_HEADS // 2):
        o_ref[:, pi * LANES:(pi + 1) * LANES] = jnp.where(lo, outs[2 * pi], outs[2 * pi + 1]).astype(o_ref.dtype)


def _rope_tables(t):
    half = HEAD_DIM // 2
    nf = half // 2
    pos = jnp.arange(t)
    row = (pos // GRID_W).astype(f32)
    col = (pos % GRID_W).astype(f32)
    inv = ROPE_BASE ** (-jnp.arange(nf, dtype=f32) / nf)
    ang_r = row[:, None] * inv
    ang_c = col[:, None] * inv
    ang = jnp.concatenate([ang_r, ang_r, ang_c, ang_c], axis=-1)
    sign = jnp.concatenate([-jnp.ones((nf,), f32), jnp.ones((nf,), f32)] * 2)
    cos = jnp.cos(ang)
    sin_signed = jnp.sin(ang) * sign
    return jnp.tile(cos, (1, 2)), jnp.tile(sin_signed, (1, 2))


def _win_attn(q, k, v, ck, cv, sink, batch):
    m = q.shape[0]
    t = m // batch
    qblk = 128
    nb = t // qblk
    lc = ck.shape[1]
    cos, sin_s = _rope_tables(t)
    kern = functools.partial(_win_attn_kernel, qblk=qblk, t_lat=t)
    return pl.pallas_call(
        kern,
        grid=(batch, nb),
        in_specs=[pl.BlockSpec(memory_space=pltpu.SMEM),
                  pl.BlockSpec((qblk, A_HEADS * HEAD_DIM), lambda b, n: (b * nb + n, 0)),
                  pl.BlockSpec((t, LANES), lambda b, n: (b, 0)),
                  pl.BlockSpec((t, LANES), lambda b, n: (b, 0)),
                  pl.BlockSpec((1, lc, LANES), lambda b, n: (b, 0, 0)),
                  pl.BlockSpec((1, lc, LANES), lambda b, n: (b, 0, 0)),
                  pl.BlockSpec((qblk, LANES), lambda b, n: (n, 0)),
                  pl.BlockSpec((qblk, LANES), lambda b, n: (n, 0)),
                  pl.BlockSpec((t, LANES), lambda b, n: (0, 0)),
                  pl.BlockSpec((t, LANES), lambda b, n: (0, 0))],
        out_specs=pl.BlockSpec((qblk, A_HEADS * HEAD_DIM), lambda b, n: (b * nb + n, 0)),
        out_shape=jax.ShapeDtypeStruct((m, A_HEADS * HEAD_DIM), bf16),
        scratch_shapes=[pltpu.VMEM((2, t, LANES), bf16), pltpu.VMEM((t, LANES), bf16),
                        pltpu.VMEM((2, lc, LANES), bf16), pltpu.VMEM((lc, LANES), bf16)],
        compiler_params=_cparams(("arbitrary", "arbitrary")),
        name="win_attn",
    )(sink.astype(f32), q, k, v, ck, cv, cos, sin_s, cos, sin_s)


def _nb_attn_kernel(q_ref, k_ref, v_ref, ck_ref, cv_ref, tl_ref, tr_ref, *rest, t_lat, qblk, n_cast):
    cast_in, cast_out, (o_ref, bias) = rest[:n_cast], rest[n_cast:2 * n_cast], rest[2 * n_cast:]
    _ride_along_cast(cast_in, cast_out)
    b = pl.program_id(1)
    lo = _lane_lo()
    rows = t_lat // GRID_W
    wr = min(NB_ROWS, rows)
    neg_slot = 2 * NB_ROWS - 1

    @pl.when(b == 0)
    def _():
        for hh in range(2):
            for rq in range(rows):
                k0 = min(max(rq - wr // 2, 0), rows - wr)
                for p in range(rows // 2):
                    idx = []
                    for rk in (2 * p, 2 * p + 1):
                        idx.append(rk - rq + NB_ROWS - 1 if k0 <= rk < k0 + wr else neg_slot)
                    bias[hh, rq * GRID_W:(rq + 1) * GRID_W, p * LANES:(p + 1) * LANES] = (
                        tl_ref[hh, idx[0]] + tr_ref[hh, idx[1]])

    kb = k_ref[...]
    km = [jnp.where(lo, kb, 0.0).astype(bf16), jnp.where(lo, 0.0, kb).astype(bf16)]
    vb = v_ref[...].astype(bf16)
    cb = ck_ref[0]
    ckm = [jnp.where(lo, cb, 0.0).astype(bf16), jnp.where(lo, 0.0, cb).astype(bf16)]
    cvb = cv_ref[0].astype(bf16)
    scale = HEAD_DIM ** -0.5
    for qb in range(t_lat // qblk):
        rs = slice(qb * qblk, (qb + 1) * qblk)
        r0, r1 = qb * qblk // GRID_W, ((qb + 1) * qblk - 1) // GRID_W
        k_lo = min(max(r0 - wr // 2, 0), rows - wr) * GRID_W
        k_hi = (min(max(r1 - wr // 2, 0), rows - wr) + wr) * GRID_W
        ks = slice(k_lo // LANES * LANES, -(-k_hi // LANES) * LANES)
        qv = (q_ref[rs, :] * scale).astype(bf16)
        outs = []
        for hh in range(2):
            s_c = _dot_nt(qv, ckm[hh])
            s_n = _dot_nt(qv, km[hh][ks]) + bias[hh, rs, ks]
            (p_c, p_n), inv = _softmax_parts([s_c, s_n], None)
            outs.append((_dot(p_c, cvb) + _dot(p_n, vb[ks])) * inv)
        o_ref[rs, :] = jnp.where(lo, outs[0], outs[1]).astype(o_ref.dtype)


def _nb_bias_tiles(rpb):
    heads = rpb.shape[0]
    cq = jnp.arange(GRID_W)
    cstart = jnp.clip(cq - NB_COLS // 2, 0, GRID_W - NB_COLS)
    col_ok = (cq[None, :] >= cstart[:, None]) & (cq[None, :] < cstart[:, None] + NB_COLS)
    dc = jnp.clip(cq[None, :] - cq[:, None], 1 - NB_COLS, NB_COLS - 1) + NB_COLS - 1
    onehot = jax.nn.one_hot(dc.reshape(-1), 2 * NB_COLS - 1, dtype=f32)
    tm = jnp.einsum('hab,kb->hak', rpb.astype(f32), onehot, precision=lax.Precision.HIGHEST)
    tm = tm.reshape(heads, 2 * NB_ROWS - 1, GRID_W, GRID_W)
    tm = jnp.where(col_ok[None, None], tm, NEG)
    tm = jnp.concatenate([tm, jnp.full((heads, 1, GRID_W, GRID_W), NEG, f32)], axis=1)
    z = jnp.zeros_like(tm)
    return jnp.concatenate([tm, z], axis=-1), jnp.concatenate([z, tm], axis=-1)


def _nb_attn(q, k, v, ck, cv, rpb, batch, cast=()):
    m = q.shape[0]
    t = m // batch
    lc = ck.shape[1]
    npairs = C_HEADS // 2
    steps = npairs * batch
    tl, tr = _nb_bias_tiles(rpb)
    kern = functools.partial(_nb_attn_kernel, t_lat=t, qblk=256, n_cast=len(cast))
    cast_specs, cast_shapes = _cast_specs(cast, steps, lambda p, b: (p * batch + b, 0))
    outs = pl.pallas_call(
        kern,
        grid=(npairs, batch),
        in_specs=[pl.BlockSpec((t, LANES), lambda p, b: (b, p)),
                  pl.BlockSpec((t, LANES), lambda p, b: (b, p)),
                  pl.BlockSpec((t, LANES), lambda p, b: (b, p)),
                  pl.BlockSpec((1, lc, LANES), lambda p, b: (b, 0, p)),
                  pl.BlockSpec((1, lc, LANES), lambda p, b: (b, 0, p)),
                  pl.BlockSpec((2, 2 * NB_ROWS, GRID_W, LANES), lambda p, b: (p, 0, 0, 0)),
                  pl.BlockSpec((2, 2 * NB_ROWS, GRID_W, LANES), lambda p, b: (p, 0, 0, 0))] + cast_specs,
        out_specs=cast_specs + [pl.BlockSpec((t, LANES), lambda p, b: (b, p))],
        out_shape=cast_shapes + [jax.ShapeDtypeStruct((m, C_HEADS * HEAD_DIM), bf16)],
        scratch_shapes=[pltpu.VMEM((2, t, t), f32)],
        compiler_params=_cparams(("arbitrary", "arbitrary")),
        name="nb_attn",
    )(q, k, v, ck, cv, tl, tr, *cast)
    return outs[-1], outs[:-1]


def _ret_kernel(q_ref, k_ref, v_ref, g_ref, s0f_ref, s0b_ref, qdf_ref, qdb_ref, kdf_ref, kdb_ref, dm_ref,
                cdf_ref, cdb_ref, bd_ref, o_ref, sf_ref, sb_ref, kvf, kvb, *, t):
    c = RET_CHUNK
    n = t // c
    lo = _lane_lo()
    bd = bd_ref[...]
    for pi in range(B_HEADS // 2):
        cols = slice(pi * LANES, (pi + 1) * LANES)
        qdf, qdb, kdf, kdb = qdf_ref[pi], qdb_ref[pi], kdf_ref[pi], kdb_ref[pi]
        cdf, cdb = cdf_ref[pi], cdb_ref[pi]
        for ci in range(n):
            rs = slice(ci * c, (ci + 1) * c)
            kc = k_ref[rs, cols] * (HEAD_DIM ** -0.5)
            vc = v_ref[rs, cols].astype(bf16)
            kvf[ci] = _dot_tn((kc * kdf).astype(bf16), vc) * bd
            kvb[ci] = _dot_tn((kc * kdb).astype(bf16), vc) * bd
        s = s0f_ref[0, pi]
        for ci in range(n):
            upd = kvf[ci]
            kvf[ci] = s
            s = s * cdf + upd
        sf_ref[0, pi] = s
        s = s0b_ref[0, pi]
        for ci in range(n - 1, -1, -1):
            upd = kvb[ci]
            kvb[ci] = s
            s = s * cdb + upd
        sb_ref[0, pi] = s
        for ci in range(n):
            rs = slice(ci * c, (ci + 1) * c)
            qc = q_ref[rs, cols]
            kc = k_ref[rs, cols] * (HEAD_DIM ** -0.5)
            vc = v_ref[rs, cols].astype(bf16)
            qb = qc.astype(bf16)
            a0 = _dot_nt(qb, jnp.where(lo, kc, 0.0).astype(bf16)) * dm_ref[2 * pi]
            a1 = _dot_nt(qb, jnp.where(lo, 0.0, kc).astype(bf16)) * dm_ref[2 * pi + 1]
            o = jnp.where(lo, _dot(a0.astype(bf16), vc), _dot(a1.astype(bf16), vc))
            o = o + _dot((qc * qdf).astype(bf16), kvf[ci].astype(bf16))
            o = o + _dot((qc * qdb).astype(bf16), kvb[ci].astype(bf16))
            inv_n = 1.0 / HEAD_DIM
            m0 = jnp.where(lo, o, 0.0).sum(axis=-1, keepdims=True) * inv_n
            m1 = jnp.where(lo, 0.0, o).sum(axis=-1, keepdims=True) * inv_n
            d = o - jnp.where(lo, m0, m1)
            d2 = d * d
            v0 = jnp.where(lo, d2, 0.0).sum(axis=-1, keepdims=True) * inv_n
            v1 = jnp.where(lo, 0.0, d2).sum(axis=-1, keepdims=True) * inv_n
            y = d * lax.rsqrt(jnp.where(lo, v0, v1) + EPS)
            gt = g_ref[rs, cols]
            o_ref[rs, cols] = (gt * jax.nn.sigmoid(gt) * y).astype(o_ref.dtype)


def _ret_tables(lg_f, lg_b):
    c = RET_CHUNK
    lf = jax.nn.log_sigmoid(lg_f.astype(f32))
    lb = jax.nn.log_sigmoid(lg_b.astype(f32))
    idx = jnp.arange(c, dtype=f32)

    def lanes(per_head):
        r = per_head.shape[1]
        x = jnp.repeat(per_head[:, :, None], HEAD_DIM, axis=2)
        x = x.reshape(B_HEADS // 2, 2, r, HEAD_DIM).transpose(0, 2, 1, 3)
        return x.reshape(B_HEADS // 2, r, LANES)

    qdf = lanes(jnp.exp(lf[:, None] * (idx + 1.0)))
    kdf = lanes(jnp.exp(lf[:, None] * (c - 1.0 - idx)))
    qdb = lanes(jnp.exp(lb[:, None] * (c - idx)))
    kdb = lanes(jnp.exp(lb[:, None] * idx))
    diff = idx[:, None] - idx[None, :]
    low = jnp.where(diff >= 0, jnp.exp(lf[:, None, None] * jnp.maximum(diff, 0.0)), 0.0)
    upp = jnp.where(diff <= 0, jnp.exp(lb[:, None, None] * jnp.maximum(-diff, 0.0)), 0.0)
    dm = low + upp
    cdf = lanes(jnp.exp(lf * c)[:, None])
    cdb = lanes(jnp.exp(lb * c)[:, None])
    r = jnp.arange(LANES)
    bd = ((r[:, None] < HEAD_DIM) == (r[None, :] < HEAD_DIM)).astype(f32)
    return qdf, qdb, kdf, kdb, dm, cdf, cdb, bd


def _blockdiag_states(s):
    b = s.shape[0]
    s = s.astype(f32).reshape(b, B_HEADS // 2, 2, HEAD_DIM, HEAD_DIM)
    z = jnp.zeros_like(s[:, :, 0])
    top = jnp.concatenate([s[:, :, 0], z], axis=-1)
    bot = jnp.concatenate([z, s[:, :, 1]], axis=-1)
    return jnp.concatenate([top, bot], axis=-2)


def _diag_states(s):
    b = s.shape[0]
    h0 = s[:, :, :HEAD_DIM, :HEAD_DIM]
    h1 = s[:, :, HEAD_DIM:, HEAD_DIM:]
    return jnp.stack([h0, h1], axis=2).reshape(b, B_HEADS, HEAD_DIM, HEAD_DIM)


def _retention(q, k, v, g, s0f, s0b, tables, batch):
    m = q.shape[0]
    t = m // batch
    w = B_HEADS * HEAD_DIM
    np_ = B_HEADS // 2
    c = RET_CHUNK
    kern = functools.partial(_ret_kernel, t=t)
    tok = pl.BlockSpec((t, w), lambda b: (b, 0))
    st = pl.BlockSpec((1, np_, LANES, LANES), lambda b: (b, 0, 0, 0))

    def full(a):
        nd = a.ndim
        return pl.BlockSpec(a.shape, lambda b: (0,) * nd)

    o, sf, sb = pl.pallas_call(
        kern,
        grid=(batch,),
        in_specs=[tok, tok, tok, tok, st, st] + [full(a) for a in tables],
        out_specs=[tok, st, st],
        out_shape=[jax.ShapeDtypeStruct((m, w), bf16),
                   jax.ShapeDtypeStruct((batch, np_, LANES, LANES), f32),
                   jax.ShapeDtypeStruct((batch, np_, LANES, LANES), f32)],
        scratch_shapes=[pltpu.VMEM((t // c, LANES, LANES), f32), pltpu.VMEM((t // c, LANES, LANES), f32)],
        compiler_params=_cparams(("arbitrary",)),
        name="retention",
    )(q, k, v, g, _blockdiag_states(s0f), _blockdiag_states(s0b), *tables)
    return o, _diag_states(sf), _diag_states(sb)


def _route(h, wr):
    lane = lax.broadcasted_iota(jnp.int32, (h.shape[0], LANES), 1).astype(f32)
    logits = jnp.full((h.shape[0], LANES), -jnp.inf, f32)
    for e in range(N_EXPERTS):
        logits = jnp.where(lane == e, (h * wr[e:e + 1, :]).sum(axis=-1, keepdims=True), logits)
    m1 = logits.max(axis=-1, keepdims=True)
    i1 = jnp.where(logits == m1, lane, float(LANES)).min(axis=-1, keepdims=True)
    rest = jnp.where(lane == i1, -jnp.inf, logits)
    m2 = rest.max(axis=-1, keepdims=True)
    i2 = jnp.where(rest == m2, lane, float(LANES)).min(axis=-1, keepdims=True)
    e2 = jnp.exp(m2 - m1)
    g1 = 1.0 / (1.0 + e2)
    g2 = e2 * g1
    return jnp.where(lane == 0.0, i1, jnp.where(lane == 1.0, i2, jnp.where(lane == 2.0, g1, jnp.where(lane == 3.0, g2, 0.0))))


def _outproj_kernel(a_ref, b_ref, w_ref, x_ref, gate_ref, g2_ref, sh_ref, sc_ref, *rest, has_router):
    if has_router:
        wr_ref, xo_ref, h_ref, info_ref, wb = rest
    else:
        xo_ref, h_ref, wb = rest
    _cast_weights_once(w_ref, wb)
    half = a_ref.shape[1]
    acc = _dot(a_ref[...], wb[:half, :]) + _dot(b_ref[...], wb[half:, :])
    xn = x_ref[...] + gate_ref[0] * acc
    xo_ref[...] = xn
    h = _norm_mod(xn, g2_ref[...], sh_ref[0], sc_ref[0])
    h_ref[...] = h.astype(h_ref.dtype)
    if has_router:
        info_ref[...] = _route(h, wr_ref[...])


def _outproj(a, a_blk, b, b_blk, w, x, g2, mod3, layer, rows_per_batch, is_sample, w_router=None, tm=1024):
    m = x.shape[0]
    tm = min(tm, m)
    half = w.shape[0] // 2
    has_router = w_router is not None
    kern = functools.partial(_outproj_kernel, has_router=has_router)
    row = pl.BlockSpec((tm, D), lambda i: (i, 0))
    in_specs = [pl.BlockSpec((tm, half), lambda i: (i, a_blk)),
                pl.BlockSpec((tm, half), lambda i: (i, b_blk)),
                pl.BlockSpec(w.shape, lambda i: (0, 0)),
                row,
                _mod_spec(layer, 2, tm, rows_per_batch, is_sample),
                pl.BlockSpec((1, D), lambda i: (0, 0)),
                _mod_spec(layer, 3, tm, rows_per_batch, is_sample),
                _mod_spec(layer, 4, tm, rows_per_batch, is_sample)]
    args = [a, b, w, x, mod3, g2, mod3, mod3]
    out_specs = [row, row]
    out_shape = [jax.ShapeDtypeStruct((m, D), f32), jax.ShapeDtypeStruct((m, D), f32 if has_router else bf16)]
    if has_router:
        in_specs.append(pl.BlockSpec((N_EXPERTS, D), lambda i: (0, 0)))
        args.append(w_router)
        out_specs.append(pl.BlockSpec((tm, LANES), lambda i: (i, 0)))
        out_shape.append(jax.ShapeDtypeStruct((m, LANES), f32))
    return pl.pallas_call(
        kern, grid=(m // tm,), in_specs=in_specs, out_specs=out_specs, out_shape=out_shape,
        scratch_shapes=[pltpu.VMEM(w.shape, bf16)],
        compiler_params=_cparams(("arbitrary",)), name="out_proj",
    )(*args)


def _rms(x, g):
    return x * lax.rsqrt(jnp.mean(x * x, axis=-1, keepdims=True) + EPS) * g


FFN_TM = 512
FFN_TF = 256


def _swiglu(h, wgu_ref, wd_ref, act):
    f = wd_ref.shape[0]
    for c0 in range(0, f, FFN_TF):
        g = _dot(h, wgu_ref[:, c0:c0 + FFN_TF])
        u = _dot(h, wgu_ref[:, f + c0:f + c0 + FFN_TF])
        act[:, c0:c0 + FFN_TF] = (g * jax.nn.sigmoid(g) * u).astype(bf16)
    return _dot(act[...], wd_ref[...])


def _ffn_kernel(*refs, final_norm):
    h_ref, wgu_ref, wd_ref, x_ref, gate_ref = refs[:5]
    rest = list(refs[5:])
    fg_ref = rest.pop(0) if final_norm else None
    o_ref, act = rest
    out = x_ref[...] + gate_ref[0] * _swiglu(h_ref[...], wgu_ref, wd_ref, act)
    o_ref[...] = _rms(out, fg_ref[...]) if final_norm else out


def _ffn(h, w_gu, w_down, x, mod3, layer, rows_per_batch, is_sample, final_g=None, tm=FFN_TM):
    m = x.shape[0]
    tm = min(tm, m)
    f = w_down.shape[0]
    final_norm = final_g is not None
    kern = functools.partial(_ffn_kernel, final_norm=final_norm)
    row = pl.BlockSpec((tm, D), lambda i: (i, 0))
    in_specs = [row,
                pl.BlockSpec((D, 2 * f), lambda i: (0, 0)),
                pl.BlockSpec((f, D), lambda i: (0, 0)),
                row,
                _mod_spec(layer, 5, tm, rows_per_batch, is_sample)]
    args = [h, w_gu, w_down, x, mod3]
    if final_norm:
        in_specs.append(pl.BlockSpec((1, D), lambda i: (0, 0)))
        args.append(final_g)
    return pl.pallas_call(
        kern, grid=(m // tm,), in_specs=in_specs, out_specs=row,
        out_shape=jax.ShapeDtypeStruct((m, D), f32),
        scratch_shapes=[pltpu.VMEM((tm, f), bf16)],
        compiler_params=_cparams(("arbitrary",)), name="ffn",
    )(*args)


MOE_TM = FFN_TM
DMA_UNROLL = 8


def _moe_pos_kernel(info_ref, pos_ref, tot_ref, carry, *, tb):
    ph = pl.program_id(0)
    i = pl.program_id(1)
    lane = lax.broadcasted_iota(jnp.int32, (1, LANES), 1).astype(f32)
    info = info_ref[...]
    oh1 = (lane == info[:, 0:1]).astype(f32)
    oh2 = (lane == info[:, 1:2]).astype(f32)
    cnt = oh1 + oh2

    @pl.when((ph == 0) & (i == 0))
    def _():
        carry[...] = jnp.zeros_like(carry)

    @pl.when(ph == 0)
    def _():
        carry[0:1, :] += cnt.sum(axis=0, keepdims=True)

    @pl.when((ph == 1) & (i == 0))
    def _():
        tot = carry[0:1, :]
        tot_ref[...] = jnp.broadcast_to(tot, tot_ref.shape)
        padded = jnp.floor((tot + (MOE_TM - 1)) * (1.0 / MOE_TM)) * MOE_TM
        start = jnp.zeros_like(tot)
        for e in range(N_EXPERTS - 1):
            pe = jnp.where(lane == e, padded, 0.0).sum(axis=-1, keepdims=True)
            start = start + jnp.where(lane > e, pe, 0.0)
        carry[1:2, :] = start

    @pl.when(ph == 1)
    def _():
        r = lax.broadcasted_iota(jnp.int32, (tb, tb), 0)
        c = lax.broadcasted_iota(jnp.int32, (tb, tb), 1)
        tri = jnp.where(r > c, 1.0, 0.0).astype(bf16)
        base = carry[1:2, :] + _dot(tri, cnt.astype(bf16))
        p1 = (oh1 * base).sum(axis=-1, keepdims=True)
        p2 = (oh2 * base).sum(axis=-1, keepdims=True)
        pos_ref[...] = jnp.where(lane == 0.0, p1, jnp.where(lane == 1.0, p2, 0.0)).astype(jnp.int32)
        carry[1:2, :] += cnt.sum(axis=0, keepdims=True)


def _moe_positions(info, tb=1024):
    m = info.shape[0]
    tb = min(tb, m)
    nblk = m // tb
    pos, tot = pl.pallas_call(
        functools.partial(_moe_pos_kernel, tb=tb),
        grid=(2, nblk),
        in_specs=[pl.BlockSpec((tb, LANES), lambda p, i: (i, 0))],
        out_specs=[pl.BlockSpec((tb, LANES), lambda p, i: (i * p, 0)),
                   pl.BlockSpec((8, LANES), lambda p, i: (0, 0))],
        out_shape=[jax.ShapeDtypeStruct((m, LANES), jnp.int32), jax.ShapeDtypeStruct((8, LANES), f32)],
        scratch_shapes=[pltpu.VMEM((8, LANES), f32)],
        compiler_params=_cparams(("arbitrary", "arbitrary")), name="moe_pos",
    )(info)
    return pos[:, :2].reshape(-1), tot[0, :N_EXPERTS].astype(jnp.int32)


def _row_copy(src, s_row, dst, d_row, sem):
    return pltpu.make_async_copy(src.at[pl.ds(s_row, 1)], dst.at[pl.ds(d_row, 1)], sem)


def _moe_dispatch_kernel(pos_ref, zrow_ref, zflag_ref, hp_ref, hs_ref, o_ref, zeros, sem, zsem, *, nbp, tb):
    i = pl.program_id(0)

    @pl.when(i == 0)
    def _():
        zeros[...] = jnp.zeros_like(zeros)

        def zero_tile(z):
            dst = o_ref.at[pl.ds(pl.multiple_of(zrow_ref[z], MOE_TM), MOE_TM)]
            return pltpu.make_async_copy(zeros, dst, zsem)

        for z in range(zrow_ref.shape[0]):
            @pl.when(zflag_ref[z] != 0)
            def _():
                zero_tile(z).start()
        for z in range(zrow_ref.shape[0]):
            @pl.when(zflag_ref[z] != 0)
            def _():
                zero_tile(z).wait()

    def scatter(src):
        def issue(r, _):
            g = 2 * (i * tb + r)
            _row_copy(src, r, o_ref, pos_ref[g], sem).start()
            _row_copy(src, r, o_ref, pos_ref[g + 1], sem).start(priority=1)
            return 0
        lax.fori_loop(0, tb, issue, 0, unroll=DMA_UNROLL)

        def drain(r, _):
            _row_copy(src, 0, o_ref, 0, sem).wait()
            _row_copy(src, 0, o_ref, 0, sem).wait()
            return 0
        lax.fori_loop(0, tb, drain, 0, unroll=DMA_UNROLL)

    @pl.when(i < nbp)
    def _():
        scatter(hp_ref)

    @pl.when(i >= nbp)
    def _():
        scatter(hs_ref)


def _moe_dispatch(pos, zrow, zflag, hp, hs, n_rows, tb=1024):
    mp, ms = hp.shape[0], hs.shape[0]
    tb = min(tb, mp, ms)
    nbp, nbs = mp // tb, ms // tb
    kern = functools.partial(_moe_dispatch_kernel, nbp=nbp, tb=tb)
    return pl.pallas_call(
        kern,
        grid_spec=pltpu.PrefetchScalarGridSpec(
            num_scalar_prefetch=3, grid=(nbp + nbs,),
            in_specs=[pl.BlockSpec((tb, D), lambda i, *_: (jnp.minimum(i, nbp - 1), 0)),
                      pl.BlockSpec((tb, D), lambda i, *_: (jnp.maximum(i - nbp, 0), 0))],
            out_specs=pl.BlockSpec(memory_space=pl.ANY),
            scratch_shapes=[pltpu.VMEM((MOE_TM, D), f32), pltpu.SemaphoreType.DMA(()),
                            pltpu.SemaphoreType.DMA(())]),
        out_shape=jax.ShapeDtypeStruct((n_rows, D), f32),
        compiler_params=_cparams(("arbitrary",)), name="moe_dispatch",
    )(pos, zrow, zflag, hp, hs)


HEAD_ROWS = 128


def _moe_ffn_kernel(te_ref, na_ref, h_ref, wgu_ref, wd_ref, *rest, n_heads_out):
    kv_in, kv_out, (o_ref, act) = rest[:n_heads_out], rest[n_heads_out:2 * n_heads_out], rest[2 * n_heads_out:]
    i = pl.program_id(0)

    @pl.when(i < na_ref[0])
    def _():
        for src, dst in zip(kv_in, kv_out):
            x = src[...]
            for hd in range(dst.shape[1]):
                dst[:, hd, :] = x[:, hd * HEAD_DIM:(hd + 1) * HEAD_DIM]
        o_ref[...] = _swiglu(h_ref[...].astype(bf16), wgu_ref.at[0], wd_ref.at[0], act)

    @pl.when(i >= na_ref[0])
    def _():
        o_ref[...] = jnp.zeros_like(o_ref)


def _moe_ffn(h_sorted, w_gu, w_down, tile_expert, n_active, heads_out=()):
    n_rows = h_sorted.shape[0]
    tm = MOE_TM
    f = w_down.shape[1]
    n_tiles = n_rows // tm
    kv_specs, kv_out_specs, kv_shapes = [], [], []
    for a in heads_out:
        nblk = a.shape[0] // HEAD_ROWS
        nh = a.shape[1] // HEAD_DIM
        assert nblk * HEAD_ROWS == a.shape[0] and nblk <= (n_rows // tm - N_EXPERTS), (a.shape, n_tiles)
        kv_specs.append(pl.BlockSpec((HEAD_ROWS, a.shape[1]), lambda i, te, na, nblk=nblk: (jnp.minimum(i, nblk - 1), 0)))
        kv_out_specs.append(pl.BlockSpec((HEAD_ROWS, nh, HEAD_DIM),
                                         lambda i, te, na, nblk=nblk: (jnp.minimum(i, nblk - 1), 0, 0)))
        kv_shapes.append(jax.ShapeDtypeStruct((a.shape[0], nh, HEAD_DIM), a.dtype))
    outs = pl.pallas_call(
        functools.partial(_moe_ffn_kernel, n_heads_out=len(heads_out)),
        grid_spec=pltpu.PrefetchScalarGridSpec(
            num_scalar_prefetch=2, grid=(n_tiles,),
            in_specs=[pl.BlockSpec((tm, D), lambda i, te, na: (jnp.minimum(i, na[0] - 1), 0)),
                      pl.BlockSpec((1, D, 2 * f), lambda i, te, na: (te[i], 0, 0)),
                      pl.BlockSpec((1, f, D), lambda i, te, na: (te[i], 0, 0))] + kv_specs,
            out_specs=kv_out_specs + [pl.BlockSpec((tm, D), lambda i, te, na: (i, 0))],
            scratch_shapes=[pltpu.VMEM((tm, f), bf16)]),
        out_shape=kv_shapes + [jax.ShapeDtypeStruct((n_rows, D), f32)],
        compiler_params=_cparams(("arbitrary",)), name="moe_ffn",
    )(tile_expert, n_active, h_sorted, w_gu, w_down, *heads_out)
    return outs[-1], outs[:-1]


def _moe_combine_kernel(pos_ref, info_ref, x_ref, gate_ref, fg_ref, y_ref, o_ref, ybuf, sem, *, tb, base,
                        final_norm):
    i = pl.program_id(0)
    n = pl.num_programs(0)

    def issue(blk, slot):
        def body(r, _):
            g = 2 * (base + blk * tb + r)
            _row_copy(y_ref, pos_ref[g], ybuf.at[slot, 0], r, sem.at[slot]).start()
            _row_copy(y_ref, pos_ref[g + 1], ybuf.at[slot, 1], r, sem.at[slot]).start(priority=1)
            return 0
        lax.fori_loop(0, tb, body, 0, unroll=DMA_UNROLL)

    @pl.when(i == 0)
    def _():
        issue(0, 0)

    @pl.when(i + 1 < n)
    def _():
        issue(i + 1, (i + 1) % 2)

    slot = i % 2

    def drain(r, _):
        _row_copy(y_ref, 0, ybuf.at[slot, 0], 0, sem.at[slot]).wait()
        _row_copy(y_ref, 0, ybuf.at[slot, 1], 0, sem.at[slot]).wait()
        return 0
    lax.fori_loop(0, tb, drain, 0, unroll=DMA_UNROLL)

    info = info_ref[...]
    y = info[:, 2:3] * ybuf[slot, 0] + info[:, 3:4] * ybuf[slot, 1]
    out = x_ref[...] + gate_ref[0] * y
    o_ref[...] = _rms(out, fg_ref[...]) if final_norm else out


def _moe_combine(pos, info, x, mod3, layer, y_sorted, base, rows_per_batch, is_sample, final_g, tb=512):
    m = x.shape[0]
    tb = min(tb, m)
    final_norm = final_g is not None
    if final_g is None:
        final_g = jnp.ones((1, D), f32)
    kern = functools.partial(_moe_combine_kernel, tb=tb, base=base, final_norm=final_norm)
    return pl.pallas_call(
        kern,
        grid_spec=pltpu.PrefetchScalarGridSpec(
            num_scalar_prefetch=1, grid=(m // tb,),
            in_specs=[pl.BlockSpec((tb, LANES), lambda i, p: (i, 0)),
                      pl.BlockSpec((tb, D), lambda i, p: (i, 0)),
                      _mod_spec(layer, 5, tb, rows_per_batch, is_sample),
                      pl.BlockSpec((1, D), lambda i, p: (0, 0)),
                      pl.BlockSpec(memory_space=pl.ANY)],
            out_specs=pl.BlockSpec((tb, D), lambda i, p: (i, 0)),
            scratch_shapes=[pltpu.VMEM((2, 2, tb, D), f32), pltpu.SemaphoreType.DMA((2,))]),
        out_shape=jax.ShapeDtypeStruct((m, D), f32),
        compiler_params=_cparams(("arbitrary",)), name="moe_combine",
    )(pos, info, x, mod3, final_g, y_sorted)


def _moe(hp, hs, info_p, info_s, xp, xs, w_gu, w_down, mod3, layer, tp, ts, final_g, heads_out=()):
    mp, ms = hp.shape[0], hs.shape[0]
    tm = MOE_TM
    n_tiles = (2 * (mp + ms)) // tm + N_EXPERTS
    pos, counts = _moe_positions(jnp.concatenate([info_p, info_s], axis=0))
    tiles_per = (counts + tm - 1) // tm
    cum = jnp.cumsum(tiles_per)
    n_active = cum[-1:].astype(jnp.int32)
    ids = jnp.arange(n_tiles, dtype=jnp.int32)
    te = jnp.minimum((ids[:, None] >= cum[None, :]).astype(jnp.int32).sum(axis=1), N_EXPERTS - 1)
    last_e = te[jnp.maximum(n_active[0] - 1, 0)]
    te = jnp.where(ids < n_active[0], te, last_e)
    tail = jnp.arange(n_tiles - N_EXPERTS, n_tiles, dtype=jnp.int32)
    zrow = jnp.concatenate([jnp.maximum(cum - 1, 0), tail]).astype(jnp.int32) * tm
    zflag = jnp.concatenate([counts % tm != 0, tail >= n_active[0]]).astype(jnp.int32)
    h_sorted = _moe_dispatch(pos, zrow, zflag, hp, hs, n_tiles * tm)
    y_sorted, by_head = _moe_ffn(h_sorted, w_gu, w_down, te, n_active, heads_out)
    yp = _moe_combine(pos, info_p, xp, mod3, layer, y_sorted, 0, tp, False, final_g)
    ys = _moe_combine(pos, info_s, xs, mod3, layer, y_sorted, mp, ts, True, final_g)
    return yp, ys, by_head


def kernel(x_prompt, x_sample, cache_a_k, cache_a_v, state_ret_fwd, state_ret_bwd, cache_c_k, cache_c_v, c, c_ctx,
           mod_w, mod_b, norm1_g, norm2_g, final_g, ev_w_in, ev_w_out, ev_sink, ev_ret_logit_fwd, ev_ret_logit_bwd,
           ev_ffn_w_gu, ev_ffn_w_down, od_w_qkv, od_w_out, od_rpb, od_w_router, od_moe_w_gu, od_moe_w_down):
    bp, tp, _ = x_prompt.shape
    bs, ts, _ = x_sample.shape
    depth = mod_w.shape[0]
    assert 1 + bs <= MOD_ROWS
    xp = x_prompt.reshape(bp * tp, D)
    xs = x_sample.reshape(bs * ts, D)

    cvecs = jnp.zeros((MOD_ROWS, D), f32).at[0].set(c_ctx).at[1:1 + bs].set(c)
    mod3 = _adaln(cvecs, mod_w, mod_b)

    a_q = A_HEADS * HEAD_DIM
    a_kv = A_KV_HEADS * HEAD_DIM
    b_w = B_HEADS * HEAD_DIM
    c_w = C_HEADS * HEAD_DIM
    streams = ((False, tp), (True, ts))
    outs = {}
    pending = []
    for i in range(depth):
        j = i // 2
        g1 = norm1_g[i].reshape(1, D)
        g2 = norm2_g[i].reshape(1, D)
        last = i == depth - 1
        fg = final_g.reshape(1, D) if last else None
        if i % 2 == 0:
            w_in = ev_w_in[j]
            w_out = ev_w_out[j]
            splits = (a_q, a_kv, a_kv, b_w, b_w, b_w, b_w)
            dts_p = (bf16, f32, f32, f32, f32, bf16, f32)
            dts_s = (f32, f32, f32, f32, f32, bf16, f32)
            tables = _ret_tables(ev_ret_logit_fwd[j], ev_ret_logit_bwd[j])
            qa_p, ka_p, va_p, qr_p, kr_p, vr_p, gr_p, w_gu, w_dn = _proj(
                xp, g1, mod3, i, w_in, splits, dts_p, tp, False, cast=(ev_ffn_w_gu[j], ev_ffn_w_down[j]))
            qa_s, ka_s, va_s, qr_s, kr_s, vr_s, gr_s = _proj(xs, g1, mod3, i, w_in, splits, dts_s, ts, True)
            oa_p = _ctx_attn(qa_p, ka_p, va_p, ev_sink[j], bp, A_HEADS, A_KV_HEADS)
            zero = jnp.zeros((bp, B_HEADS, HEAD_DIM, HEAD_DIM), f32)
            or_p, sf, sb = _retention(qr_p, kr_p, vr_p, gr_p, zero, zero, tables, bp)
            lc = cache_a_k.shape[2]
            oa_s = _win_attn(qa_s, ka_s, va_s, cache_a_k[:, j].reshape(bs, lc, a_kv),
                             cache_a_v[:, j].reshape(bs, lc, a_kv), ev_sink[j], bs)
            or_s, _, _ = _retention(qr_s, kr_s, vr_s, gr_s, state_ret_fwd[:, j], state_ret_bwd[:, j], tables, bs)
            pending += [('a_k', ka_p, A_KV_HEADS), ('a_v', va_p, A_KV_HEADS)]
            outs.setdefault('r_f', []).append(sf)
            outs.setdefault('r_b', []).append(sb)
            xp, hp = _outproj(oa_p, 0, or_p, 0, w_out, xp, g2, mod3, i, tp, False)
            xs, hs = _outproj(oa_s, 0, or_s, 0, w_out, xs, g2, mod3, i, ts, True)
            xp = _ffn(hp, w_gu, w_dn, xp, mod3, i, tp, False, final_g=fg)
            xs = _ffn(hs, w_gu, w_dn, xs, mod3, i, ts, True, final_g=fg)
        else:
            w_qkv = od_w_qkv[j]
            w_out = od_w_out[j]
            w_router = od_w_router[j].astype(f32).T
            splits = (c_w, c_w, c_w)
            q_p, k_p, v_p = _proj(xp, g1, mod3, i, w_qkv, splits, (bf16, f32, f32), tp, False)
            q_s, k_s, v_s = _proj(xs, g1, mod3, i, w_qkv, splits, (bf16, bf16, bf16), ts, True)
            o_p = _ctx_attn(q_p, k_p, v_p, None, bp, C_HEADS, C_HEADS)
            lc = cache_c_k.shape[2]
            w_gu, w_dn = od_moe_w_gu[j], od_moe_w_down[j]
            o_s, (w_gu_b, w_dn_b) = _nb_attn(
                q_s, k_s, v_s, cache_c_k[:, j].reshape(bs, lc, c_w), cache_c_v[:, j].reshape(bs, lc, c_w),
                od_rpb[j], bs, cast=(w_gu.reshape(-1, w_gu.shape[-1]), w_dn.reshape(-1, w_dn.shape[-1])))
            xp, hp, info_p = _outproj(o_p, 0, o_p, 1, w_out, xp, g2, mod3, i, tp, False, w_router=w_router)
            xs, hs, info_s = _outproj(o_s, 0, o_s, 1, w_out, xs, g2, mod3, i, ts, True, w_router=w_router)
            pending += [('c_k', k_p, C_HEADS), ('c_v', v_p, C_HEADS)]
            xp, xs, by_head = _moe(hp, hs, info_p, info_s, xp, xs, w_gu_b.reshape(w_gu.shape),
                                   w_dn_b.reshape(w_dn.shape), mod3, i, tp, ts, fg,
                                   heads_out=tuple(a for _, a, _ in pending))
            for (name, _, nh), a in zip(pending, by_head):
                outs.setdefault(name, []).append(a.reshape(bp, tp, nh, HEAD_DIM))
            pending = []
    for name, a, nh in pending:
        outs.setdefault(name, []).append(a.reshape(bp, tp, nh, HEAD_DIM))
    y_prompt = xp.reshape(bp, tp, D)
    y_sample = xs.reshape(bs, ts, D)
    def per_layer(name):
        xs_ = outs[name]
        return xs_[0][:, None] if len(xs_) == 1 else jnp.stack(xs_, axis=1)

    return (y_prompt, y_sample, per_layer('a_k'), per_layer('a_v'), per_layer('r_f'), per_layer('r_b'),
            per_layer('c_k'), per_layer('c_v'))
```
